```python
import math
import jax, jax.numpy as jnp
from jax import lax
import numpy as np

D_MODEL = 2048
BATCH = 1
SEQ = 16384
DEPTH = 1

A_HEADS = 16
A_KV_HEADS = 4
A_HEAD_DIM = 64
A_GROUP = A_HEADS // A_KV_HEADS
WINDOW = 128
BAND_BLOCK = 128
B_HEADS = 16
B_Q_LORA = 512
B_KV_LORA = 256
B_NOPE = 64
B_ROPE = 32
B_V = 64
ROPE_THETA = 10000.0
Q_BLOCK = 128
D_FF = 4 * D_MODEL
EPS = 1e-6
NEG_INF = -1e30

A_Q_W = A_HEADS * A_HEAD_DIM
A_KV_W = A_KV_HEADS * A_HEAD_DIM
A_OUT_W = A_HEADS * A_HEAD_DIM
B_OUT_W = B_HEADS * B_V
IN_SPLITS = (A_Q_W, A_KV_W, A_KV_W, B_Q_LORA, B_KV_LORA, B_ROPE, D_MODEL, D_MODEL)
IN_WIDTH = int(sum(IN_SPLITS))
IN_OFFSETS = tuple(int(v) for v in np.cumsum(IN_SPLITS)[:-1])

kernel_name = "gated_swa_mla_hybrid_encoder"


def rms_norm(x, g):
    xf = x.astype(jnp.float32)
    y = xf * lax.rsqrt(jnp.mean(xf * xf, axis=-1, keepdims=True) + EPS)
    return (y * g.astype(jnp.float32)).astype(x.dtype)


def alibi_slopes(n_heads):
    return jnp.exp2(-8.0 * (jnp.arange(n_heads, dtype=jnp.float32) + 1.0) / n_heads)


def apply_rope(x, pos):
    half = x.shape[-1] // 2
    inv = ROPE_THETA ** (-jnp.arange(half, dtype=jnp.float32) / half)
    ang = pos.astype(jnp.float32)[..., None] * inv
    cos = jnp.cos(ang)[:, :, None, :]
    sin = jnp.sin(ang)[:, :, None, :]
    x1 = x[..., :half].astype(jnp.float32)
    x2 = x[..., half:].astype(jnp.float32)
    out = jnp.concatenate([x1 * cos - x2 * sin, x2 * cos + x1 * sin], axis=-1)
    return out.astype(x.dtype)


def band_blocks(t, n_blocks):
    b = t.shape[0]
    pad = [(0, 0), (BAND_BLOCK, BAND_BLOCK)] + [(0, 0)] * (t.ndim - 2)
    tb = jnp.pad(t, pad).reshape(b, n_blocks + 2, BAND_BLOCK, *t.shape[2:])
    return jnp.concatenate([tb[:, :-2], tb[:, 1:-1], tb[:, 2:]], axis=2)


def windowed_gqa_sink(q, k, v, pos, sink):
    b, s = q.shape[0], q.shape[1]
    nb = s // BAND_BLOCK
    qb = q.reshape(b, nb, BAND_BLOCK, A_KV_HEADS, A_GROUP, A_HEAD_DIM)
    kb = band_blocks(k, nb)
    vb = band_blocks(v, nb)
    r = jnp.arange(BAND_BLOCK, dtype=jnp.int32)[:, None]
    c = jnp.arange(3 * BAND_BLOCK, dtype=jnp.int32)[None, :]
    in_win = jnp.abs(BAND_BLOCK + r - c) <= WINDOW
    kglob = (jnp.arange(nb, dtype=jnp.int32)[:, None, None] - 1) * BAND_BLOCK + c[None]
    mask = in_win[None] & (kglob >= 0) & (kglob < s)
    pq = pos.reshape(b, nb, BAND_BLOCK)
    pk = band_blocks(pos, nb)
    dist = jnp.abs(pq[..., :, None] - pk[..., None, :]).astype(jnp.float32)
    slopes = alibi_slopes(A_HEADS).reshape(A_KV_HEADS, A_GROUP)
    scale = A_HEAD_DIM ** -0.5
    scores = jnp.einsum('bnqhgd,bnchd->bnhgqc', qb, kb).astype(jnp.float32) * scale
    scores = scores - slopes[None, None, :, :, None, None] * dist[:, :, None, None]
    scores = jnp.where(mask[None, :, None, None], scores, NEG_INF)
    sk = sink.astype(jnp.float32).reshape(A_KV_HEADS, A_GROUP)[None, None, :, :, None, None]
    m = jnp.maximum(jnp.max(scores, axis=-1, keepdims=True), sk)
    p = jnp.exp(scores - m)
    denom = jnp.sum(p, axis=-1, keepdims=True) + jnp.exp(sk - m)
    probs = (p / denom).astype(v.dtype)
    out = jnp.einsum('bnhgqc,bnchd->bnqhgd', probs, vb)
    return out.reshape(b, s, A_OUT_W)


def mla_attention(c_q, c_kv, k_r, pos, g_q, g_kv, w_uq, w_uk, w_uv):
    b, s = c_q.shape[0], c_q.shape[1]
    c_q = rms_norm(c_q, g_q)
    c_kv = rms_norm(c_kv, g_kv)
    q = jnp.einsum('bsc,chd->bshd', c_q, w_uq)
    q_nope = q[..., :B_NOPE]
    q_rope = apply_rope(q[..., B_NOPE:], pos)
    k_nope = jnp.einsum('bsc,chd->bshd', c_kv, w_uk)
    val = jnp.einsum('bsc,chd->bshd', c_kv, w_uv)
    k_rope = apply_rope(k_r[:, :, None, :], pos)[:, :, 0]
    scale = (B_NOPE + B_ROPE) ** -0.5
    nb = s // Q_BLOCK
    qn = q_nope.reshape(b, nb, Q_BLOCK, B_HEADS, B_NOPE).transpose(1, 0, 2, 3, 4)
    qr = q_rope.reshape(b, nb, Q_BLOCK, B_HEADS, B_ROPE).transpose(1, 0, 2, 3, 4)

    def query_block(args):
        qn_b, qr_b = args
        sc = (jnp.einsum('bqhd,bkhd->bhqk', qn_b, k_nope)
              + jnp.einsum('bqhr,bkr->bhqk', qr_b, k_rope)).astype(jnp.float32) * scale
        p = jax.nn.softmax(sc, axis=-1).astype(val.dtype)
        return jnp.einsum('bhqk,bkhd->bqhd', p, val)

    o = lax.map(query_block, (qn, qr))
    return o.transpose(1, 0, 2, 3, 4).reshape(b, s, B_OUT_W)


def setup_inputs(seed: int = 0) -> dict:
    key = jax.random.key(seed)
    ks = jax.random.split(key, 20)

    def w(k, shape, fan_in):
        return jax.random.normal(k, shape, jnp.float32) * (fan_in ** -0.5)

    def gain(k, shape):
        return 1.0 + 0.02 * jax.random.normal(k, shape, jnp.float32)

    x = jax.random.normal(ks[0], (BATCH, SEQ, D_MODEL), jnp.float32)
    positions = jnp.broadcast_to(jnp.arange(SEQ, dtype=jnp.int32)[None, :], (BATCH, SEQ))
    return {
        "x": x,
        "positions": positions,
        "attn_norm_g": gain(ks[1], (DEPTH, D_MODEL)),
        "w_in": w(ks[2], (DEPTH, D_MODEL, IN_WIDTH), D_MODEL),
        "a_sink": 0.5 * jax.random.normal(ks[3], (DEPTH, A_HEADS), jnp.float32),
        "b_q_norm_g": gain(ks[4], (DEPTH, B_Q_LORA)),
        "b_kv_norm_g": gain(ks[5], (DEPTH, B_KV_LORA)),
        "b_w_uq": w(ks[6], (DEPTH, B_Q_LORA, B_HEADS, B_NOPE + B_ROPE), B_Q_LORA),
        "b_w_uk": w(ks[7], (DEPTH, B_KV_LORA, B_HEADS, B_NOPE), B_KV_LORA),
        "b_w_uv": w(ks[8], (DEPTH, B_KV_LORA, B_HEADS, B_V), B_KV_LORA),
        "w_branch_a": w(ks[9], (DEPTH, A_OUT_W, D_MODEL), A_OUT_W),
        "w_branch_b": w(ks[10], (DEPTH, B_OUT_W, D_MODEL), B_OUT_W),
        "w_out": w(ks[11], (DEPTH, D_MODEL, D_MODEL), D_MODEL),
        "mlp_norm_g": gain(ks[12], (DEPTH, D_MODEL)),
        "w_mlp_in": w(ks[13], (DEPTH, D_MODEL, D_FF), D_MODEL),
        "w_mlp_out": w(ks[14], (DEPTH, D_FF, D_MODEL), D_FF),
        "final_norm_g": gain(ks[15], (D_MODEL,)),
    }


def reference(x, positions, attn_norm_g, w_in, a_sink, b_q_norm_g, b_kv_norm_g,
              b_w_uq, b_w_uk, b_w_uv, w_branch_a, w_branch_b, w_out,
              mlp_norm_g, w_mlp_in, w_mlp_out, final_norm_g):
    b, s, _ = x.shape
    for l in range(DEPTH):
        h = rms_norm(x, attn_norm_g[l])
        proj = jnp.einsum('bsd,de->bse', h, w_in[l])
        qa, ka, va, c_q, c_kv, k_r, gate_a, gate_b = jnp.split(proj, IN_OFFSETS, axis=-1)
        o_a = windowed_gqa_sink(
            qa.reshape(b, s, A_HEADS, A_HEAD_DIM),
            ka.reshape(b, s, A_KV_HEADS, A_HEAD_DIM),
            va.reshape(b, s, A_KV_HEADS, A_HEAD_DIM),
            positions, a_sink[l])
        o_b = mla_attention(c_q, c_kv, k_r, positions, b_q_norm_g[l], b_kv_norm_g[l],
                            b_w_uq[l], b_w_uk[l], b_w_uv[l])
        merged = (jax.nn.sigmoid(gate_a) * jnp.einsum('bse,ed->bsd', o_a, w_branch_a[l])
                  + jax.nn.sigmoid(gate_b) * jnp.einsum('bse,ed->bsd', o_b, w_branch_b[l]))
        x = x + jnp.einsum('bsd,de->bse', merged, w_out[l])
        h2 = rms_norm(x, mlp_norm_g[l])
        u = jax.nn.relu(jnp.einsum('bsd,df->bsf', h2, w_mlp_in[l]))
        x = x + jnp.einsum('bsf,fd->bsd', u * u, w_mlp_out[l])
    return rms_norm(x, final_norm_g)
```

```python
import functools
import math

import jax
import jax.numpy as jnp
import numpy as np
from jax import lax
from jax.experimental import pallas as pl
from jax.experimental.pallas import tpu as pltpu

F32 = jnp.float32
BF16 = jnp.bfloat16

EPS = 1e-6
NEG_INF = -1e30
ROPE_THETA = 10000.0

A_HEADS = 16
A_KV_HEADS = 4
A_GROUP = A_HEADS // A_KV_HEADS
A_HEAD_DIM = 64
WINDOW = 128
BAND = 128

B_HEADS = 16
B_NOPE = 64
B_ROPE = 32
B_V = 64
B_HALF = B_ROPE // 2
HEAD_PAD = 128

VMEM_LIMIT = 52 * 1024 * 1024

N_ATT = 1536
N_LAT = 1024
N_GATE = 4096
IN_TN = 512
J_ATT = N_ATT // IN_TN
J_LAT = N_LAT // IN_TN
J_GATE = N_GATE // IN_TN


def _rms(x, g):
    ms = jnp.mean(x * x, axis=-1, keepdims=True)
    return x * lax.rsqrt(ms + EPS) * g


def _dot(a, b):
    return jnp.dot(a, b, preferred_element_type=F32)


def _dot_nt(a, b):
    return lax.dot_general(a, b, (((1,), (1,)), ((), ())), preferred_element_type=F32)


def _dot_tn(a, b):
    return lax.dot_general(a, b, (((0,), (0,)), ((), ())), preferred_element_type=F32)


def _in_proj_kernel(x_ref, g_ref, w_ref, att_ref, lat_ref, gate_ref, h_ref):
    j = pl.program_id(1)

    @pl.when(j == 0)
    def _():
        h_ref[...] = _rms(x_ref[...], g_ref[...]).astype(BF16)

    y = _dot(h_ref[...], w_ref[...])

    @pl.when(j < J_ATT)
    def _():
        att_ref[...] = y.astype(BF16)

    @pl.when((j >= J_ATT) & (j < J_ATT + J_LAT))
    def _():
        lat_ref[...] = y

    @pl.when(j >= J_ATT + J_LAT)
    def _():
        gate_ref[...] = jax.nn.sigmoid(y).astype(BF16)


def _in_proj(x, g, w, tm=512):
    s, d = x.shape
    n = w.shape[1]
    nj = n // IN_TN
    return pl.pallas_call(
        _in_proj_kernel,
        grid=(s // tm, nj),
        in_specs=[
            pl.BlockSpec((tm, d), lambda i, j: (i, 0)),
            pl.BlockSpec((1, d), lambda i, j: (0, 0)),
            pl.BlockSpec((d, IN_TN), lambda i, j: (0, j)),
        ],
        out_specs=[
            pl.BlockSpec((tm, IN_TN), lambda i, j: (i, jnp.minimum(j, J_ATT - 1))),
            pl.BlockSpec((tm, IN_TN), lambda i, j: (i, jnp.clip(j - J_ATT, 0, J_LAT - 1))),
            pl.BlockSpec((tm, IN_TN), lambda i, j: (i, jnp.clip(j - J_ATT - J_LAT, 0, J_GATE - 1))),
        ],
        out_shape=[
            jax.ShapeDtypeStruct((s, N_ATT), BF16),
            jax.ShapeDtypeStruct((s, N_LAT), F32),
            jax.ShapeDtypeStruct((s, N_GATE), BF16),
        ],
        scratch_shapes=[pltpu.VMEM((tm, d), BF16)],
        compiler_params=pltpu.CompilerParams(
            dimension_semantics=("arbitrary", "arbitrary"), vmem_limit_bytes=VMEM_LIMIT),
        name="in_proj",
    )(x, g, w)


def _mla_prep_kernel(cq_ref, ckv_ref, kr_ref, posc_ref, posr_ref, gq_ref, gkv_ref,
                     wq_ref, wqr_ref, wk_ref, wkr_ref, wv_ref, invc_ref, invr_ref,
                     qt_ref, k_ref, vt_ref):
    cqn = _rms(cq_ref[...], gq_ref[...]).astype(BF16)
    ckvn = _rms(ckv_ref[...], gkv_ref[...]).astype(BF16)
    tm = cqn.shape[0]

    qt = _dot_nt(wq_ref[...], cqn)
    qrt = _dot_nt(wqr_ref[...], cqn)
    ang_t = invc_ref[...] * posr_ref[...].astype(F32)
    cos_t = jnp.cos(ang_t)
    sin_t = jnp.sin(ang_t)
    zeros = jnp.zeros((HEAD_PAD - B_NOPE - B_ROPE, tm), BF16)
    for h in range(B_HEADS):
        r0 = h * HEAD_PAD
        qt_ref[r0:r0 + B_NOPE, :] = qt[r0:r0 + B_NOPE].astype(BF16)
        rope = (qt[r0 + B_NOPE:r0 + B_NOPE + B_ROPE] * cos_t
                + qrt[h * B_ROPE:(h + 1) * B_ROPE] * sin_t)
        qt_ref[r0 + B_NOPE:r0 + B_NOPE + B_ROPE, :] = rope.astype(BF16)
        qt_ref[r0 + B_NOPE + B_ROPE:r0 + HEAD_PAD, :] = zeros

    ang = posc_ref[...].astype(F32) * invr_ref[...]
    lane = lax.broadcasted_iota(jnp.int32, ang.shape, 1)
    cs = jnp.where(lane < B_ROPE, jnp.cos(ang), jnp.where(lane < 2 * B_ROPE, jnp.sin(ang), 0.0))
    u = kr_ref[...] * cs
    k_rope = u + pltpu.roll(u, HEAD_PAD - B_ROPE, 1)
    kfull = _dot(ckvn, wk_ref[...]) + _dot(k_rope.astype(BF16), wkr_ref[...])
    k_ref[...] = kfull.astype(BF16)

    vt_ref[0] = _dot_nt(wv_ref[...], ckvn).astype(BF16)


def _mla_prep(lat, pos_col, pos_row, gq, gkv, wq_t, wqr_t, wk_ext, wk_rope, wv_t, inv_col, inv_row, tm):
    s = lat.shape[0]
    nq = B_HEADS * HEAD_PAD
    nv = B_HEADS * B_V
    const = lambda i: (0, 0)
    return pl.pallas_call(
        _mla_prep_kernel,
        grid=(s // tm,),
        in_specs=[
            pl.BlockSpec((tm, 512), lambda i: (i, 0)),
            pl.BlockSpec((tm, 256), lambda i: (i, 2)),
            pl.BlockSpec((tm, 128), lambda i: (i, 6)),
            pl.BlockSpec((tm, 1), lambda i: (i, 0)),
            pl.BlockSpec((1, tm), lambda i: (0, i)),
            pl.BlockSpec((1, 512), const),
            pl.BlockSpec((1, 256), const),
            pl.BlockSpec(wq_t.shape, const),
            pl.BlockSpec(wqr_t.shape, const),
            pl.BlockSpec(wk_ext.shape, const),
            pl.BlockSpec(wk_rope.shape, const),
            pl.BlockSpec(wv_t.shape, const),
            pl.BlockSpec(inv_col.shape, const),
            pl.BlockSpec(inv_row.shape, const),
        ],
        out_specs=[
            pl.BlockSpec((nq, tm), lambda i: (0, i)),
            pl.BlockSpec((tm, nq), lambda i: (i, 0)),
            pl.BlockSpec((1, nv, tm), lambda i: (i, 0, 0)),
        ],
        out_shape=[
            jax.ShapeDtypeStruct((nq, s), BF16),
            jax.ShapeDtypeStruct((s, nq), BF16),
            jax.ShapeDtypeStruct((s // tm, nv, tm), BF16),
        ],
        compiler_params=pltpu.CompilerParams(
            dimension_semantics=("arbitrary",), vmem_limit_bytes=VMEM_LIMIT),
        name="mla_prep",
    )(lat, lat, lat, pos_col, pos_row, gq, gkv, wq_t, wqr_t, wk_ext, wk_rope, wv_t, inv_col, inv_row)


def _mla_attn_kernel(q_ref, k_ref, v_ref, o_ref, *, tk, nkb):
    q = q_ref[...]
    tq = q.shape[1]

    def body(kb, carry):
        m, l, acc = carry
        k = k_ref[pl.ds(pl.multiple_of(kb * tk, tk), tk), :]
        st = _dot(k, q)
        m_new = jnp.maximum(m, jnp.max(st, axis=0, keepdims=True))
        alpha = jnp.exp2(m - m_new)
        p = jnp.exp2(st - m_new)
        l = l * alpha + jnp.sum(p, axis=0, keepdims=True)
        acc = acc * alpha + _dot(v_ref[kb], p.astype(BF16))
        return m_new, l, acc

    m0 = jnp.full((1, tq), NEG_INF, F32)
    l0 = jnp.zeros((1, tq), F32)
    a0 = jnp.zeros((B_V, tq), F32)
    _, l, acc = lax.fori_loop(0, nkb, body, (m0, l0, a0))
    o_ref[...] = (acc / l).astype(BF16)


def _mla_attn(qt, kfull, vt, tq, tk):
    s = kfull.shape[0]
    nkb = s // tk
    return pl.pallas_call(
        functools.partial(_mla_attn_kernel, tk=tk, nkb=nkb),
        grid=(B_HEADS, s // tq),
        in_specs=[
            pl.BlockSpec((HEAD_PAD, tq), lambda h, i: (h, i)),
            pl.BlockSpec((s, HEAD_PAD), lambda h, i: (0, h)),
            pl.BlockSpec((nkb, B_V, tk), lambda h, i: (0, h, 0)),
        ],
        out_specs=pl.BlockSpec((B_V, tq), lambda h, i: (h, i)),
        out_shape=jax.ShapeDtypeStruct((B_HEADS * B_V, s), BF16),
        compiler_params=pltpu.CompilerParams(
            dimension_semantics=("arbitrary", "arbitrary"), vmem_limit_bytes=VMEM_LIMIT),
        name="mla_attn",
    )(qt, kfull, vt)


def _win_attn_kernel(sink_ref, q_ref, kvp_ref, kvc_ref, kvn_ref, pq_ref, pkp_ref, pkc_ref, pkn_ref,
                     o_ref, *, seq, slopes):
    i = pl.program_id(0)
    q = q_ref[...]
    kv = jnp.concatenate([kvp_ref[...], kvc_ref[...], kvn_ref[...]], axis=0)
    pk = jnp.concatenate([pkp_ref[...], pkc_ref[...], pkn_ref[...]], axis=1)
    dist = jnp.abs(pq_ref[...] - pk).astype(F32)
    r = lax.broadcasted_iota(jnp.int32, dist.shape, 0)
    c = lax.broadcasted_iota(jnp.int32, dist.shape, 1)
    kglob = (i - 1) * BAND + c
    mask = (jnp.abs(BAND + r - c) <= WINDOW) & (kglob >= 0) & (kglob < seq)
    kw = A_KV_HEADS * A_HEAD_DIM
    outs = []
    for h in range(A_HEADS):
        g = h // A_GROUP
        qh = q[:, h * A_HEAD_DIM:(h + 1) * A_HEAD_DIM]
        kh = kv[:, g * A_HEAD_DIM:(g + 1) * A_HEAD_DIM]
        vh = kv[:, kw + g * A_HEAD_DIM:kw + (g + 1) * A_HEAD_DIM]
        sc = _dot_nt(qh, kh) - slopes[h] * dist
        sc = jnp.where(mask, sc, NEG_INF)
        sk = sink_ref[h]
        m = jnp.maximum(jnp.max(sc, axis=-1, keepdims=True), sk)
        p = jnp.exp(sc - m)
        denom = jnp.sum(p, axis=-1, keepdims=True) + jnp.exp(sk - m)
        outs.append(_dot((p / denom).astype(BF16), vh))
    o_ref[...] = jnp.concatenate(outs, axis=-1).astype(BF16)


def _win_attn(att, pos_col, pos_row, sink):
    s = att.shape[0]
    nb = s // BAND
    slopes = tuple(float(2.0 ** (-8.0 * (h + 1) / A_HEADS)) for h in range(A_HEADS))
    kvcol = (A_HEADS * A_HEAD_DIM) // (2 * A_KV_HEADS * A_HEAD_DIM)
    prev = lambda i, sk: jnp.maximum(i - 1, 0)
    nxt = lambda i, sk: jnp.minimum(i + 1, nb - 1)
    grid_spec = pltpu.PrefetchScalarGridSpec(
        num_scalar_prefetch=1,
        grid=(nb,),
        in_specs=[
            pl.BlockSpec((BAND, A_HEADS * A_HEAD_DIM), lambda i, sk: (i, 0)),
            pl.BlockSpec((BAND, 512), lambda i, sk: (prev(i, sk), kvcol)),
            pl.BlockSpec((BAND, 512), lambda i, sk: (i, kvcol)),
            pl.BlockSpec((BAND, 512), lambda i, sk: (nxt(i, sk), kvcol)),
            pl.BlockSpec((BAND, 1), lambda i, sk: (i, 0)),
            pl.BlockSpec((1, BAND), lambda i, sk: (0, prev(i, sk))),
            pl.BlockSpec((1, BAND), lambda i, sk: (0, i)),
            pl.BlockSpec((1, BAND), lambda i, sk: (0, nxt(i, sk))),
        ],
        out_specs=pl.BlockSpec((BAND, A_HEADS * A_HEAD_DIM), lambda i, sk: (i, 0)),
    )
    return pl.pallas_call(
        functools.partial(_win_attn_kernel, seq=s, slopes=slopes),
        grid_spec=grid_spec,
        out_shape=jax.ShapeDtypeStruct((s, A_HEADS * A_HEAD_DIM), BF16),
        compiler_params=pltpu.CompilerParams(
            dimension_semantics=("arbitrary",), vmem_limit_bytes=VMEM_LIMIT),
        name="win_attn",
    )(sink, att, att, att, att, pos_col, pos_row, pos_row, pos_row)


def _merge_out_kernel(x_ref, oa_ref, obt_ref, ga_ref, gb_ref, wa_ref, wb_ref, wo_ref, g2_ref,
                      x1_ref, h2_ref):
    ya = _dot(oa_ref[...], wa_ref[...])
    yb = _dot_tn(obt_ref[...], wb_ref[...])
    merged = ga_ref[...].astype(F32) * ya + gb_ref[...].astype(F32) * yb
    x1 = x_ref[...] + _dot(merged.astype(BF16), wo_ref[...])
    x1_ref[...] = x1
    h2_ref[...] = _rms(x1, g2_ref[...]).astype(BF16)


def _merge_out(x, oa, obt, gates, wa, wb, wo, g2, tm=256):
    s, d = x.shape
    const = lambda i: (0, 0)
    return pl.pallas_call(
        _merge_out_kernel,
        grid=(s // tm,),
        in_specs=[
            pl.BlockSpec((tm, d), lambda i: (i, 0)),
            pl.BlockSpec((tm, oa.shape[1]), lambda i: (i, 0)),
            pl.BlockSpec((obt.shape[0], tm), lambda i: (0, i)),
            pl.BlockSpec((tm, d), lambda i: (i, 0)),
            pl.BlockSpec((tm, d), lambda i: (i, 1)),
            pl.BlockSpec(wa.shape, const),
            pl.BlockSpec(wb.shape, const),
            pl.BlockSpec(wo.shape, const),
            pl.BlockSpec((1, d), const),
        ],
        out_specs=[
            pl.BlockSpec((tm, d), lambda i: (i, 0)),
            pl.BlockSpec((tm, d), lambda i: (i, 0)),
        ],
        out_shape=[
            jax.ShapeDtypeStruct((s, d), F32),
            jax.ShapeDtypeStruct((s, d), BF16),
        ],
        compiler_params=pltpu.CompilerParams(
            dimension_semantics=("arbitrary",), vmem_limit_bytes=VMEM_LIMIT),
        name="merge_out",
    )(x, oa, obt, gates, gates, wa, wb, wo, g2)


def _mlp_kernel(h2_ref, x1_ref, w1_ref, w2_ref, gf_ref, o_ref, *, nf, final_norm):
    f = pl.program_id(1)

    @pl.when(f == 0)
    def _():
        o_ref[...] = x1_ref[...]

    u = jnp.maximum(_dot(h2_ref[...], w1_ref[...]), 0.0)
    o_ref[...] += _dot((u * u).astype(BF16), w2_ref[...])

    if final_norm:
        @pl.when(f == nf - 1)
        def _():
            o_ref[...] = _rms(o_ref[...], gf_ref[...])


def _mlp(h2, x1, w1, w2, gf, final_norm, tm=512, tf=512):
    s, d = x1.shape
    nf = w1.shape[1] // tf
    return pl.pallas_call(
        functools.partial(_mlp_kernel, nf=nf, final_norm=final_norm),
        grid=(s // tm, nf),
        in_specs=[
            pl.BlockSpec((tm, d), lambda i, f: (i, 0)),
            pl.BlockSpec((tm, d), lambda i, f: (i, 0)),
            pl.BlockSpec((d, tf), lambda i, f: (0, f)),
            pl.BlockSpec((tf, d), lambda i, f: (f, 0)),
            pl.BlockSpec((1, d), lambda i, f: (0, 0)),
        ],
        out_specs=pl.BlockSpec((tm, d), lambda i, f: (i, 0)),
        out_shape=jax.ShapeDtypeStruct((s, d), F32),
        compiler_params=pltpu.CompilerParams(
            dimension_semantics=("arbitrary", "arbitrary"), vmem_limit_bytes=VMEM_LIMIT),
        name="mlp",
    )(h2, x1, w1, w2, gf)


def _arrange_w_in(w_in):
    d = w_in.shape[0]
    o = 0
    qa = w_in[:, o:o + 1024] * (A_HEAD_DIM ** -0.5); o += 1024
    ka = w_in[:, o:o + 256]; o += 256
    va = w_in[:, o:o + 256]; o += 256
    cq = w_in[:, o:o + 512]; o += 512
    ckv = w_in[:, o:o + 256]; o += 256
    kr = w_in[:, o:o + B_ROPE]; o += B_ROPE
    ga = w_in[:, o:o + 2048]; o += 2048
    gb = w_in[:, o:o + 2048]
    kr_rot = jnp.concatenate([-kr[:, B_HALF:], kr[:, :B_HALF]], axis=1)
    pad = jnp.zeros((d, N_LAT - 512 - 256 - 2 * B_ROPE), w_in.dtype)
    return jnp.concatenate([qa, ka, va, cq, ckv, kr, kr_rot, pad, ga, gb], axis=1).astype(BF16)


def _arrange_mla_weights(w_uq, w_uk, w_uv):
    c_q = w_uq.shape[0]
    c_kv = w_uk.shape[0]
    qscale = (B_NOPE + B_ROPE) ** -0.5 * math.log2(math.e)
    wq = w_uq * qscale
    wq_pad = jnp.concatenate(
        [wq, jnp.zeros((c_q, B_HEADS, HEAD_PAD - B_NOPE - B_ROPE), wq.dtype)], axis=-1)
    wq_t = wq_pad.reshape(c_q, B_HEADS * HEAD_PAD).T
    rope = wq[:, :, B_NOPE:]
    rot = jnp.concatenate([-rope[..., B_HALF:], rope[..., :B_HALF]], axis=-1)
    wqr_t = rot.reshape(c_q, B_HEADS * B_ROPE).T
    wk_ext = jnp.concatenate(
        [w_uk, jnp.zeros((c_kv, B_HEADS, HEAD_PAD - B_NOPE), w_uk.dtype)], axis=-1
    ).reshape(c_kv, B_HEADS * HEAD_PAD)
    sel = np.zeros((HEAD_PAD, B_HEADS, HEAD_PAD), np.float32)
    for r in range(B_ROPE):
        sel[r, :, B_NOPE + r] = 1.0
    wk_rope = jnp.asarray(sel.reshape(HEAD_PAD, B_HEADS * HEAD_PAD))
    wv_t = w_uv.reshape(c_kv, B_HEADS * B_V).T
    return (wq_t.astype(BF16), wqr_t.astype(BF16), wk_ext.astype(BF16),
            wk_rope.astype(BF16), wv_t.astype(BF16))


def _rope_tables():
    inv = ROPE_THETA ** (-jnp.arange(B_HALF, dtype=F32) / B_HALF)
    inv_col = jnp.concatenate([inv, inv])[:, None]
    inv_row = jnp.concatenate([inv, inv, inv, inv, jnp.zeros((HEAD_PAD - 4 * B_HALF,), F32)])[None, :]
    return inv_col, inv_row


MLA_TQ = 512
MLA_TK = 512


def kernel(x, positions, attn_norm_g, w_in, a_sink, b_q_norm_g, b_kv_norm_g, b_w_uq, b_w_uk, b_w_uv,
           w_branch_a, w_branch_b, w_out, mlp_norm_g, w_mlp_in, w_mlp_out, final_norm_g):
    b, s, d = x.shape
    depth = w_in.shape[0]
    inv_col, inv_row = _rope_tables()
    outs = []
    for bi in range(b):
        xb = x[bi]
        pos_col = positions[bi][:, None]
        pos_row = positions[bi][None, :]
        for l in range(depth):
            w_in_l = _arrange_w_in(w_in[l])
            wq_t, wqr_t, wk_ext, wk_rope, wv_t = _arrange_mla_weights(b_w_uq[l], b_w_uk[l], b_w_uv[l])
            att, lat, gates = _in_proj(xb, attn_norm_g[l][None, :], w_in_l)
            qt, kfull, vt = _mla_prep(lat, pos_col, pos_row, b_q_norm_g[l][None, :], b_kv_norm_g[l][None, :],
                                      wq_t, wqr_t, wk_ext, wk_rope, wv_t, inv_col, inv_row, MLA_TK)
            obt = _mla_attn(qt, kfull, vt, MLA_TQ, MLA_TK)
            oa = _win_attn(att, pos_col, pos_row, a_sink[l])
            x1, h2 = _merge_out(xb, oa, obt, gates, w_branch_a[l].astype(BF16), w_branch_b[l].astype(BF16),
                                w_out[l].astype(BF16), mlp_norm_g[l][None, :])
            xb = _mlp(h2, x1, w_mlp_in[l].astype(BF16), w_mlp_out[l].astype(BF16),
                      final_norm_g[None, :], final_norm=(l == depth - 1))
        outs.append(xb)
    return outs[0][None] if b == 1 else jnp.stack(outs, axis=0)
```

```python
import functools
import math

import jax
import jax.numpy as jnp
import numpy as np
from jax import lax
from jax.experimental import pallas as pl
from jax.experimental.pallas import tpu as pltpu

F32 = jnp.float32
BF16 = jnp.bfloat16

EPS = 1e-6
NEG_INF = -1e30
ROPE_THETA = 10000.0

A_HEADS = 16
A_KV_HEADS = 4
A_GROUP = A_HEADS // A_KV_HEADS
A_HEAD_DIM = 64
WINDOW = 128
BAND = 128

B_HEADS = 16
B_NOPE = 64
B_ROPE = 32
B_V = 64
B_HALF = B_ROPE // 2
HEAD_PAD = 128

VMEM_LIMIT = 52 * 1024 * 1024

N_ATT = 1536
N_LAT = 1024
N_GATE = 4096
IN_TN = 512
J_ATT = N_ATT // IN_TN
J_LAT = N_LAT // IN_TN
J_GATE = N_GATE // IN_TN


def _rms(x, g):
    ms = jnp.mean(x * x, axis=-1, keepdims=True)
    return x * lax.rsqrt(ms + EPS) * g


def _dot(a, b):
    return jnp.dot(a, b, preferred_element_type=F32)


def _dot_nt(a, b):
    return lax.dot_general(a, b, (((1,), (1,)), ((), ())), preferred_element_type=F32)


def _dot_tn(a, b):
    return lax.dot_general(a, b, (((0,), (0,)), ((), ())), preferred_element_type=F32)


def _in_proj_kernel(x_ref, g_ref, w_ref, att_ref, lat_ref, gate_ref, h_ref):
    j = pl.program_id(1)

    @pl.when(j == 0)
    def _():
        h_ref[...] = _rms(x_ref[...], g_ref[...]).astype(BF16)

    y = _dot(h_ref[...], w_ref[...])

    @pl.when(j < J_ATT)
    def _():
        att_ref[...] = y.astype(BF16)

    @pl.when((j >= J_ATT) & (j < J_ATT + J_LAT))
    def _():
        lat_ref[...] = y

    @pl.when(j >= J_ATT + J_LAT)
    def _():
        gate_ref[...] = jax.nn.sigmoid(y).astype(BF16)


def _in_proj(x, g, w, tm=512):
    s, d = x.shape
    n = w.shape[1]
    nj = n // IN_TN
    return pl.pallas_call(
        _in_proj_kernel,
        grid=(s // tm, nj),
        in_specs=[
            pl.BlockSpec((tm, d), lambda i, j: (i, 0)),
            pl.BlockSpec((1, d), lambda i, j: (0, 0)),
            pl.BlockSpec((d, IN_TN), lambda i, j: (0, j)),
        ],
        out_specs=[
            pl.BlockSpec((tm, IN_TN), lambda i, j: (i, jnp.minimum(j, J_ATT - 1))),
            pl.BlockSpec((tm, IN_TN), lambda i, j: (i, jnp.clip(j - J_ATT, 0, J_LAT - 1))),
            pl.BlockSpec((tm, IN_TN), lambda i, j: (i, jnp.clip(j - J_ATT - J_LAT, 0, J_GATE - 1))),
        ],
        out_shape=[
            jax.ShapeDtypeStruct((s, N_ATT), BF16),
            jax.ShapeDtypeStruct((s, N_LAT), F32),
            jax.ShapeDtypeStruct((s, N_GATE), BF16),
        ],
        scratch_shapes=[pltpu.VMEM((tm, d), BF16)],
        compiler_params=pltpu.CompilerParams(
            dimension_semantics=("arbitrary", "arbitrary"), vmem_limit_bytes=VMEM_LIMIT),
        name="in_proj",
    )(x, g, w)


def _mla_prep_kernel(cq_ref, ckv_ref, kr_ref, posc_ref, posr_ref, gq_ref, gkv_ref,
                     wq_ref, wqr_ref, wk_ref, wkr_ref, wv_ref, invc_ref, invr_ref,
                     qt_ref, k_ref, vt_ref):
    cqn = _rms(cq_ref[...], gq_ref[...]).astype(BF16)
    ckvn = _rms(ckv_ref[...], gkv_ref[...]).astype(BF16)
    tm = cqn.shape[0]

    qt = _dot_nt(wq_ref[...], cqn)
    qrt = _dot_nt(wqr_ref[...], cqn)
    ang_t = invc_ref[...] * posr_ref[...].astype(F32)
    cos_t = jnp.cos(ang_t)
    sin_t = jnp.sin(ang_t)
    zeros = jnp.zeros((HEAD_PAD - B_NOPE - B_ROPE, tm), BF16)
    for h in range(B_HEADS):
        r0 = h * HEAD_PAD
        qt_ref[r0:r0 + B_NOPE, :] = qt[r0:r0 + B_NOPE].astype(BF16)
        rope = (qt[r0 + B_NOPE:r0 + B_NOPE + B_ROPE] * cos_t
                + qrt[h * B_ROPE:(h + 1) * B_ROPE] * sin_t)
        qt_ref[r0 + B_NOPE:r0 + B_NOPE + B_ROPE, :] = rope.astype(BF16)
        qt_ref[r0 + B_NOPE + B_ROPE:r0 + HEAD_PAD, :] = zeros

    ang = posc_ref[...].astype(F32) * invr_ref[...]
    lane = lax.broadcasted_iota(jnp.int32, ang.shape, 1)
    cs = jnp.where(lane < B_ROPE, jnp.cos(ang), jnp.where(lane < 2 * B_ROPE, jnp.sin(ang), 0.0))
    u = kr_ref[...] * cs
    k_rope = u + pltpu.roll(u, HEAD_PAD - B_ROPE, 1)
    kfull = _dot(ckvn, wk_ref[...]) + _dot(k_rope.astype(BF16), wkr_ref[...])
    k_ref[...] = kfull.astype(BF16)

    vt_ref[0] = _dot_nt(wv_ref[...], ckvn).astype(BF16)


def _mla_prep(lat, pos_col, pos_row, gq, gkv, wq_t, wqr_t, wk_ext, wk_rope, wv_t, inv_col, inv_row, tm):
    s = lat.shape[0]
    nq = B_HEADS * HEAD_PAD
    nv = B_HEADS * B_V
    const = lambda i: (0, 0)
    return pl.pallas_call(
        _mla_prep_kernel,
        grid=(s // tm,),
        in_specs=[
            pl.BlockSpec((tm, 512), lambda i: (i, 0)),
            pl.BlockSpec((tm, 256), lambda i: (i, 2)),
            pl.BlockSpec((tm, 128), lambda i: (i, 6)),
            pl.BlockSpec((tm, 1), lambda i: (i, 0)),
            pl.BlockSpec((1, tm), lambda i: (0, i)),
            pl.BlockSpec((1, 512), const),
            pl.BlockSpec((1, 256), const),
            pl.BlockSpec(wq_t.shape, const),
            pl.BlockSpec(wqr_t.shape, const),
            pl.BlockSpec(wk_ext.shape, const),
            pl.BlockSpec(wk_rope.shape, const),
            pl.BlockSpec(wv_t.shape, const),
            pl.BlockSpec(inv_col.shape, const),
            pl.BlockSpec(inv_row.shape, const),
        ],
        out_specs=[
            pl.BlockSpec((nq, tm), lambda i: (0, i)),
            pl.BlockSpec((tm, nq), lambda i: (i, 0)),
            pl.BlockSpec((1, nv, tm), lambda i: (i, 0, 0)),
        ],
        out_shape=[
            jax.ShapeDtypeStruct((nq, s), BF16),
            jax.ShapeDtypeStruct((s, nq), BF16),
            jax.ShapeDtypeStruct((s // tm, nv, tm), BF16),
        ],
        compiler_params=pltpu.CompilerParams(
            dimension_semantics=("arbitrary",), vmem_limit_bytes=VMEM_LIMIT),
        name="mla_prep",
    )(lat, lat, lat, pos_col, pos_row, gq, gkv, wq_t, wqr_t, wk_ext, wk_rope, wv_t, inv_col, inv_row)


def _mla_attn_kernel(q_ref, k_ref, v_ref, o_ref, s_ref, *, tk, nkb):
    q = q_ref[...]
    tq = q.shape[1]

    def scores(kb, slot):
        k = k_ref[pl.ds(pl.multiple_of(kb * tk, tk), tk), :]
        s_ref[slot] = _dot(k, q)

    def update(kb, slot, carry):
        m, l, acc = carry
        m_new = jnp.maximum(m, jnp.max(s_ref[slot], axis=0, keepdims=True))
        alpha = jnp.exp2(m - m_new)
        p = jnp.exp2(s_ref[slot] - m_new)
        l = l * alpha + jnp.sum(p, axis=0, keepdims=True)
        acc = acc * alpha + _dot(v_ref[kb], p.astype(BF16))
        return m_new, l, acc

    def pair(i, carry):
        kb = 2 * i
        scores(kb + 1, 1)
        carry = update(kb, 0, carry)
        scores(kb + 2, 0)
        return update(kb + 1, 1, carry)

    m0 = jnp.full((1, tq), NEG_INF, F32)
    l0 = jnp.zeros((1, tq), F32)
    a0 = jnp.zeros((B_V, tq), F32)
    scores(0, 0)
    carry = lax.fori_loop(0, nkb // 2 - 1, pair, (m0, l0, a0))
    scores(nkb - 1, 1)
    carry = update(nkb - 2, 0, carry)
    _, l, acc = update(nkb - 1, 1, carry)
    o_ref[...] = (acc / l).astype(BF16)


def _mla_attn(qt, kfull, vt, tq, tk):
    s = kfull.shape[0]
    nkb = s // tk
    return pl.pallas_call(
        functools.partial(_mla_attn_kernel, tk=tk, nkb=nkb),
        grid=(B_HEADS, s // tq),
        in_specs=[
            pl.BlockSpec((HEAD_PAD, tq), lambda h, i: (h, i)),
            pl.BlockSpec((s, HEAD_PAD), lambda h, i: (0, h)),
            pl.BlockSpec((nkb, B_V, tk), lambda h, i: (0, h, 0)),
        ],
        out_specs=pl.BlockSpec((B_V, tq), lambda h, i: (h, i)),
        out_shape=jax.ShapeDtypeStruct((B_HEADS * B_V, s), BF16),
        scratch_shapes=[pltpu.VMEM((2, tk, tq), F32)],
        compiler_params=pltpu.CompilerParams(
            dimension_semantics=("arbitrary", "arbitrary"), vmem_limit_bytes=VMEM_LIMIT),
        name="mla_attn",
    )(qt, kfull, vt)


def _win_attn_kernel(sink_ref, q_ref, kvp_ref, kvc_ref, kvn_ref, pq_ref, pkp_ref, pkc_ref, pkn_ref,
                     o_ref, *, seq, slopes):
    i = pl.program_id(0)
    q = q_ref[...]
    kv = jnp.concatenate([kvp_ref[...], kvc_ref[...], kvn_ref[...]], axis=0)
    pk = jnp.concatenate([pkp_ref[...], pkc_ref[...], pkn_ref[...]], axis=1)
    dist = jnp.abs(pq_ref[...] - pk).astype(F32)
    r = lax.broadcasted_iota(jnp.int32, dist.shape, 0)
    c = lax.broadcasted_iota(jnp.int32, dist.shape, 1)
    kglob = (i - 1) * BAND + c
    mask = (jnp.abs(BAND + r - c) <= WINDOW) & (kglob >= 0) & (kglob < seq)
    kw = A_KV_HEADS * A_HEAD_DIM
    outs = []
    for h in range(A_HEADS):
        g = h // A_GROUP
        qh = q[:, h * A_HEAD_DIM:(h + 1) * A_HEAD_DIM]
        kh = kv[:, g * A_HEAD_DIM:(g + 1) * A_HEAD_DIM]
        vh = kv[:, kw + g * A_HEAD_DIM:kw + (g + 1) * A_HEAD_DIM]
        sc = _dot_nt(qh, kh) - slopes[h] * dist
        sc = jnp.where(mask, sc, NEG_INF)
        sk = sink_ref[h]
        m = jnp.maximum(jnp.max(sc, axis=-1, keepdims=True), sk)
        p = jnp.exp(sc - m)
        denom = jnp.sum(p, axis=-1, keepdims=True) + jnp.exp(sk - m)
        outs.append(_dot((p / denom).astype(BF16), vh))
    o_ref[...] = jnp.concatenate(outs, axis=-1).astype(BF16)


def _win_attn(att, pos_col, pos_row, sink):
    s = att.shape[0]
    nb = s // BAND
    slopes = tuple(float(2.0 ** (-8.0 * (h + 1) / A_HEADS)) for h in range(A_HEADS))
    kvcol = (A_HEADS * A_HEAD_DIM) // (2 * A_KV_HEADS * A_HEAD_DIM)
    prev = lambda i, sk: jnp.maximum(i - 1, 0)
    nxt = lambda i, sk: jnp.minimum(i + 1, nb - 1)
    grid_spec = pltpu.PrefetchScalarGridSpec(
        num_scalar_prefetch=1,
        grid=(nb,),
        in_specs=[
            pl.BlockSpec((BAND, A_HEADS * A_HEAD_DIM), lambda i, sk: (i, 0)),
            pl.BlockSpec((BAND, 512), lambda i, sk: (prev(i, sk), kvcol)),
            pl.BlockSpec((BAND, 512), lambda i, sk: (i, kvcol)),
            pl.BlockSpec((BAND, 512), lambda i, sk: (nxt(i, sk), kvcol)),
            pl.BlockSpec((BAND, 1), lambda i, sk: (i, 0)),
            pl.BlockSpec((1, BAND), lambda i, sk: (0, prev(i, sk))),
            pl.BlockSpec((1, BAND), lambda i, sk: (0, i)),
            pl.BlockSpec((1, BAND), lambda i, sk: (0, nxt(i, sk))),
        ],
        out_specs=pl.BlockSpec((BAND, A_HEADS * A_HEAD_DIM), lambda i, sk: (i, 0)),
    )
    return pl.pallas_call(
        functools.partial(_win_attn_kernel, seq=s, slopes=slopes),
        grid_spec=grid_spec,
        out_shape=jax.ShapeDtypeStruct((s, A_HEADS * A_HEAD_DIM), BF16),
        compiler_params=pltpu.CompilerParams(
            dimension_semantics=("arbitrary",), vmem_limit_bytes=VMEM_LIMIT),
        name="win_attn",
    )(sink, att, att, att, att, pos_col, pos_row, pos_row, pos_row)


def _merge_out_kernel(x_ref, oa_ref, obt_ref, ga_ref, gb_ref, wa_ref, wb_ref, wo_ref, g2_ref,
                      x1_ref, h2_ref):
    ya = _dot(oa_ref[...], wa_ref[...])
    yb = _dot_tn(obt_ref[...], wb_ref[...])
    merged = ga_ref[...].astype(F32) * ya + gb_ref[...].astype(F32) * yb
    x1 = x_ref[...] + _dot(merged.astype(BF16), wo_ref[...])
    x1_ref[...] = x1
    h2_ref[...] = _rms(x1, g2_ref[...]).astype(BF16)


def _merge_out(x, oa, obt, gates, wa, wb, wo, g2, tm=256):
    s, d = x.shape
    const = lambda i: (0, 0)
    return pl.pallas_call(
        _merge_out_kernel,
        grid=(s // tm,),
        in_specs=[
            pl.BlockSpec((tm, d), lambda i: (i, 0)),
            pl.BlockSpec((tm, oa.shape[1]), lambda i: (i, 0)),
            pl.BlockSpec((obt.shape[0], tm), lambda i: (0, i)),
            pl.BlockSpec((tm, d), lambda i: (i, 0)),
            pl.BlockSpec((tm, d), lambda i: (i, 1)),
            pl.BlockSpec(wa.shape, const),
            pl.BlockSpec(wb.shape, const),
            pl.BlockSpec(wo.shape, const),
            pl.BlockSpec((1, d), const),
        ],
        out_specs=[
            pl.BlockSpec((tm, d), lambda i: (i, 0)),
            pl.BlockSpec((tm, d), lambda i: (i, 0)),
        ],
        out_shape=[
            jax.ShapeDtypeStruct((s, d), F32),
            jax.ShapeDtypeStruct((s, d), BF16),
        ],
        compiler_params=pltpu.CompilerParams(
            dimension_semantics=("arbitrary",), vmem_limit_bytes=VMEM_LIMIT),
        name="merge_out",
    )(x, oa, obt, gates, gates, wa, wb, wo, g2)


def _mlp_kernel(h2_ref, x1_ref, w1_ref, w2_ref, gf_ref, o_ref, *, nf, final_norm):
    f = pl.program_id(1)

    @pl.when(f == 0)
    def _():
        o_ref[...] = x1_ref[...]

    u = jnp.maximum(_dot(h2_ref[...], w1_ref[...]), 0.0)
    o_ref[...] += _dot((u * u).astype(BF16), w2_ref[...])

    if final_norm:
        @pl.when(f == nf - 1)
        def _():
            o_ref[...] = _rms(o_ref[...], gf_ref[...])


def _mlp(h2, x1, w1, w2, gf, final_norm, tm=512, tf=512):
    s, d = x1.shape
    nf = w1.shape[1] // tf
    return pl.pallas_call(
        functools.partial(_mlp_kernel, nf=nf, final_norm=final_norm),
        grid=(s // tm, nf),
        in_specs=[
            pl.BlockSpec((tm, d), lambda i, f: (i, 0)),
            pl.BlockSpec((tm, d), lambda i, f: (i, 0)),
            pl.BlockSpec((d, tf), lambda i, f: (0, f)),
            pl.BlockSpec((tf, d), lambda i, f: (f, 0)),
            pl.BlockSpec((1, d), lambda i, f: (0, 0)),
        ],
        out_specs=pl.BlockSpec((tm, d), lambda i, f: (i, 0)),
        out_shape=jax.ShapeDtypeStruct((s, d), F32),
        compiler_params=pltpu.CompilerParams(
            dimension_semantics=("arbitrary", "arbitrary"), vmem_limit_bytes=VMEM_LIMIT),
        name="mlp",
    )(h2, x1, w1, w2, gf)


def _arrange_w_in(w_in):
    d = w_in.shape[0]
    o = 0
    qa = w_in[:, o:o + 1024] * (A_HEAD_DIM ** -0.5); o += 1024
    ka = w_in[:, o:o + 256]; o += 256
    va = w_in[:, o:o + 256]; o += 256
    cq = w_in[:, o:o + 512]; o += 512
    ckv = w_in[:, o:o + 256]; o += 256
    kr = w_in[:, o:o + B_ROPE]; o += B_ROPE
    ga = w_in[:, o:o + 2048]; o += 2048
    gb = w_in[:, o:o + 2048]
    kr_rot = jnp.concatenate([-kr[:, B_HALF:], kr[:, :B_HALF]], axis=1)
    pad = jnp.zeros((d, N_LAT - 512 - 256 - 2 * B_ROPE), w_in.dtype)
    return jnp.concatenate([qa, ka, va, cq, ckv, kr, kr_rot, pad, ga, gb], axis=1).astype(BF16)


def _arrange_mla_weights(w_uq, w_uk, w_uv):
    c_q = w_uq.shape[0]
    c_kv = w_uk.shape[0]
    qscale = (B_NOPE + B_ROPE) ** -0.5 * math.log2(math.e)
    wq = w_uq * qscale
    wq_pad = jnp.concatenate(
        [wq, jnp.zeros((c_q, B_HEADS, HEAD_PAD - B_NOPE - B_ROPE), wq.dtype)], axis=-1)
    wq_t = wq_pad.reshape(c_q, B_HEADS * HEAD_PAD).T
    rope = wq[:, :, B_NOPE:]
    rot = jnp.concatenate([-rope[..., B_HALF:], rope[..., :B_HALF]], axis=-1)
    wqr_t = rot.reshape(c_q, B_HEADS * B_ROPE).T
    wk_ext = jnp.concatenate(
        [w_uk, jnp.zeros((c_kv, B_HEADS, HEAD_PAD - B_NOPE), w_uk.dtype)], axis=-1
    ).reshape(c_kv, B_HEADS * HEAD_PAD)
    sel = np.zeros((HEAD_PAD, B_HEADS, HEAD_PAD), np.float32)
    for r in range(B_ROPE):
        sel[r, :, B_NOPE + r] = 1.0
    wk_rope = jnp.asarray(sel.reshape(HEAD_PAD, B_HEADS * HEAD_PAD))
    wv_t = w_uv.reshape(c_kv, B_HEADS * B_V).T
    return (wq_t.astype(BF16), wqr_t.astype(BF16), wk_ext.astype(BF16),
            wk_rope.astype(BF16), wv_t.astype(BF16))


def _rope_tables():
    inv = ROPE_THETA ** (-jnp.arange(B_HALF, dtype=F32) / B_HALF)
    inv_col = jnp.concatenate([inv, inv])[:, None]
    inv_row = jnp.concatenate([inv, inv, inv, inv, jnp.zeros((HEAD_PAD - 4 * B_HALF,), F32)])[None, :]
    return inv_col, inv_row


MLA_TQ = 512
MLA_TK = 512


def kernel(x, positions, attn_norm_g, w_in, a_sink, b_q_norm_g, b_kv_norm_g, b_w_uq, b_w_uk, b_w_uv,
           w_branch_a, w_branch_b, w_out, mlp_norm_g, w_mlp_in, w_mlp_out, final_norm_g):
    b, s, d = x.shape
    depth = w_in.shape[0]
    inv_col, inv_row = _rope_tables()
    outs = []
    for bi in range(b):
        xb = x[bi]
        pos_col = positions[bi][:, None]
        pos_row = positions[bi][None, :]
        for l in range(depth):
            w_in_l = _arrange_w_in(w_in[l])
            wq_t, wqr_t, wk_ext, wk_rope, wv_t = _arrange_mla_weights(b_w_uq[l], b_w_uk[l], b_w_uv[l])
            att, lat, gates = _in_proj(xb, attn_norm_g[l][None, :], w_in_l)
            qt, kfull, vt = _mla_prep(lat, pos_col, pos_row, b_q_norm_g[l][None, :], b_kv_norm_g[l][None, :],
                                      wq_t, wqr_t, wk_ext, wk_rope, wv_t, inv_col, inv_row, MLA_TK)
            obt = _mla_attn(qt, kfull, vt, MLA_TQ, MLA_TK)
            oa = _win_attn(att, pos_col, pos_row, a_sink[l])
            x1, h2 = _merge_out(xb, oa, obt, gates, w_branch_a[l].astype(BF16), w_branch_b[l].astype(BF16),
                                w_out[l].astype(BF16), mlp_norm_g[l][None, :])
            xb = _mlp(h2, x1, w_mlp_in[l].astype(BF16), w_mlp_out[l].astype(BF16),
                      final_norm_g[None, :], final_norm=(l == depth - 1))
        outs.append(xb)
    return outs[0][None] if b == 1 else jnp.stack(outs, axis=0)
```

```python
import functools
import math

import jax
import jax.numpy as jnp
import numpy as np
from jax import lax
from jax.experimental import pallas as pl
from jax.experimental.pallas import tpu as pltpu

F32 = jnp.float32
BF16 = jnp.bfloat16

EPS = 1e-6
NEG_INF = -1e30
ROPE_THETA = 10000.0

A_HEADS = 16
A_KV_HEADS = 4
A_GROUP = A_HEADS // A_KV_HEADS
A_HEAD_DIM = 64
WINDOW = 128
BAND = 128

B_HEADS = 16
B_NOPE = 64
B_ROPE = 32
B_V = 64
B_HALF = B_ROPE // 2
HEAD_PAD = 128

VMEM_LIMIT = 52 * 1024 * 1024

N_ATT = 1536
N_LAT = 1024
N_GATE = 4096
IN_TN = 512
J_ATT = N_ATT // IN_TN
J_LAT = N_LAT // IN_TN
J_GATE = N_GATE // IN_TN
NORM_ROWS = 256


def _rms(x, g):
    ms = jnp.mean(x * x, axis=-1, keepdims=True)
    return x * lax.rsqrt(ms + EPS) * g


def _dot(a, b):
    return jnp.dot(a, b, preferred_element_type=F32)


def _dot_nt(a, b):
    return lax.dot_general(a, b, (((1,), (1,)), ((), ())), preferred_element_type=F32)


def _dot_tn(a, b):
    return lax.dot_general(a, b, (((0,), (0,)), ((), ())), preferred_element_type=F32)


def _in_proj_kernel(x_ref, g_ref, w_ref, att_ref, lat_ref, gate_ref, h_ref):
    j = pl.program_id(1)

    @pl.when(j == 0)
    def _():
        def slab(c, carry):
            rows = pl.ds(pl.multiple_of(c * NORM_ROWS, NORM_ROWS), NORM_ROWS)
            h_ref[rows, :] = _rms(x_ref[rows, :], g_ref[...]).astype(BF16)
            return carry
        lax.fori_loop(0, x_ref.shape[0] // NORM_ROWS, slab, 0)

    y = _dot(h_ref[...], w_ref[...])

    @pl.when(j < J_ATT)
    def _():
        att_ref[...] = y.astype(BF16)

    @pl.when((j >= J_ATT) & (j < J_ATT + J_LAT))
    def _():
        lat_ref[...] = y

    @pl.when(j >= J_ATT + J_LAT)
    def _():
        gate_ref[...] = jax.nn.sigmoid(y).astype(BF16)


def _in_proj(x, g, w, tm=1024):
    s, d = x.shape
    n = w.shape[1]
    nj = n // IN_TN
    return pl.pallas_call(
        _in_proj_kernel,
        grid=(s // tm, nj),
        in_specs=[
            pl.BlockSpec((tm, d), lambda i, j: (i, 0)),
            pl.BlockSpec((1, d), lambda i, j: (0, 0)),
            pl.BlockSpec((d, IN_TN), lambda i, j: (0, j)),
        ],
        out_specs=[
            pl.BlockSpec((tm, IN_TN), lambda i, j: (i, jnp.minimum(j, J_ATT - 1))),
            pl.BlockSpec((tm, IN_TN), lambda i, j: (i, jnp.clip(j - J_ATT, 0, J_LAT - 1))),
            pl.BlockSpec((tm, IN_TN), lambda i, j: (i, jnp.clip(j - J_ATT - J_LAT, 0, J_GATE - 1))),
        ],
        out_shape=[
            jax.ShapeDtypeStruct((s, N_ATT), BF16),
            jax.ShapeDtypeStruct((s, N_LAT), F32),
            jax.ShapeDtypeStruct((s, N_GATE), BF16),
        ],
        scratch_shapes=[pltpu.VMEM((tm, d), BF16)],
        compiler_params=pltpu.CompilerParams(
            dimension_semantics=("arbitrary", "arbitrary"), vmem_limit_bytes=VMEM_LIMIT),
        name="in_proj",
    )(x, g, w)


def _mla_prep_kernel(cq_ref, ckv_ref, kr_ref, posc_ref, posr_ref, gq_ref, gkv_ref,
                     wq_ref, wqr_ref, wk_ref, wkr_ref, wv_ref, invc_ref, invr_ref,
                     qt_ref, k_ref, vt_ref):
    cqn = _rms(cq_ref[...], gq_ref[...]).astype(BF16)
    ckvn = _rms(ckv_ref[...], gkv_ref[...]).astype(BF16)
    tm = cqn.shape[0]

    qt = _dot_nt(wq_ref[...], cqn)
    qrt = _dot_nt(wqr_ref[...], cqn)
    ang_t = invc_ref[...] * posr_ref[...].astype(F32)
    cos_t = jnp.cos(ang_t)
    sin_t = jnp.sin(ang_t)
    zeros = jnp.zeros((HEAD_PAD - B_NOPE - B_ROPE, tm), BF16)
    for h in range(B_HEADS):
        r0 = h * HEAD_PAD
        qt_ref[r0:r0 + B_NOPE, :] = qt[r0:r0 + B_NOPE].astype(BF16)
        rope = (qt[r0 + B_NOPE:r0 + B_NOPE + B_ROPE] * cos_t
                + qrt[h * B_ROPE:(h + 1) * B_ROPE] * sin_t)
        qt_ref[r0 + B_NOPE:r0 + B_NOPE + B_ROPE, :] = rope.astype(BF16)
        qt_ref[r0 + B_NOPE + B_ROPE:r0 + HEAD_PAD, :] = zeros

    ang = posc_ref[...].astype(F32) * invr_ref[...]
    lane = lax.broadcasted_iota(jnp.int32, ang.shape, 1)
    cs = jnp.where(lane < B_ROPE, jnp.cos(ang), jnp.where(lane < 2 * B_ROPE, jnp.sin(ang), 0.0))
    u = kr_ref[...] * cs
    k_rope = u + pltpu.roll(u, HEAD_PAD - B_ROPE, 1)
    kfull = _dot(ckvn, wk_ref[...]) + _dot(k_rope.astype(BF16), wkr_ref[...])
    col = lax.broadcasted_iota(jnp.int32, kfull.shape, 1)
    kfull = jnp.where((col & (HEAD_PAD - 1)) == SHIFT_ROW, 1.0, kfull)
    k_ref[...] = kfull.astype(BF16)

    vt_ref[0] = _dot_nt(wv_ref[...], ckvn).astype(BF16)


def _mla_prep(lat, pos_col, pos_row, gq, gkv, wq_t, wqr_t, wk_ext, wk_rope, wv_t, inv_col, inv_row, tm):
    s = lat.shape[0]
    nq = B_HEADS * HEAD_PAD
    nv = B_HEADS * B_V
    const = lambda i: (0, 0)
    return pl.pallas_call(
        _mla_prep_kernel,
        grid=(s // tm,),
        in_specs=[
            pl.BlockSpec((tm, 512), lambda i: (i, 0)),
            pl.BlockSpec((tm, 256), lambda i: (i, 2)),
            pl.BlockSpec((tm, 128), lambda i: (i, 6)),
            pl.BlockSpec((tm, 1), lambda i: (i, 0)),
            pl.BlockSpec((1, tm), lambda i: (0, i)),
            pl.BlockSpec((1, 512), const),
            pl.BlockSpec((1, 256), const),
            pl.BlockSpec(wq_t.shape, const),
            pl.BlockSpec(wqr_t.shape, const),
            pl.BlockSpec(wk_ext.shape, const),
            pl.BlockSpec(wk_rope.shape, const),
            pl.BlockSpec(wv_t.shape, const),
            pl.BlockSpec(inv_col.shape, const),
            pl.BlockSpec(inv_row.shape, const),
        ],
        out_specs=[
            pl.BlockSpec((nq, tm), lambda i: (0, i)),
            pl.BlockSpec((tm, nq), lambda i: (i, 0)),
            pl.BlockSpec((1, nv, tm), lambda i: (i, 0, 0)),
        ],
        out_shape=[
            jax.ShapeDtypeStruct((nq, s), BF16),
            jax.ShapeDtypeStruct((s, nq), BF16),
            jax.ShapeDtypeStruct((s // tm, nv, tm), BF16),
        ],
        compiler_params=pltpu.CompilerParams(
            dimension_semantics=("arbitrary",), vmem_limit_bytes=VMEM_LIMIT),
        name="mla_prep",
    )(lat, lat, lat, pos_col, pos_row, gq, gkv, wq_t, wqr_t, wk_ext, wk_rope, wv_t, inv_col, inv_row)


SHIFT_KEYS = 256
SHIFT_ROW = B_NOPE + B_ROPE
L_MIN = 0.5
L_MAX = 2.0 ** 100


def _sweep(nkb, scores, update, carry, group):
    def blocks(kb0, carry, last):
        for j in range(group):
            if not (last and j == group - 1):
                scores(kb0 + j + 1, (j + 1) % 2)
            carry = update(kb0 + j, j % 2, carry)
        return carry

    scores(0, 0)
    carry = lax.fori_loop(0, nkb // group - 1, lambda i, c: blocks(group * i, c, False), carry)
    return blocks(nkb - group, carry, True)


def _mla_attn_kernel(q_ref, k_ref, v_ref, o_ref, s_ref, qs_ref, *, tk, nkb, group):
    tq = q_ref.shape[1]

    def key_block(kb):
        return k_ref[pl.ds(pl.multiple_of(kb * tk, tk), tk), :]

    s0 = _dot(k_ref[0:SHIFT_KEYS, :], q_ref[...])
    shift = jnp.max(s0, axis=0, keepdims=True)
    row = lax.broadcasted_iota(jnp.int32, (16, tq), 0)
    qs_ref[...] = q_ref[...]
    qs_ref[SHIFT_ROW:SHIFT_ROW + 16, :] = jnp.where(row == 0, -shift, 0.0).astype(BF16)

    def scores_fixed(kb, slot):
        s_ref[slot] = _dot(key_block(kb), qs_ref[...])

    def update_fixed(kb, slot, carry):
        l8, acc = carry
        p = jnp.exp2(s_ref[slot])
        l8 = l8 + jnp.sum(p.reshape(tk // 8, 8, tq), axis=0)
        acc = acc + _dot(v_ref[kb], p.astype(BF16))
        return l8, acc

    l8, acc = _sweep(nkb, scores_fixed, update_fixed,
                     (jnp.zeros((8, tq), F32), jnp.zeros((B_V, tq), F32)), group)
    l = jnp.sum(l8, axis=0, keepdims=True)
    l_ok = (l >= L_MIN) & (l < L_MAX)
    acc_ok = jnp.abs(acc) < L_MAX
    n_bad = jnp.sum(jnp.where(l_ok, 0.0, 1.0)) + jnp.sum(jnp.where(acc_ok, 0.0, 1.0))

    @pl.when(n_bad == 0.0)
    def _():
        o_ref[...] = (acc / l).astype(BF16)

    @pl.when(n_bad != 0.0)
    def _():
        def scores_raw(kb, slot):
            s_ref[slot] = _dot(key_block(kb), q_ref[...])

        def update_online(kb, slot, carry):
            m, l, acc = carry
            m_new = jnp.maximum(m, jnp.max(s_ref[slot], axis=0, keepdims=True))
            alpha = jnp.exp2(m - m_new)
            p = jnp.exp2(s_ref[slot] - m_new)
            l = l * alpha + jnp.sum(p, axis=0, keepdims=True)
            acc = acc * alpha + _dot(v_ref[kb], p.astype(BF16))
            return m_new, l, acc

        _, l, acc = _sweep(nkb, scores_raw, update_online,
                           (jnp.full((1, tq), NEG_INF, F32), jnp.zeros((1, tq), F32),
                            jnp.zeros((B_V, tq), F32)), 2)
        o_ref[...] = (acc / l).astype(BF16)


def _mla_attn(qt, kfull, vt, tq, tk, group=8):
    s = kfull.shape[0]
    nkb = s // tk
    group = min(group, nkb)
    return pl.pallas_call(
        functools.partial(_mla_attn_kernel, tk=tk, nkb=nkb, group=group),
        grid=(B_HEADS, s // tq),
        in_specs=[
            pl.BlockSpec((HEAD_PAD, tq), lambda h, i: (h, i)),
            pl.BlockSpec((s, HEAD_PAD), lambda h, i: (0, h)),
            pl.BlockSpec((nkb, B_V, tk), lambda h, i: (0, h, 0)),
        ],
        out_specs=pl.BlockSpec((B_V, tq), lambda h, i: (h, i)),
        out_shape=jax.ShapeDtypeStruct((B_HEADS * B_V, s), BF16),
        scratch_shapes=[pltpu.VMEM((2, tk, tq), F32), pltpu.VMEM((HEAD_PAD, tq), BF16)],
        compiler_params=pltpu.CompilerParams(
            dimension_semantics=("arbitrary", "arbitrary"), vmem_limit_bytes=VMEM_LIMIT),
        name="mla_attn",
    )(qt, kfull, vt)


def _win_attn_kernel(sink_ref, q_ref, kvp_ref, kvc_ref, kvn_ref, pq_ref, pkp_ref, pkc_ref, pkn_ref,
                     o_ref, *, seq, slopes):
    i = pl.program_id(0)
    q = q_ref[...]
    kv = jnp.concatenate([kvp_ref[...], kvc_ref[...], kvn_ref[...]], axis=0)
    pk = jnp.concatenate([pkp_ref[...], pkc_ref[...], pkn_ref[...]], axis=1)
    dist = jnp.abs(pq_ref[...] - pk).astype(F32)
    r = lax.broadcasted_iota(jnp.int32, dist.shape, 0)
    c = lax.broadcasted_iota(jnp.int32, dist.shape, 1)
    kglob = (i - 1) * BAND + c
    mask = (jnp.abs(BAND + r - c) <= WINDOW) & (kglob >= 0) & (kglob < seq)
    kw = A_KV_HEADS * A_HEAD_DIM
    outs = []
    for h in range(A_HEADS):
        g = h // A_GROUP
        qh = q[:, h * A_HEAD_DIM:(h + 1) * A_HEAD_DIM]
        kh = kv[:, g * A_HEAD_DIM:(g + 1) * A_HEAD_DIM]
        vh = kv[:, kw + g * A_HEAD_DIM:kw + (g + 1) * A_HEAD_DIM]
        sc = _dot_nt(qh, kh) - slopes[h] * dist
        sc = jnp.where(mask, sc, NEG_INF)
        sk = sink_ref[h]
        m = jnp.maximum(jnp.max(sc, axis=-1, keepdims=True), sk)
        p = jnp.exp(sc - m)
        denom = jnp.sum(p, axis=-1, keepdims=True) + jnp.exp(sk - m)
        outs.append(_dot((p / denom).astype(BF16), vh))
    o_ref[...] = jnp.concatenate(outs, axis=-1).astype(BF16)


def _win_attn(att, pos_col, pos_row, sink):
    s = att.shape[0]
    nb = s // BAND
    slopes = tuple(float(2.0 ** (-8.0 * (h + 1) / A_HEADS)) for h in range(A_HEADS))
    kvcol = (A_HEADS * A_HEAD_DIM) // (2 * A_KV_HEADS * A_HEAD_DIM)
    prev = lambda i, sk: jnp.maximum(i - 1, 0)
    nxt = lambda i, sk: jnp.minimum(i + 1, nb - 1)
    grid_spec = pltpu.PrefetchScalarGridSpec(
        num_scalar_prefetch=1,
        grid=(nb,),
        in_specs=[
            pl.BlockSpec((BAND, A_HEADS * A_HEAD_DIM), lambda i, sk: (i, 0)),
            pl.BlockSpec((BAND, 512), lambda i, sk: (prev(i, sk), kvcol)),
            pl.BlockSpec((BAND, 512), lambda i, sk: (i, kvcol)),
            pl.BlockSpec((BAND, 512), lambda i, sk: (nxt(i, sk), kvcol)),
            pl.BlockSpec((BAND, 1), lambda i, sk: (i, 0)),
            pl.BlockSpec((1, BAND), lambda i, sk: (0, prev(i, sk))),
            pl.BlockSpec((1, BAND), lambda i, sk: (0, i)),
            pl.BlockSpec((1, BAND), lambda i, sk: (0, nxt(i, sk))),
        ],
        out_specs=pl.BlockSpec((BAND, A_HEADS * A_HEAD_DIM), lambda i, sk: (i, 0)),
    )
    return pl.pallas_call(
        functools.partial(_win_attn_kernel, seq=s, slopes=slopes),
        grid_spec=grid_spec,
        out_shape=jax.ShapeDtypeStruct((s, A_HEADS * A_HEAD_DIM), BF16),
        compiler_params=pltpu.CompilerParams(
            dimension_semantics=("arbitrary",), vmem_limit_bytes=VMEM_LIMIT),
        name="win_attn",
    )(sink, att, att, att, att, pos_col, pos_row, pos_row, pos_row)


def _merge_out_kernel(x_ref, oa_ref, obt_ref, ga_ref, gb_ref, wa_ref, wb_ref, wo_ref, g2_ref,
                      x1_ref, h2_ref):
    ya = _dot(oa_ref[...], wa_ref[...])
    yb = _dot_tn(obt_ref[...], wb_ref[...])
    merged = ga_ref[...].astype(F32) * ya + gb_ref[...].astype(F32) * yb
    x1 = x_ref[...] + _dot(merged.astype(BF16), wo_ref[...])
    x1_ref[...] = x1
    h2_ref[...] = _rms(x1, g2_ref[...]).astype(BF16)


def _merge_out(x, oa, obt, gates, wa, wb, wo, g2, tm=256):
    s, d = x.shape
    const = lambda i: (0, 0)
    return pl.pallas_call(
        _merge_out_kernel,
        grid=(s // tm,),
        in_specs=[
            pl.BlockSpec((tm, d), lambda i: (i, 0)),
            pl.BlockSpec((tm, oa.shape[1]), lambda i: (i, 0)),
            pl.BlockSpec((obt.shape[0], tm), lambda i: (0, i)),
            pl.BlockSpec((tm, d), lambda i: (i, 0)),
            pl.BlockSpec((tm, d), lambda i: (i, 1)),
            pl.BlockSpec(wa.shape, const),
            pl.BlockSpec(wb.shape, const),
            pl.BlockSpec(wo.shape, const),
            pl.BlockSpec((1, d), const),
        ],
        out_specs=[
            pl.BlockSpec((tm, d), lambda i: (i, 0)),
            pl.BlockSpec((tm, d), lambda i: (i, 0)),
        ],
        out_shape=[
            jax.ShapeDtypeStruct((s, d), F32),
            jax.ShapeDtypeStruct((s, d), BF16),
        ],
        compiler_params=pltpu.CompilerParams(
            dimension_semantics=("arbitrary",), vmem_limit_bytes=VMEM_LIMIT),
        name="merge_out",
    )(x, oa, obt, gates, gates, wa, wb, wo, g2)


def _mlp_kernel(h2_ref, x1_ref, w1_ref, w2_ref, gf_ref, o_ref, *, nf, final_norm):
    f = pl.program_id(1)

    @pl.when(f == 0)
    def _():
        o_ref[...] = x1_ref[...]

    u = jnp.maximum(_dot(h2_ref[...], w1_ref[...]), 0.0)
    o_ref[...] += _dot((u * u).astype(BF16), w2_ref[...])

    if final_norm:
        @pl.when(f == nf - 1)
        def _():
            o_ref[...] = _rms(o_ref[...], gf_ref[...])


def _mlp(h2, x1, w1, w2, gf, final_norm, tm=512, tf=1024):
    s, d = x1.shape
    nf = w1.shape[1] // tf
    return pl.pallas_call(
        functools.partial(_mlp_kernel, nf=nf, final_norm=final_norm),
        grid=(s // tm, nf),
        in_specs=[
            pl.BlockSpec((tm, d), lambda i, f: (i, 0)),
            pl.BlockSpec((tm, d), lambda i, f: (i, 0)),
            pl.BlockSpec((d, tf), lambda i, f: (0, f)),
            pl.BlockSpec((tf, d), lambda i, f: (f, 0)),
            pl.BlockSpec((1, d), lambda i, f: (0, 0)),
        ],
        out_specs=pl.BlockSpec((tm, d), lambda i, f: (i, 0)),
        out_shape=jax.ShapeDtypeStruct((s, d), F32),
        compiler_params=pltpu.CompilerParams(
            dimension_semantics=("arbitrary", "arbitrary"), vmem_limit_bytes=VMEM_LIMIT),
        name="mlp",
    )(h2, x1, w1, w2, gf)


def _arrange_w_in(w_in):
    d = w_in.shape[0]
    o = 0
    qa = w_in[:, o:o + 1024] * (A_HEAD_DIM ** -0.5); o += 1024
    ka = w_in[:, o:o + 256]; o += 256
    va = w_in[:, o:o + 256]; o += 256
    cq = w_in[:, o:o + 512]; o += 512
    ckv = w_in[:, o:o + 256]; o += 256
    kr = w_in[:, o:o + B_ROPE]; o += B_ROPE
    ga = w_in[:, o:o + 2048]; o += 2048
    gb = w_in[:, o:o + 2048]
    kr_rot = jnp.concatenate([-kr[:, B_HALF:], kr[:, :B_HALF]], axis=1)
    pad = jnp.zeros((d, N_LAT - 512 - 256 - 2 * B_ROPE), w_in.dtype)
    return jnp.concatenate([qa, ka, va, cq, ckv, kr, kr_rot, pad, ga, gb], axis=1).astype(BF16)


def _arrange_mla_weights(w_uq, w_uk, w_uv):
    c_q = w_uq.shape[0]
    c_kv = w_uk.shape[0]
    qscale = (B_NOPE + B_ROPE) ** -0.5 * math.log2(math.e)
    wq = w_uq * qscale
    wq_pad = jnp.concatenate(
        [wq, jnp.zeros((c_q, B_HEADS, HEAD_PAD - B_NOPE - B_ROPE), wq.dtype)], axis=-1)
    wq_t = wq_pad.reshape(c_q, B_HEADS * HEAD_PAD).T
    rope = wq[:, :, B_NOPE:]
    rot = jnp.concatenate([-rope[..., B_HALF:], rope[..., :B_HALF]], axis=-1)
    wqr_t = rot.reshape(c_q, B_HEADS * B_ROPE).T
    wk_ext = jnp.concatenate(
        [w_uk, jnp.zeros((c_kv, B_HEADS, HEAD_PAD - B_NOPE), w_uk.dtype)], axis=-1
    ).reshape(c_kv, B_HEADS * HEAD_PAD)
    sel = np.zeros((HEAD_PAD, B_HEADS, HEAD_PAD), np.float32)
    for r in range(B_ROPE):
        sel[r, :, B_NOPE + r] = 1.0
    wk_rope = jnp.asarray(sel.reshape(HEAD_PAD, B_HEADS * HEAD_PAD))
    wv_t = w_uv.reshape(c_kv, B_HEADS * B_V).T
    return (wq_t.astype(BF16), wqr_t.astype(BF16), wk_ext.astype(BF16),
            wk_rope.astype(BF16), wv_t.astype(BF16))


def _rope_tables():
    inv = ROPE_THETA ** (-jnp.arange(B_HALF, dtype=F32) / B_HALF)
    inv_col = jnp.concatenate([inv, inv])[:, None]
    inv_row = jnp.concatenate([inv, inv, inv, inv, jnp.zeros((HEAD_PAD - 4 * B_HALF,), F32)])[None, :]
    return inv_col, inv_row


MLA_TQ = 512
MLA_TK = 512


def kernel(x, positions, attn_norm_g, w_in, a_sink, b_q_norm_g, b_kv_norm_g, b_w_uq, b_w_uk, b_w_uv,
           w_branch_a, w_branch_b, w_out, mlp_norm_g, w_mlp_in, w_mlp_out, final_norm_g):
    b, s, d = x.shape
    depth = w_in.shape[0]
    inv_col, inv_row = _rope_tables()
    outs = []
    for bi in range(b):
        xb = x[bi]
        pos_col = positions[bi][:, None]
        pos_row = positions[bi][None, :]
        for l in range(depth):
            w_in_l = _arrange_w_in(w_in[l])
            wq_t, wqr_t, wk_ext, wk_rope, wv_t = _arrange_mla_weights(b_w_uq[l], b_w_uk[l], b_w_uv[l])
            att, lat, gates = _in_proj(xb, attn_norm_g[l][None, :], w_in_l)
            qt, kfull, vt = _mla_prep(lat, pos_col, pos_row, b_q_norm_g[l][None, :], b_kv_norm_g[l][None, :],
                                      wq_t, wqr_t, wk_ext, wk_rope, wv_t, inv_col, inv_row, MLA_TK)
            obt = _mla_attn(qt, kfull, vt, MLA_TQ, MLA_TK)
            oa = _win_attn(att, pos_col, pos_row, a_sink[l])
            x1, h2 = _merge_out(xb, oa, obt, gates, w_branch_a[l].astype(BF16), w_branch_b[l].astype(BF16),
                                w_out[l].astype(BF16), mlp_norm_g[l][None, :])
            xb = _mlp(h2, x1, w_mlp_in[l].astype(BF16), w_mlp_out[l].astype(BF16),
                      final_norm_g[None, :], final_norm=(l == depth - 1))
        outs.append(xb)
    return outs[0][None] if b == 1 else jnp.stack(outs, axis=0)
```

```python
import functools
import math

import jax
import jax.numpy as jnp
import numpy as np
from jax import lax
from jax.experimental import pallas as pl
from jax.experimental.pallas import tpu as pltpu

F32 = jnp.float32
BF16 = jnp.bfloat16

EPS = 1e-6
NEG_INF = -1e30
LOG2E = math.log2(math.e)
MASKED_DIST = 1e33
ROPE_THETA = 10000.0

A_HEADS = 16
A_KV_HEADS = 4
A_GROUP = A_HEADS // A_KV_HEADS
A_HEAD_DIM = 64
WINDOW = 128
BAND = 128

B_HEADS = 16
B_NOPE = 64
B_ROPE = 32
B_V = 64
B_HALF = B_ROPE // 2
HEAD_PAD = 128

VMEM_LIMIT = 52 * 1024 * 1024

N_ATT = 1536
N_LAT = 1024
N_GATE = 4096
IN_TN = 512
J_ATT = N_ATT // IN_TN
J_LAT = N_LAT // IN_TN
J_GATE = N_GATE // IN_TN
NORM_ROWS = 256
MLP_OUT_PIECE = 512


def _rms(x, g):
    ms = jnp.mean(x * x, axis=-1, keepdims=True)
    return x * lax.rsqrt(ms + EPS) * g


def _dot(a, b):
    return jnp.dot(a, b, preferred_element_type=F32)


def _dot_nt(a, b):
    return lax.dot_general(a, b, (((1,), (1,)), ((), ())), preferred_element_type=F32)


def _dot_tn(a, b):
    return lax.dot_general(a, b, (((0,), (0,)), ((), ())), preferred_element_type=F32)


def _in_proj_kernel(x_ref, g_ref, w_ref, att_ref, lat_ref, gate_ref, h_ref):
    j = pl.program_id(1)

    @pl.when(j == 0)
    def _():
        def slab(c, carry):
            rows = pl.ds(pl.multiple_of(c * NORM_ROWS, NORM_ROWS), NORM_ROWS)
            h_ref[rows, :] = _rms(x_ref[rows, :], g_ref[...]).astype(BF16)
            return carry
        lax.fori_loop(0, x_ref.shape[0] // NORM_ROWS, slab, 0)

    @pl.when(j < J_ATT)
    def _():
        att_ref[...] = _dot(h_ref[...], w_ref[...]).astype(BF16)

    @pl.when((j >= J_ATT) & (j < J_ATT + J_LAT))
    def _():
        lat_ref[...] = _dot(h_ref[...], w_ref[...])

    @pl.when(j >= J_ATT + J_LAT)
    def _():
        gate_ref[...] = jax.nn.sigmoid(_dot(h_ref[...], w_ref[...])).astype(BF16)


def _in_proj(x, g, w, tm=1024):
    s, d = x.shape
    n = w.shape[1]
    nj = n // IN_TN
    return pl.pallas_call(
        _in_proj_kernel,
        grid=(s // tm, nj),
        in_specs=[
            pl.BlockSpec((tm, d), lambda i, j: (i, 0)),
            pl.BlockSpec((1, d), lambda i, j: (0, 0)),
            pl.BlockSpec((d, IN_TN), lambda i, j: (0, j)),
        ],
        out_specs=[
            pl.BlockSpec((tm, IN_TN), lambda i, j: (i, jnp.minimum(j, J_ATT - 1))),
            pl.BlockSpec((tm, IN_TN), lambda i, j: (i, jnp.clip(j - J_ATT, 0, J_LAT - 1))),
            pl.BlockSpec((tm, IN_TN), lambda i, j: (i, jnp.clip(j - J_ATT - J_LAT, 0, J_GATE - 1))),
        ],
        out_shape=[
            jax.ShapeDtypeStruct((s, N_ATT), BF16),
            jax.ShapeDtypeStruct((s, N_LAT), F32),
            jax.ShapeDtypeStruct((s, N_GATE), BF16),
        ],
        scratch_shapes=[pltpu.VMEM((tm, d), BF16)],
        compiler_params=pltpu.CompilerParams(
            dimension_semantics=("arbitrary", "arbitrary"), vmem_limit_bytes=VMEM_LIMIT),
        name="in_proj",
    )(x, g, w)


def _mla_prep_kernel(cq_ref, ckv_ref, kr_ref, posc_ref, posr_ref, gq_ref, gkv_ref,
                     wq_ref, wqr_ref, wk_ref, wkr_ref, wv_ref, invc_ref, invr_ref,
                     qt_ref, k_ref, vt_ref):
    cqn = _rms(cq_ref[...], gq_ref[...]).astype(BF16)
    ckvn = _rms(ckv_ref[...], gkv_ref[...]).astype(BF16)
    tm = cqn.shape[0]

    qt = _dot_nt(wq_ref[...], cqn)
    qrt = _dot_nt(wqr_ref[...], cqn)
    ang_t = invc_ref[...] * posr_ref[...].astype(F32)
    cos_t = jnp.cos(ang_t)
    sin_t = jnp.sin(ang_t)
    zeros = jnp.zeros((HEAD_PAD - B_NOPE - B_ROPE, tm), BF16)
    for h in range(B_HEADS):
        r0 = h * HEAD_PAD
        qt_ref[r0:r0 + B_NOPE, :] = qt[r0:r0 + B_NOPE].astype(BF16)
        rope = (qt[r0 + B_NOPE:r0 + B_NOPE + B_ROPE] * cos_t
                + qrt[h * B_ROPE:(h + 1) * B_ROPE] * sin_t)
        qt_ref[r0 + B_NOPE:r0 + B_NOPE + B_ROPE, :] = rope.astype(BF16)
        qt_ref[r0 + B_NOPE + B_ROPE:r0 + HEAD_PAD, :] = zeros

    ang = posc_ref[...].astype(F32) * invr_ref[...]
    lane = lax.broadcasted_iota(jnp.int32, ang.shape, 1)
    cs = jnp.where(lane < B_ROPE, jnp.cos(ang), jnp.where(lane < 2 * B_ROPE, jnp.sin(ang), 0.0))
    u = kr_ref[...] * cs
    k_rope = u + pltpu.roll(u, HEAD_PAD - B_ROPE, 1)
    kfull = _dot(ckvn, wk_ref[...]) + _dot(k_rope.astype(BF16), wkr_ref[...])
    col = lax.broadcasted_iota(jnp.int32, kfull.shape, 1)
    kfull = jnp.where((col & (HEAD_PAD - 1)) == SHIFT_ROW, 1.0, kfull)
    k_ref[...] = kfull.astype(BF16)

    vt_ref[0] = _dot_nt(wv_ref[...], ckvn).astype(BF16)


def _mla_prep(lat, pos_col, pos_row, gq, gkv, wq_t, wqr_t, wk_ext, wk_rope, wv_t, inv_col, inv_row, tm):
    s = lat.shape[0]
    nq = B_HEADS * HEAD_PAD
    nv = B_HEADS * B_V
    const = lambda i: (0, 0)
    return pl.pallas_call(
        _mla_prep_kernel,
        grid=(s // tm,),
        in_specs=[
            pl.BlockSpec((tm, 512), lambda i: (i, 0)),
            pl.BlockSpec((tm, 256), lambda i: (i, 2)),
            pl.BlockSpec((tm, 128), lambda i: (i, 6)),
            pl.BlockSpec((tm, 1), lambda i: (i, 0)),
            pl.BlockSpec((1, tm), lambda i: (0, i)),
            pl.BlockSpec((1, 512), const),
            pl.BlockSpec((1, 256), const),
            pl.BlockSpec(wq_t.shape, const),
            pl.BlockSpec(wqr_t.shape, const),
            pl.BlockSpec(wk_ext.shape, const),
            pl.BlockSpec(wk_rope.shape, const),
            pl.BlockSpec(wv_t.shape, const),
            pl.BlockSpec(inv_col.shape, const),
            pl.BlockSpec(inv_row.shape, const),
        ],
        out_specs=[
            pl.BlockSpec((nq, tm), lambda i: (0, i)),
            pl.BlockSpec((tm, nq), lambda i: (i, 0)),
            pl.BlockSpec((1, nv, tm), lambda i: (i, 0, 0)),
        ],
        out_shape=[
            jax.ShapeDtypeStruct((nq, s), BF16),
            jax.ShapeDtypeStruct((s, nq), BF16),
            jax.ShapeDtypeStruct((s // tm, nv, tm), BF16),
        ],
        compiler_params=pltpu.CompilerParams(
            dimension_semantics=("arbitrary",), vmem_limit_bytes=VMEM_LIMIT),
        name="mla_prep",
    )(lat, lat, lat, pos_col, pos_row, gq, gkv, wq_t, wqr_t, wk_ext, wk_rope, wv_t, inv_col, inv_row)


SHIFT_KEYS = 256
SHIFT_ROW = B_NOPE + B_ROPE
L_MIN = 0.5
L_MAX = 2.0 ** 100


def _sweep(nkb, scores, update, carry, group):
    def blocks(kb0, carry, last):
        for j in range(group):
            if not (last and j == group - 1):
                scores(kb0 + j + 1, (j + 1) % 2)
            carry = update(kb0 + j, j % 2, carry)
        return carry

    scores(0, 0)
    carry = lax.fori_loop(0, nkb // group - 1, lambda i, c: blocks(group * i, c, False), carry)
    return blocks(nkb - group, carry, True)


def _mla_attn_kernel(q_ref, k_ref, v_ref, o_ref, s_ref, qs_ref, *, tk, nkb, group):
    tq = q_ref.shape[1]

    def key_block(kb):
        return k_ref[pl.ds(pl.multiple_of(kb * tk, tk), tk), :]

    s0 = _dot(k_ref[0:SHIFT_KEYS, :], q_ref[...])
    shift = jnp.max(s0, axis=0, keepdims=True)
    row = lax.broadcasted_iota(jnp.int32, (16, tq), 0)
    qs_ref[...] = q_ref[...]
    qs_ref[SHIFT_ROW:SHIFT_ROW + 16, :] = jnp.where(row == 0, -shift, 0.0).astype(BF16)

    def scores_fixed(kb, slot):
        s_ref[slot] = _dot(key_block(kb), qs_ref[...])

    def update_fixed(kb, slot, carry):
        l8, acc = carry
        p = jnp.exp2(s_ref[slot])
        l8 = l8 + jnp.sum(p.reshape(tk // 8, 8, tq), axis=0)
        acc = acc + _dot(v_ref[kb], p.astype(BF16))
        return l8, acc

    l8, acc = _sweep(nkb, scores_fixed, update_fixed,
                     (jnp.zeros((8, tq), F32), jnp.zeros((B_V, tq), F32)), group)
    l = jnp.sum(l8, axis=0, keepdims=True)
    l_ok = (l >= L_MIN) & (l < L_MAX)
    acc_ok = jnp.abs(acc) < L_MAX
    n_bad = jnp.sum(jnp.where(l_ok, 0.0, 1.0)) + jnp.sum(jnp.where(acc_ok, 0.0, 1.0))

    @pl.when(n_bad == 0.0)
    def _():
        o_ref[...] = (acc / l).astype(BF16)

    @pl.when(n_bad != 0.0)
    def _():
        def scores_raw(kb, slot):
            s_ref[slot] = _dot(key_block(kb), q_ref[...])

        def update_online(kb, slot, carry):
            m, l, acc = carry
            m_new = jnp.maximum(m, jnp.max(s_ref[slot], axis=0, keepdims=True))
            alpha = jnp.exp2(m - m_new)
            p = jnp.exp2(s_ref[slot] - m_new)
            l = l * alpha + jnp.sum(p, axis=0, keepdims=True)
            acc = acc * alpha + _dot(v_ref[kb], p.astype(BF16))
            return m_new, l, acc

        _, l, acc = _sweep(nkb, scores_raw, update_online,
                           (jnp.full((1, tq), NEG_INF, F32), jnp.zeros((1, tq), F32),
                            jnp.zeros((B_V, tq), F32)), 2)
        o_ref[...] = (acc / l).astype(BF16)


def _mla_attn(qt, kfull, vt, tq, tk, group=8):
    s = kfull.shape[0]
    nkb = s // tk
    group = min(group, nkb)
    return pl.pallas_call(
        functools.partial(_mla_attn_kernel, tk=tk, nkb=nkb, group=group),
        grid=(B_HEADS, s // tq),
        in_specs=[
            pl.BlockSpec((HEAD_PAD, tq), lambda h, i: (h, i)),
            pl.BlockSpec((s, HEAD_PAD), lambda h, i: (0, h)),
            pl.BlockSpec((nkb, B_V, tk), lambda h, i: (0, h, 0)),
        ],
        out_specs=pl.BlockSpec((B_V, tq), lambda h, i: (h, i)),
        out_shape=jax.ShapeDtypeStruct((B_HEADS * B_V, s), BF16),
        scratch_shapes=[pltpu.VMEM((2, tk, tq), F32), pltpu.VMEM((HEAD_PAD, tq), BF16)],
        compiler_params=pltpu.CompilerParams(
            dimension_semantics=("arbitrary", "arbitrary"), vmem_limit_bytes=VMEM_LIMIT),
        name="mla_attn",
    )(qt, kfull, vt)


def _win_attn_kernel(sink_ref, q_ref, kvp_ref, kvc_ref, kvn_ref, pq_ref, pkp_ref, pkc_ref, pkn_ref,
                     o_ref, *, seq, slopes):
    i = pl.program_id(0)
    hd = A_HEAD_DIM
    qt = q_ref[...].astype(F32).T.astype(BF16)
    kv = jnp.concatenate([kvp_ref[...], kvc_ref[...], kvn_ref[...]], axis=0)
    kw = A_KV_HEADS * hd
    vt = kv[:, kw:].astype(F32).T.astype(BF16)
    pk = jnp.concatenate([pkp_ref[...], pkc_ref[...], pkn_ref[...]], axis=0)
    dist = jnp.abs(pk - pq_ref[...]).astype(F32)
    c = lax.broadcasted_iota(jnp.int32, dist.shape, 0)
    r = lax.broadcasted_iota(jnp.int32, dist.shape, 1)
    kglob = (i - 1) * BAND + c
    mask = (jnp.abs(BAND + r - c) <= WINDOW) & (kglob >= 0) & (kglob < seq)
    dist = jnp.where(mask, dist, MASKED_DIST)
    for g in range(A_KV_HEADS):
        heads = range(g * A_GROUP, (g + 1) * A_GROUP)
        qg = jnp.concatenate([qt[h * hd:(h + 1) * hd] for h in heads], axis=1)
        st = _dot(kv[:, g * hd:(g + 1) * hd], qg)
        st = st - jnp.concatenate([slopes[h] * dist for h in heads], axis=1)
        sk = jnp.concatenate([jnp.full((1, BAND), sink_ref[h] * LOG2E, F32) for h in heads], axis=1)
        m = jnp.maximum(jnp.max(st, axis=0, keepdims=True), sk)
        p = jnp.exp2(st - m)
        denom = jnp.sum(p, axis=0, keepdims=True) + jnp.exp2(sk - m)
        og = _dot(vt[g * hd:(g + 1) * hd], p.astype(BF16)) / denom
        for j, h in enumerate(heads):
            o_ref[h * hd:(h + 1) * hd, :] = og[:, j * BAND:(j + 1) * BAND].astype(BF16)


def _win_attn(att, pos_col, pos_row, sink):
    s = att.shape[0]
    nb = s // BAND
    slopes = tuple(float(2.0 ** (-8.0 * (h + 1) / A_HEADS)) * LOG2E for h in range(A_HEADS))
    kvcol = (A_HEADS * A_HEAD_DIM) // (2 * A_KV_HEADS * A_HEAD_DIM)
    prev = lambda i, sk: jnp.maximum(i - 1, 0)
    nxt = lambda i, sk: jnp.minimum(i + 1, nb - 1)
    grid_spec = pltpu.PrefetchScalarGridSpec(
        num_scalar_prefetch=1,
        grid=(nb,),
        in_specs=[
            pl.BlockSpec((BAND, A_HEADS * A_HEAD_DIM), lambda i, sk: (i, 0)),
            pl.BlockSpec((BAND, 512), lambda i, sk: (prev(i, sk), kvcol)),
            pl.BlockSpec((BAND, 512), lambda i, sk: (i, kvcol)),
            pl.BlockSpec((BAND, 512), lambda i, sk: (nxt(i, sk), kvcol)),
            pl.BlockSpec((1, BAND), lambda i, sk: (0, i)),
            pl.BlockSpec((BAND, 1), lambda i, sk: (prev(i, sk), 0)),
            pl.BlockSpec((BAND, 1), lambda i, sk: (i, 0)),
            pl.BlockSpec((BAND, 1), lambda i, sk: (nxt(i, sk), 0)),
        ],
        out_specs=pl.BlockSpec((A_HEADS * A_HEAD_DIM, BAND), lambda i, sk: (0, i)),
    )
    return pl.pallas_call(
        functools.partial(_win_attn_kernel, seq=s, slopes=slopes),
        grid_spec=grid_spec,
        out_shape=jax.ShapeDtypeStruct((A_HEADS * A_HEAD_DIM, s), BF16),
        compiler_params=pltpu.CompilerParams(
            dimension_semantics=("arbitrary",), vmem_limit_bytes=VMEM_LIMIT),
        name="win_attn",
    )(sink, att, att, att, att, pos_row, pos_col, pos_col, pos_col)


def _merge_out_kernel(x_ref, oat_ref, obt_ref, ga_ref, gb_ref, wa_ref, wb_ref, wo_ref, g2_ref,
                      x1_ref, h2_ref):
    ya = _dot_tn(oat_ref[...], wa_ref[...])
    yb = _dot_tn(obt_ref[...], wb_ref[...])
    merged = ga_ref[...].astype(F32) * ya + gb_ref[...].astype(F32) * yb
    x1 = x_ref[...] + _dot(merged.astype(BF16), wo_ref[...])
    x1_ref[...] = x1
    h2_ref[...] = _rms(x1, g2_ref[...]).astype(BF16)


def _merge_out(x, oat, obt, gates, wa, wb, wo, g2, tm=256):
    s, d = x.shape
    const = lambda i: (0, 0)
    return pl.pallas_call(
        _merge_out_kernel,
        grid=(s // tm,),
        in_specs=[
            pl.BlockSpec((tm, d), lambda i: (i, 0)),
            pl.BlockSpec((oat.shape[0], tm), lambda i: (0, i)),
            pl.BlockSpec((obt.shape[0], tm), lambda i: (0, i)),
            pl.BlockSpec((tm, d), lambda i: (i, 0)),
            pl.BlockSpec((tm, d), lambda i: (i, 1)),
            pl.BlockSpec(wa.shape, const),
            pl.BlockSpec(wb.shape, const),
            pl.BlockSpec(wo.shape, const),
            pl.BlockSpec((1, d), const),
        ],
        out_specs=[
            pl.BlockSpec((tm, d), lambda i: (i, 0)),
            pl.BlockSpec((tm, d), lambda i: (i, 0)),
        ],
        out_shape=[
            jax.ShapeDtypeStruct((s, d), F32),
            jax.ShapeDtypeStruct((s, d), BF16),
        ],
        compiler_params=pltpu.CompilerParams(
            dimension_semantics=("arbitrary",), vmem_limit_bytes=VMEM_LIMIT),
        name="merge_out",
    )(x, oat, obt, gates, gates, wa, wb, wo, g2)


def _mlp_kernel(h2_ref, x1_ref, w1_ref, w2_ref, gf_ref, o_ref, *, nf, final_norm):
    f = pl.program_id(1)

    @pl.when(f == 0)
    def _():
        o_ref[...] = x1_ref[...]

    u = jnp.maximum(_dot(h2_ref[...], w1_ref[...]), 0.0)
    uu = (u * u).astype(BF16)
    for n in range(0, o_ref.shape[1], MLP_OUT_PIECE):
        o_ref[:, n:n + MLP_OUT_PIECE] += _dot(uu, w2_ref[:, n:n + MLP_OUT_PIECE])

    if final_norm:
        @pl.when(f == nf - 1)
        def _():
            o_ref[...] = _rms(o_ref[...], gf_ref[...])


def _mlp(h2, x1, w1, w2, gf, final_norm, tm=512, tf=1024):
    s, d = x1.shape
    nf = w1.shape[1] // tf
    return pl.pallas_call(
        functools.partial(_mlp_kernel, nf=nf, final_norm=final_norm),
        grid=(s // tm, nf),
        in_specs=[
            pl.BlockSpec((tm, d), lambda i, f: (i, 0)),
            pl.BlockSpec((tm, d), lambda i, f: (i, 0)),
            pl.BlockSpec((d, tf), lambda i, f: (0, f)),
            pl.BlockSpec((tf, d), lambda i, f: (f, 0)),
            pl.BlockSpec((1, d), lambda i, f: (0, 0)),
        ],
        out_specs=pl.BlockSpec((tm, d), lambda i, f: (i, 0)),
        out_shape=jax.ShapeDtypeStruct((s, d), F32),
        compiler_params=pltpu.CompilerParams(
            dimension_semantics=("arbitrary", "arbitrary"), vmem_limit_bytes=VMEM_LIMIT),
        name="mlp",
    )(h2, x1, w1, w2, gf)


def _arrange_w_in(w_in):
    d = w_in.shape[0]
    o = 0
    qa = w_in[:, o:o + 1024] * (A_HEAD_DIM ** -0.5 * LOG2E); o += 1024
    ka = w_in[:, o:o + 256]; o += 256
    va = w_in[:, o:o + 256]; o += 256
    cq = w_in[:, o:o + 512]; o += 512
    ckv = w_in[:, o:o + 256]; o += 256
    kr = w_in[:, o:o + B_ROPE]; o += B_ROPE
    ga = w_in[:, o:o + 2048]; o += 2048
    gb = w_in[:, o:o + 2048]
    kr_rot = jnp.concatenate([-kr[:, B_HALF:], kr[:, :B_HALF]], axis=1)
    pad = jnp.zeros((d, N_LAT - 512 - 256 - 2 * B_ROPE), w_in.dtype)
    return jnp.concatenate([qa, ka, va, cq, ckv, kr, kr_rot, pad, ga, gb], axis=1).astype(BF16)


def _arrange_mla_weights(w_uq, w_uk, w_uv):
    c_q = w_uq.shape[0]
    c_kv = w_uk.shape[0]
    qscale = (B_NOPE + B_ROPE) ** -0.5 * math.log2(math.e)
    wq = w_uq * qscale
    wq_pad = jnp.concatenate(
        [wq, jnp.zeros((c_q, B_HEADS, HEAD_PAD - B_NOPE - B_ROPE), wq.dtype)], axis=-1)
    wq_t = wq_pad.reshape(c_q, B_HEADS * HEAD_PAD).T
    rope = wq[:, :, B_NOPE:]
    rot = jnp.concatenate([-rope[..., B_HALF:], rope[..., :B_HALF]], axis=-1)
    wqr_t = rot.reshape(c_q, B_HEADS * B_ROPE).T
    wk_ext = jnp.concatenate(
        [w_uk, jnp.zeros((c_kv, B_HEADS, HEAD_PAD - B_NOPE), w_uk.dtype)], axis=-1
    ).reshape(c_kv, B_HEADS * HEAD_PAD)
    sel = np.zeros((HEAD_PAD, B_HEADS, HEAD_PAD), np.float32)
    for r in range(B_ROPE):
        sel[r, :, B_NOPE + r] = 1.0
    wk_rope = jnp.asarray(sel.reshape(HEAD_PAD, B_HEADS * HEAD_PAD))
    wv_t = w_uv.reshape(c_kv, B_HEADS * B_V).T
    return (wq_t.astype(BF16), wqr_t.astype(BF16), wk_ext.astype(BF16),
            wk_rope.astype(BF16), wv_t.astype(BF16))


def _rope_tables():
    inv = ROPE_THETA ** (-jnp.arange(B_HALF, dtype=F32) / B_HALF)
    inv_col = jnp.concatenate([inv, inv])[:, None]
    inv_row = jnp.concatenate([inv, inv, inv, inv, jnp.zeros((HEAD_PAD - 4 * B_HALF,), F32)])[None, :]
    return inv_col, inv_row


MLA_TQ = 512
MLA_TK = 512


def kernel(x, positions, attn_norm_g, w_in, a_sink, b_q_norm_g, b_kv_norm_g, b_w_uq, b_w_uk, b_w_uv,
           w_branch_a, w_branch_b, w_out, mlp_norm_g, w_mlp_in, w_mlp_out, final_norm_g):
    b, s, d = x.shape
    depth = w_in.shape[0]
    inv_col, inv_row = _rope_tables()
    outs = []
    for bi in range(b):
        xb = x[bi]
        pos_col = positions[bi][:, None]
        pos_row = positions[bi][None, :]
        for l in range(depth):
            w_in_l = _arrange_w_in(w_in[l])
            wq_t, wqr_t, wk_ext, wk_rope, wv_t = _arrange_mla_weights(b_w_uq[l], b_w_uk[l], b_w_uv[l])
            att, lat, gates = _in_proj(xb, attn_norm_g[l][None, :], w_in_l)
            qt, kfull, vt = _mla_prep(lat, pos_col, pos_row, b_q_norm_g[l][None, :], b_kv_norm_g[l][None, :],
                                      wq_t, wqr_t, wk_ext, wk_rope, wv_t, inv_col, inv_row, MLA_TK)
            obt = _mla_attn(qt, kfull, vt, MLA_TQ, MLA_TK)
            oat = _win_attn(att, pos_col, pos_row, a_sink[l])
            x1, h2 = _merge_out(xb, oat, obt, gates, w_branch_a[l].astype(BF16), w_branch_b[l].astype(BF16),
                                w_out[l].astype(BF16), mlp_norm_g[l][None, :])
            xb = _mlp(h2, x1, w_mlp_in[l].astype(BF16), w_mlp_out[l].astype(BF16),
                      final_norm_g[None, :], final_norm=(l == depth - 1))
        outs.append(xb)
    return outs[0][None] if b == 1 else jnp.stack(outs, axis=0)
```

```python
import functools
import math

import jax
import jax.numpy as jnp
import numpy as np
from jax import lax
from jax.experimental import pallas as pl
from jax.experimental.pallas import tpu as pltpu

F32 = jnp.float32
BF16 = jnp.bfloat16

EPS = 1e-6
NEG_INF = -1e30
LOG2E = math.log2(math.e)
MASKED_DIST = 1e33
ROPE_THETA = 10000.0

A_HEADS = 16
A_KV_HEADS = 4
A_GROUP = A_HEADS // A_KV_HEADS
A_HEAD_DIM = 64
WINDOW = 128
BAND = 128

B_HEADS = 16
B_NOPE = 64
B_ROPE = 32
B_V = 64
B_HALF = B_ROPE // 2
HEAD_PAD = 128

VMEM_LIMIT = 52 * 1024 * 1024

N_ATT = 1536
N_LAT = 1024
N_GATE = 4096
IN_TN = 512
J_ATT = N_ATT // IN_TN
J_LAT = N_LAT // IN_TN
J_GATE = N_GATE // IN_TN
NORM_ROWS = 256
MLP_OUT_PIECE = 512


def _rms(x, g):
    ms = jnp.mean(x * x, axis=-1, keepdims=True)
    return x * lax.rsqrt(ms + EPS) * g


def _dot(a, b):
    return jnp.dot(a, b, preferred_element_type=F32)


def _dot_nt(a, b):
    return lax.dot_general(a, b, (((1,), (1,)), ((), ())), preferred_element_type=F32)


def _dot_tn(a, b):
    return lax.dot_general(a, b, (((0,), (0,)), ((), ())), preferred_element_type=F32)


def _in_proj_kernel(x_ref, g_ref, wa_ref, wb_ref, wg_ref, cs_ref, att_ref, lat_ref, gate_ref, h_ref):
    j = pl.program_id(1)

    @pl.when(j == 0)
    def _():
        def slab(c, carry):
            rows = pl.ds(pl.multiple_of(c * NORM_ROWS, NORM_ROWS), NORM_ROWS)
            h_ref[rows, :] = _rms(x_ref[rows, :], g_ref[...]).astype(BF16)
            return carry
        lax.fori_loop(0, x_ref.shape[0] // NORM_ROWS, slab, 0)

    @pl.when(j < J_ATT)
    def _():
        att_ref[...] = (_dot(h_ref[...], wa_ref[...]) * cs_ref[...]).astype(BF16)

    @pl.when(j == J_ATT)
    def _():
        lat_ref[...] = _dot(h_ref[...], wa_ref[...])

    @pl.when(j == J_ATT + 1)
    def _():
        lat_ref[...] = _dot(h_ref[...], wb_ref[...])

    @pl.when(j >= J_ATT + J_LAT)
    def _():
        gate_ref[...] = jax.nn.sigmoid(_dot(h_ref[...], wg_ref[...])).astype(BF16)


def _in_proj(x, g, wa, wb, wg, cs, tm=1024):
    s, d = x.shape
    nj = J_ATT + J_LAT + J_GATE
    return pl.pallas_call(
        _in_proj_kernel,
        grid=(s // tm, nj),
        in_specs=[
            pl.BlockSpec((tm, d), lambda i, j: (i, 0)),
            pl.BlockSpec((1, d), lambda i, j: (0, 0)),
            pl.BlockSpec((d, IN_TN), lambda i, j: (0, jnp.minimum(j, J_ATT))),
            pl.BlockSpec((d, IN_TN), lambda i, j: (0, 0)),
            pl.BlockSpec((d, IN_TN), lambda i, j: (0, jnp.clip(j - J_ATT - J_LAT, 0, J_GATE - 1))),
            pl.BlockSpec((1, IN_TN), lambda i, j: (0, jnp.minimum(j, J_ATT - 1))),
        ],
        out_specs=[
            pl.BlockSpec((tm, IN_TN), lambda i, j: (i, jnp.minimum(j, J_ATT - 1))),
            pl.BlockSpec((tm, IN_TN), lambda i, j: (i, jnp.clip(j - J_ATT, 0, J_LAT - 1))),
            pl.BlockSpec((tm, IN_TN), lambda i, j: (i, jnp.clip(j - J_ATT - J_LAT, 0, J_GATE - 1))),
        ],
        out_shape=[
            jax.ShapeDtypeStruct((s, N_ATT), BF16),
            jax.ShapeDtypeStruct((s, N_LAT), F32),
            jax.ShapeDtypeStruct((s, N_GATE), BF16),
        ],
        scratch_shapes=[pltpu.VMEM((tm, d), BF16)],
        compiler_params=pltpu.CompilerParams(
            dimension_semantics=("arbitrary", "arbitrary"), vmem_limit_bytes=VMEM_LIMIT),
        name="in_proj",
    )(x, g, wa, wb, wg, cs)


def _mla_prep_kernel(cq_ref, ckv_ref, kr_ref, posc_ref, posr_ref, gq_ref, gkv_ref,
                     wq_ref, wqr_ref, wk_ref, wkr_ref, wv_ref, invc_ref, invr_ref,
                     qt_ref, k_ref, vt_ref):
    cqn = _rms(cq_ref[...], gq_ref[...]).astype(BF16)
    ckvn = _rms(ckv_ref[...], gkv_ref[...]).astype(BF16)
    tm = cqn.shape[0]

    qt = _dot_nt(wq_ref[...], cqn)
    qrt = _dot_nt(wqr_ref[...], cqn)
    ang_t = invc_ref[...] * posr_ref[...].astype(F32)
    cos_t = jnp.cos(ang_t)
    sin_t = jnp.sin(ang_t)
    zeros = jnp.zeros((HEAD_PAD - B_NOPE - B_ROPE, tm), BF16)
    for h in range(B_HEADS):
        r0 = h * HEAD_PAD
        qt_ref[r0:r0 + B_NOPE, :] = qt[r0:r0 + B_NOPE].astype(BF16)
        rope = (qt[r0 + B_NOPE:r0 + B_NOPE + B_ROPE] * cos_t
                + qrt[h * B_ROPE:(h + 1) * B_ROPE] * sin_t)
        qt_ref[r0 + B_NOPE:r0 + B_NOPE + B_ROPE, :] = rope.astype(BF16)
        qt_ref[r0 + B_NOPE + B_ROPE:r0 + HEAD_PAD, :] = zeros

    ang = posc_ref[...].astype(F32) * invr_ref[...]
    lane = lax.broadcasted_iota(jnp.int32, ang.shape, 1)
    cs = jnp.where(lane < B_ROPE, jnp.cos(ang), jnp.where(lane < 2 * B_ROPE, jnp.sin(ang), 0.0))
    u = kr_ref[...] * cs
    k_rope = u + pltpu.roll(u, HEAD_PAD - B_ROPE, 1)
    kfull = _dot(ckvn, wk_ref[...]) + _dot(k_rope.astype(BF16), wkr_ref[...])
    col = lax.broadcasted_iota(jnp.int32, kfull.shape, 1)
    kfull = jnp.where((col & (HEAD_PAD - 1)) == SHIFT_ROW, 1.0, kfull)
    k_ref[...] = kfull.astype(BF16)

    vt_ref[0] = _dot_nt(wv_ref[...], ckvn).astype(BF16)


def _mla_prep(lat, pos_col, pos_row, gq, gkv, wq_t, wqr_t, wk_ext, wk_rope, wv_t, inv_col, inv_row, tm):
    s = lat.shape[0]
    nq = B_HEADS * HEAD_PAD
    nv = B_HEADS * B_V
    const = lambda i: (0, 0)
    return pl.pallas_call(
        _mla_prep_kernel,
        grid=(s // tm,),
        in_specs=[
            pl.BlockSpec((tm, 512), lambda i: (i, 0)),
            pl.BlockSpec((tm, 256), lambda i: (i, 2)),
            pl.BlockSpec((tm, 128), lambda i: (i, 6)),
            pl.BlockSpec((tm, 1), lambda i: (i, 0)),
            pl.BlockSpec((1, tm), lambda i: (0, i)),
            pl.BlockSpec((1, 512), const),
            pl.BlockSpec((1, 256), const),
            pl.BlockSpec(wq_t.shape, const),
            pl.BlockSpec(wqr_t.shape, const),
            pl.BlockSpec(wk_ext.shape, const),
            pl.BlockSpec(wk_rope.shape, const),
            pl.BlockSpec(wv_t.shape, const),
            pl.BlockSpec(inv_col.shape, const),
            pl.BlockSpec(inv_row.shape, const),
        ],
        out_specs=[
            pl.BlockSpec((nq, tm), lambda i: (0, i)),
            pl.BlockSpec((tm, nq), lambda i: (i, 0)),
            pl.BlockSpec((1, nv, tm), lambda i: (i, 0, 0)),
        ],
        out_shape=[
            jax.ShapeDtypeStruct((nq, s), BF16),
            jax.ShapeDtypeStruct((s, nq), BF16),
            jax.ShapeDtypeStruct((s // tm, nv, tm), BF16),
        ],
        compiler_params=pltpu.CompilerParams(
            dimension_semantics=("arbitrary",), vmem_limit_bytes=VMEM_LIMIT),
        name="mla_prep",
    )(lat, lat, lat, pos_col, pos_row, gq, gkv, wq_t, wqr_t, wk_ext, wk_rope, wv_t, inv_col, inv_row)


SHIFT_KEYS = 256
SHIFT_ROW = B_NOPE + B_ROPE
L_MIN = 0.5
L_MAX = 2.0 ** 100


SCORE_SLOTS = 2


def _sweep(nkb, scores, update, carry, group):
    def blocks(kb0, carry, last):
        for j in range(group):
            if not (last and j == group - 1):
                scores(kb0 + j + 1, (j + 1) % SCORE_SLOTS)
            carry = update(kb0 + j, j % SCORE_SLOTS, carry)
        return carry

    scores(0, 0)
    if nkb > group:
        carry = lax.fori_loop(0, nkb // group - 1, lambda i, c: blocks(group * i, c, False), carry)
    return blocks(nkb - group, carry, True)


def _mla_attn_kernel(q_ref, k_ref, v_ref, o_ref, s_ref, qs_ref, *, tk, nkb, group):
    tq = q_ref.shape[1]

    def key_block(kb):
        if isinstance(kb, int):
            return k_ref[kb * tk:(kb + 1) * tk, :]
        return k_ref[pl.ds(pl.multiple_of(kb * tk, tk), tk), :]

    s0 = _dot(k_ref[0:SHIFT_KEYS, :], q_ref[...])
    shift = jnp.max(s0, axis=0, keepdims=True)
    row = lax.broadcasted_iota(jnp.int32, (16, tq), 0)
    qs_ref[...] = q_ref[...]
    qs_ref[SHIFT_ROW:SHIFT_ROW + 16, :] = jnp.where(row == 0, -shift, 0.0).astype(BF16)

    def scores_fixed(kb, slot):
        s_ref[slot] = _dot(key_block(kb), qs_ref[...])

    def update_fixed(kb, slot, carry):
        l8, acc = carry
        p = jnp.exp2(s_ref[slot])
        l8 = l8 + jnp.sum(p.reshape(tk // 8, 8, tq), axis=0)
        acc = acc + _dot(v_ref[kb], p.astype(BF16))
        return l8, acc

    l8, acc = _sweep(nkb, scores_fixed, update_fixed,
                     (jnp.zeros((8, tq), F32), jnp.zeros((B_V, tq), F32)), group)
    l = jnp.sum(l8, axis=0, keepdims=True)
    l_ok = (l >= L_MIN) & (l < L_MAX)
    acc_ok = jnp.abs(acc) < L_MAX
    n_bad = jnp.sum(jnp.where(l_ok, 0.0, 1.0)) + jnp.sum(jnp.where(acc_ok, 0.0, 1.0))

    @pl.when(n_bad == 0.0)
    def _():
        o_ref[...] = (acc / l).astype(BF16)

    @pl.when(n_bad != 0.0)
    def _():
        def scores_raw(kb, slot):
            s_ref[slot] = _dot(key_block(kb), q_ref[...])

        def update_online(kb, slot, carry):
            m, l, acc = carry
            m_new = jnp.maximum(m, jnp.max(s_ref[slot], axis=0, keepdims=True))
            alpha = jnp.exp2(m - m_new)
            p = jnp.exp2(s_ref[slot] - m_new)
            l = l * alpha + jnp.sum(p, axis=0, keepdims=True)
            acc = acc * alpha + _dot(v_ref[kb], p.astype(BF16))
            return m_new, l, acc

        _, l, acc = _sweep(nkb, scores_raw, update_online,
                           (jnp.full((1, tq), NEG_INF, F32), jnp.zeros((1, tq), F32),
                            jnp.zeros((B_V, tq), F32)), SCORE_SLOTS)
        o_ref[...] = (acc / l).astype(BF16)


def _mla_attn(qt, kfull, vt, tq, tk, group=32):
    s = kfull.shape[0]
    nkb = s // tk
    group = min(group, nkb)
    return pl.pallas_call(
        functools.partial(_mla_attn_kernel, tk=tk, nkb=nkb, group=group),
        grid=(B_HEADS, s // tq),
        in_specs=[
            pl.BlockSpec((HEAD_PAD, tq), lambda h, i: (h, i)),
            pl.BlockSpec((s, HEAD_PAD), lambda h, i: (0, h)),
            pl.BlockSpec((nkb, B_V, tk), lambda h, i: (0, h, 0)),
        ],
        out_specs=pl.BlockSpec((B_V, tq), lambda h, i: (h, i)),
        out_shape=jax.ShapeDtypeStruct((B_HEADS * B_V, s), BF16),
        scratch_shapes=[pltpu.VMEM((SCORE_SLOTS, tk, tq), F32), pltpu.VMEM((HEAD_PAD, tq), BF16)],
        compiler_params=pltpu.CompilerParams(
            dimension_semantics=("arbitrary", "arbitrary"), vmem_limit_bytes=VMEM_LIMIT),
        name="mla_attn",
    )(qt, kfull, vt)


def _win_attn_kernel(sink_ref, q_ref, kvp_ref, kvc_ref, kvn_ref, pq_ref, pkp_ref, pkc_ref, pkn_ref,
                     o_ref, *, seq, slopes):
    i = pl.program_id(0)
    hd = A_HEAD_DIM
    qt = q_ref[...].astype(F32).T.astype(BF16)
    kv = jnp.concatenate([kvp_ref[...], kvc_ref[...], kvn_ref[...]], axis=0)
    kw = A_KV_HEADS * hd
    vt = kv[:, kw:].astype(F32).T.astype(BF16)
    pk = jnp.concatenate([pkp_ref[...], pkc_ref[...], pkn_ref[...]], axis=0)
    dist = jnp.abs(pk - pq_ref[...]).astype(F32)
    c = lax.broadcasted_iota(jnp.int32, dist.shape, 0)
    r = lax.broadcasted_iota(jnp.int32, dist.shape, 1)
    kglob = (i - 1) * BAND + c
    mask = (jnp.abs(BAND + r - c) <= WINDOW) & (kglob >= 0) & (kglob < seq)
    dist = jnp.where(mask, dist, MASKED_DIST)
    for g in range(A_KV_HEADS):
        heads = range(g * A_GROUP, (g + 1) * A_GROUP)
        qg = jnp.concatenate([qt[h * hd:(h + 1) * hd] for h in heads], axis=1)
        st = _dot(kv[:, g * hd:(g + 1) * hd], qg)
        st = st - jnp.concatenate([slopes[h] * dist for h in heads], axis=1)
        sk = jnp.concatenate([jnp.full((1, BAND), sink_ref[h] * LOG2E, F32) for h in heads], axis=1)
        m = jnp.maximum(jnp.max(st, axis=0, keepdims=True), sk)
        p = jnp.exp2(st - m)
        denom = jnp.sum(p, axis=0, keepdims=True) + jnp.exp2(sk - m)
        og = _dot(vt[g * hd:(g + 1) * hd], p.astype(BF16)) / denom
        for j, h in enumerate(heads):
            o_ref[h * hd:(h + 1) * hd, :] = og[:, j * BAND:(j + 1) * BAND].astype(BF16)


def _win_attn(att, pos_col, pos_row, sink):
    s = att.shape[0]
    nb = s // BAND
    slopes = tuple(float(2.0 ** (-8.0 * (h + 1) / A_HEADS)) * LOG2E for h in range(A_HEADS))
    kvcol = (A_HEADS * A_HEAD_DIM) // (2 * A_KV_HEADS * A_HEAD_DIM)
    prev = lambda i, sk: jnp.maximum(i - 1, 0)
    nxt = lambda i, sk: jnp.minimum(i + 1, nb - 1)
    grid_spec = pltpu.PrefetchScalarGridSpec(
        num_scalar_prefetch=1,
        grid=(nb,),
        in_specs=[
            pl.BlockSpec((BAND, A_HEADS * A_HEAD_DIM), lambda i, sk: (i, 0)),
            pl.BlockSpec((BAND, 512), lambda i, sk: (prev(i, sk), kvcol)),
            pl.BlockSpec((BAND, 512), lambda i, sk: (i, kvcol)),
            pl.BlockSpec((BAND, 512), lambda i, sk: (nxt(i, sk), kvcol)),
            pl.BlockSpec((1, BAND), lambda i, sk: (0, i)),
            pl.BlockSpec((BAND, 1), lambda i, sk: (prev(i, sk), 0)),
            pl.BlockSpec((BAND, 1), lambda i, sk: (i, 0)),
            pl.BlockSpec((BAND, 1), lambda i, sk: (nxt(i, sk), 0)),
        ],
        out_specs=pl.BlockSpec((A_HEADS * A_HEAD_DIM, BAND), lambda i, sk: (0, i)),
    )
    return pl.pallas_call(
        functools.partial(_win_attn_kernel, seq=s, slopes=slopes),
        grid_spec=grid_spec,
        out_shape=jax.ShapeDtypeStruct((A_HEADS * A_HEAD_DIM, s), BF16),
        compiler_params=pltpu.CompilerParams(
            dimension_semantics=("arbitrary",), vmem_limit_bytes=VMEM_LIMIT),
        name="win_attn",
    )(sink, att, att, att, att, pos_row, pos_col, pos_col, pos_col)


def _merge_out_kernel(x_ref, oat_ref, obt_ref, ga_ref, gb_ref, wa_ref, wb_ref, wo_ref, g2_ref,
                      x1_ref, h2_ref):
    ya = _dot_tn(oat_ref[...], wa_ref[...])
    yb = _dot_tn(obt_ref[...], wb_ref[...])
    merged = ga_ref[...].astype(F32) * ya + gb_ref[...].astype(F32) * yb
    x1 = x_ref[...] + _dot(merged.astype(BF16), wo_ref[...])
    x1_ref[...] = x1
    h2_ref[...] = _rms(x1, g2_ref[...]).astype(BF16)


def _merge_out(x, oat, obt, gates, wa, wb, wo, g2, tm=256):
    s, d = x.shape
    const = lambda i: (0, 0)
    return pl.pallas_call(
        _merge_out_kernel,
        grid=(s // tm,),
        in_specs=[
            pl.BlockSpec((tm, d), lambda i: (i, 0)),
            pl.BlockSpec((oat.shape[0], tm), lambda i: (0, i)),
            pl.BlockSpec((obt.shape[0], tm), lambda i: (0, i)),
            pl.BlockSpec((tm, d), lambda i: (i, 0)),
            pl.BlockSpec((tm, d), lambda i: (i, 1)),
            pl.BlockSpec(wa.shape, const),
            pl.BlockSpec(wb.shape, const),
            pl.BlockSpec(wo.shape, const),
            pl.BlockSpec((1, d), const),
        ],
        out_specs=[
            pl.BlockSpec((tm, d), lambda i: (i, 0)),
            pl.BlockSpec((tm, d), lambda i: (i, 0)),
        ],
        out_shape=[
            jax.ShapeDtypeStruct((s, d), F32),
            jax.ShapeDtypeStruct((s, d), BF16),
        ],
        compiler_params=pltpu.CompilerParams(
            dimension_semantics=("arbitrary",), vmem_limit_bytes=VMEM_LIMIT),
        name="merge_out",
    )(x, oat, obt, gates, gates, wa, wb, wo, g2)


def _mlp_kernel(h2_ref, x1_ref, w1_ref, w2_ref, gf_ref, o_ref, *, nf, final_norm):
    f = pl.program_id(1)

    @pl.when(f == 0)
    def _():
        o_ref[...] = x1_ref[...]

    u = jnp.maximum(_dot(h2_ref[...], w1_ref[...]), 0.0)
    uu = (u * u).astype(BF16)
    for n in range(0, o_ref.shape[1], MLP_OUT_PIECE):
        o_ref[:, n:n + MLP_OUT_PIECE] += _dot(uu, w2_ref[:, n:n + MLP_OUT_PIECE])

    if final_norm:
        @pl.when(f == nf - 1)
        def _():
            o_ref[...] = _rms(o_ref[...], gf_ref[...])


def _mlp(h2, x1, w1, w2, gf, final_norm, tm=512, tf=1024):
    s, d = x1.shape
    nf = w1.shape[1] // tf
    return pl.pallas_call(
        functools.partial(_mlp_kernel, nf=nf, final_norm=final_norm),
        grid=(s // tm, nf),
        in_specs=[
            pl.BlockSpec((tm, d), lambda i, f: (i, 0)),
            pl.BlockSpec((tm, d), lambda i, f: (i, 0)),
            pl.BlockSpec((d, tf), lambda i, f: (0, f)),
            pl.BlockSpec((tf, d), lambda i, f: (f, 0)),
            pl.BlockSpec((1, d), lambda i, f: (0, 0)),
        ],
        out_specs=pl.BlockSpec((tm, d), lambda i, f: (i, 0)),
        out_shape=jax.ShapeDtypeStruct((s, d), F32),
        compiler_params=pltpu.CompilerParams(
            dimension_semantics=("arbitrary", "arbitrary"), vmem_limit_bytes=VMEM_LIMIT),
        name="mlp",
    )(h2, x1, w1, w2, gf)


def _arrange_w_in(w_in):
    d = w_in.shape[0]
    n_a = N_ATT + 512
    n_kv = 256 + B_ROPE
    wa = w_in[:, :n_a].astype(BF16)
    kr = w_in[:, n_a + 256:n_a + n_kv]
    kr_rot = jnp.concatenate([-kr[:, B_HALF:], kr[:, :B_HALF]], axis=1)
    pad = jnp.zeros((d, IN_TN - n_kv - B_ROPE), w_in.dtype)
    wb = jnp.concatenate([w_in[:, n_a:n_a + n_kv], kr_rot, pad], axis=1).astype(BF16)
    wg = w_in[:, n_a + n_kv:].astype(BF16)
    n_q = A_HEADS * A_HEAD_DIM
    cs = jnp.concatenate([jnp.full((1, n_q), A_HEAD_DIM ** -0.5 * LOG2E, F32),
                          jnp.ones((1, N_ATT - n_q), F32)], axis=1)
    return wa, wb, wg, cs


def _arrange_mla_weights(w_uq, w_uk, w_uv):
    c_q = w_uq.shape[0]
    c_kv = w_uk.shape[0]
    qscale = (B_NOPE + B_ROPE) ** -0.5 * math.log2(math.e)
    wq = w_uq * qscale
    wq_pad = jnp.concatenate(
        [wq, jnp.zeros((c_q, B_HEADS, HEAD_PAD - B_NOPE - B_ROPE), wq.dtype)], axis=-1)
    wq_t = wq_pad.reshape(c_q, B_HEADS * HEAD_PAD).T
    rope = wq[:, :, B_NOPE:]
    rot = jnp.concatenate([-rope[..., B_HALF:], rope[..., :B_HALF]], axis=-1)
    wqr_t = rot.reshape(c_q, B_HEADS * B_ROPE).T
    wk_ext = jnp.concatenate(
        [w_uk, jnp.zeros((c_kv, B_HEADS, HEAD_PAD - B_NOPE), w_uk.dtype)], axis=-1
    ).reshape(c_kv, B_HEADS * HEAD_PAD)
    sel = np.zeros((HEAD_PAD, B_HEADS, HEAD_PAD), np.float32)
    for r in range(B_ROPE):
        sel[r, :, B_NOPE + r] = 1.0
    wk_rope = jnp.asarray(sel.reshape(HEAD_PAD, B_HEADS * HEAD_PAD))
    wv_t = w_uv.reshape(c_kv, B_HEADS * B_V).T
    return (wq_t.astype(BF16), wqr_t.astype(BF16), wk_ext.astype(BF16),
            wk_rope.astype(BF16), wv_t.astype(BF16))


def _rope_tables():
    inv = ROPE_THETA ** (-jnp.arange(B_HALF, dtype=F32) / B_HALF)
    inv_col = jnp.concatenate([inv, inv])[:, None]
    inv_row = jnp.concatenate([inv, inv, inv, inv, jnp.zeros((HEAD_PAD - 4 * B_HALF,), F32)])[None, :]
    return inv_col, inv_row


MLA_TQ = 512
MLA_TK = 512


def kernel(x, positions, attn_norm_g, w_in, a_sink, b_q_norm_g, b_kv_norm_g, b_w_uq, b_w_uk, b_w_uv,
           w_branch_a, w_branch_b, w_out, mlp_norm_g, w_mlp_in, w_mlp_out, final_norm_g):
    b, s, d = x.shape
    depth = w_in.shape[0]
    inv_col, inv_row = _rope_tables()
    outs = []
    for bi in range(b):
        xb = x[bi]
        pos_col = positions[bi][:, None]
        pos_row = positions[bi][None, :]
        for l in range(depth):
            wa, wb, wg, cs = _arrange_w_in(w_in[l])
            wq_t, wqr_t, wk_ext, wk_rope, wv_t = _arrange_mla_weights(b_w_uq[l], b_w_uk[l], b_w_uv[l])
            att, lat, gates = _in_proj(xb, attn_norm_g[l][None, :], wa, wb, wg, cs)
            qt, kfull, vt = _mla_prep(lat, pos_col, pos_row, b_q_norm_g[l][None, :], b_kv_norm_g[l][None, :],
                                      wq_t, wqr_t, wk_ext, wk_rope, wv_t, inv_col, inv_row, MLA_TK)
            obt = _mla_attn(qt, kfull, vt, MLA_TQ, MLA_TK)
            oat = _win_attn(att, pos_col, pos_row, a_sink[l])
            x1, h2 = _merge_out(xb, oat, obt, gates, w_branch_a[l].astype(BF16), w_branch_b[l].astype(BF16),
                                w_out[l].astype(BF16), mlp_norm_g[l][None, :])
            xb = _mlp(h2, x1, w_mlp_in[l].astype(BF16), w_mlp_out[l].astype(BF16),
                      final_norm_g[None, :], final_norm=(l == depth - 1))
        outs.append(xb)
    return outs[0][None] if b == 1 else jnp.stack(outs, axis=0)
```

```python
import functools
import math

import jax
import jax.numpy as jnp
import numpy as np
from jax import lax
from jax.experimental import pallas as pl
from jax.experimental.pallas import tpu as pltpu

F32 = jnp.float32
BF16 = jnp.bfloat16

EPS = 1e-6
NEG_INF = -1e30
LOG2E = math.log2(math.e)
MASKED_DIST = 1e33
ROPE_THETA = 10000.0

A_HEADS = 16
A_KV_HEADS = 4
A_GROUP = A_HEADS // A_KV_HEADS
A_HEAD_DIM = 64
WINDOW = 128
BAND = 128

B_HEADS = 16
B_NOPE = 64
B_ROPE = 32
B_V = 64
B_HALF = B_ROPE // 2
HEAD_PAD = 128

VMEM_LIMIT = 52 * 1024 * 1024

N_ATT = 1536
N_LAT = 1024
N_GATE = 4096
IN_TN = 512
J_ATT = N_ATT // IN_TN
J_LAT = N_LAT // IN_TN
GATE_TN = 1024
J_GATE = N_GATE // GATE_TN
NORM_ROWS = 256
MLP_OUT_PIECE = 512


def _rms(x, g):
    ms = jnp.mean(x * x, axis=-1, keepdims=True)
    return x * lax.rsqrt(ms + EPS) * g


def _dot(a, b):
    return jnp.dot(a, b, preferred_element_type=F32)


def _dot_nt(a, b):
    return lax.dot_general(a, b, (((1,), (1,)), ((), ())), preferred_element_type=F32)


def _dot_tn(a, b):
    return lax.dot_general(a, b, (((0,), (0,)), ((), ())), preferred_element_type=F32)


def _in_proj_kernel(x_ref, g_ref, wa_ref, wb_ref, wg_ref, cs_ref, att_ref, lat_ref, gate_ref, h_ref):
    j = pl.program_id(1)

    @pl.when(j == 0)
    def _():
        def slab(c, carry):
            rows = pl.ds(pl.multiple_of(c * NORM_ROWS, NORM_ROWS), NORM_ROWS)
            h_ref[rows, :] = _rms(x_ref[rows, :], g_ref[...]).astype(BF16)
            return carry
        lax.fori_loop(0, x_ref.shape[0] // NORM_ROWS, slab, 0)

    @pl.when(j < J_ATT)
    def _():
        att_ref[...] = (_dot(h_ref[...], wa_ref[...]) * cs_ref[...]).astype(BF16)

    @pl.when(j == J_ATT)
    def _():
        lat_ref[...] = _dot(h_ref[...], wa_ref[...])

    @pl.when(j == J_ATT + 1)
    def _():
        lat_ref[...] = _dot(h_ref[...], wb_ref[...])

    @pl.when(j >= J_ATT + J_LAT)
    def _():
        gate_ref[...] = jax.nn.sigmoid(_dot(h_ref[...], wg_ref[...])).astype(BF16)


def _in_proj(x, g, wa, wb, wg, cs, tm=1024):
    s, d = x.shape
    nj = J_ATT + J_LAT + J_GATE
    return pl.pallas_call(
        _in_proj_kernel,
        grid=(s // tm, nj),
        in_specs=[
            pl.BlockSpec((tm, d), lambda i, j: (i, 0)),
            pl.BlockSpec((1, d), lambda i, j: (0, 0)),
            pl.BlockSpec((d, IN_TN), lambda i, j: (0, jnp.minimum(j, J_ATT))),
            pl.BlockSpec((d, IN_TN), lambda i, j: (0, 0)),
            pl.BlockSpec((d, GATE_TN), lambda i, j: (0, jnp.clip(j - J_ATT - J_LAT, 0, J_GATE - 1))),
            pl.BlockSpec((1, IN_TN), lambda i, j: (0, jnp.minimum(j, J_ATT - 1))),
        ],
        out_specs=[
            pl.BlockSpec((tm, IN_TN), lambda i, j: (i, jnp.minimum(j, J_ATT - 1))),
            pl.BlockSpec((tm, IN_TN), lambda i, j: (i, jnp.clip(j - J_ATT, 0, J_LAT - 1))),
            pl.BlockSpec((tm, GATE_TN), lambda i, j: (i, jnp.clip(j - J_ATT - J_LAT, 0, J_GATE - 1))),
        ],
        out_shape=[
            jax.ShapeDtypeStruct((s, N_ATT), BF16),
            jax.ShapeDtypeStruct((s, N_LAT), F32),
            jax.ShapeDtypeStruct((s, N_GATE), BF16),
        ],
        scratch_shapes=[pltpu.VMEM((tm, d), BF16)],
        compiler_params=pltpu.CompilerParams(
            dimension_semantics=("arbitrary", "arbitrary"), vmem_limit_bytes=VMEM_LIMIT),
        name="in_proj",
    )(x, g, wa, wb, wg, cs)


def _mla_prep_kernel(cq_ref, ckv_ref, kr_ref, posc_ref, posr_ref, gq_ref, gkv_ref,
                     wq_ref, wqr_ref, wk_ref, wkr_ref, wv_ref, invc_ref, invr_ref,
                     qt_ref, k_ref, vt_ref):
    cqn = _rms(cq_ref[...], gq_ref[...]).astype(BF16)
    ckvn = _rms(ckv_ref[...], gkv_ref[...]).astype(BF16)
    tm = cqn.shape[0]

    qt = _dot_nt(wq_ref[...], cqn)
    qrt = _dot_nt(wqr_ref[...], cqn)
    ang_t = invc_ref[...] * posr_ref[...].astype(F32)
    cos_t = jnp.cos(ang_t)
    sin_t = jnp.sin(ang_t)
    zeros = jnp.zeros((HEAD_PAD - B_NOPE - B_ROPE, tm), BF16)
    for h in range(B_HEADS):
        r0 = h * HEAD_PAD
        qt_ref[r0:r0 + B_NOPE, :] = qt[r0:r0 + B_NOPE].astype(BF16)
        rope = (qt[r0 + B_NOPE:r0 + B_NOPE + B_ROPE] * cos_t
                + qrt[h * B_ROPE:(h + 1) * B_ROPE] * sin_t)
        qt_ref[r0 + B_NOPE:r0 + B_NOPE + B_ROPE, :] = rope.astype(BF16)
        qt_ref[r0 + B_NOPE + B_ROPE:r0 + HEAD_PAD, :] = zeros

    ang = posc_ref[...].astype(F32) * invr_ref[...]
    lane = lax.broadcasted_iota(jnp.int32, ang.shape, 1)
    cs = jnp.where(lane < B_ROPE, jnp.cos(ang), jnp.where(lane < 2 * B_ROPE, jnp.sin(ang), 0.0))
    u = kr_ref[...] * cs
    k_rope = u + pltpu.roll(u, HEAD_PAD - B_ROPE, 1)
    kfull = _dot(ckvn, wk_ref[...]) + _dot(k_rope.astype(BF16), wkr_ref[...])
    col = lax.broadcasted_iota(jnp.int32, kfull.shape, 1)
    kfull = jnp.where((col & (HEAD_PAD - 1)) == SHIFT_ROW, 1.0, kfull).astype(BF16)
    for h in range(B_HEADS):
        k_ref[h] = kfull[:, h * HEAD_PAD:(h + 1) * HEAD_PAD]

    vt_ref[0] = _dot_nt(wv_ref[...], ckvn).astype(BF16)


def _mla_prep(lat, pos_col, pos_row, gq, gkv, wq_t, wqr_t, wk_ext, wk_rope, wv_t, inv_col, inv_row, tm):
    s = lat.shape[0]
    nq = B_HEADS * HEAD_PAD
    nv = B_HEADS * B_V
    const = lambda i: (0, 0)
    return pl.pallas_call(
        _mla_prep_kernel,
        grid=(s // tm,),
        in_specs=[
            pl.BlockSpec((tm, 512), lambda i: (i, 0)),
            pl.BlockSpec((tm, 256), lambda i: (i, 2)),
            pl.BlockSpec((tm, 128), lambda i: (i, 6)),
            pl.BlockSpec((tm, 1), lambda i: (i, 0)),
            pl.BlockSpec((1, tm), lambda i: (0, i)),
            pl.BlockSpec((1, 512), const),
            pl.BlockSpec((1, 256), const),
            pl.BlockSpec(wq_t.shape, const),
            pl.BlockSpec(wqr_t.shape, const),
            pl.BlockSpec(wk_ext.shape, const),
            pl.BlockSpec(wk_rope.shape, const),
            pl.BlockSpec(wv_t.shape, const),
            pl.BlockSpec(inv_col.shape, const),
            pl.BlockSpec(inv_row.shape, const),
        ],
        out_specs=[
            pl.BlockSpec((nq, tm), lambda i: (0, i)),
            pl.BlockSpec((B_HEADS, tm, HEAD_PAD), lambda i: (0, i, 0)),
            pl.BlockSpec((1, nv, tm), lambda i: (i, 0, 0)),
        ],
        out_shape=[
            jax.ShapeDtypeStruct((nq, s), BF16),
            jax.ShapeDtypeStruct((B_HEADS, s, HEAD_PAD), BF16),
            jax.ShapeDtypeStruct((s // tm, nv, tm), BF16),
        ],
        compiler_params=pltpu.CompilerParams(
            dimension_semantics=("arbitrary",), vmem_limit_bytes=VMEM_LIMIT),
        name="mla_prep",
    )(lat, lat, lat, pos_col, pos_row, gq, gkv, wq_t, wqr_t, wk_ext, wk_rope, wv_t, inv_col, inv_row)


SHIFT_KEYS = 256
SHIFT_ROW = B_NOPE + B_ROPE
L_MIN = 0.5
L_MAX = 2.0 ** 100


SCORE_SLOTS = 2


def _sweep(nkb, scores, update, carry, group):
    def blocks(kb0, carry, last):
        for j in range(group):
            if not (last and j == group - 1):
                scores(kb0 + j + 1, (j + 1) % SCORE_SLOTS)
            carry = update(kb0 + j, j % SCORE_SLOTS, carry)
        return carry

    scores(0, 0)
    if nkb > group:
        carry = lax.fori_loop(0, nkb // group - 1, lambda i, c: blocks(group * i, c, False), carry)
    return blocks(nkb - group, carry, True)


def _mla_attn_kernel(q_ref, qn_ref, k_ref, v_ref, o_ref, s_ref, qs_ref, shift_ref, *, tk, nkb, group):
    tq = q_ref.shape[1]

    def key_block(kb):
        if isinstance(kb, int):
            return k_ref[kb * tk:(kb + 1) * tk, :]
        return k_ref[pl.ds(pl.multiple_of(kb * tk, tk), tk), :]

    def tile_shift(qt_ref):
        s0 = _dot(k_ref[0:SHIFT_KEYS, :], qt_ref[...])
        return jnp.max(s0, axis=0, keepdims=True)

    @pl.when(pl.program_id(1) == 0)
    def _():
        shift_ref[...] = tile_shift(q_ref)

    row = lax.broadcasted_iota(jnp.int32, (16, tq), 0)
    qs_ref[...] = q_ref[...]
    qs_ref[SHIFT_ROW:SHIFT_ROW + 16, :] = jnp.where(row == 0, -shift_ref[...], 0.0).astype(BF16)
    shift_ref[...] = tile_shift(qn_ref)

    def scores_fixed(kb, slot):
        s_ref[slot] = _dot(key_block(kb), qs_ref[...])

    def update_fixed(kb, slot, carry):
        l8, acc = carry
        p = jnp.exp2(s_ref[slot])
        l8 = l8 + jnp.sum(p.reshape(tk // 8, 8, tq), axis=0)
        acc = acc + _dot(v_ref[kb], p.astype(BF16))
        return l8, acc

    l8, acc = _sweep(nkb, scores_fixed, update_fixed,
                     (jnp.zeros((8, tq), F32), jnp.zeros((B_V, tq), F32)), group)
    l = jnp.sum(l8, axis=0, keepdims=True)
    l_ok = (l >= L_MIN) & (l < L_MAX)
    acc_ok = jnp.abs(acc) < L_MAX
    n_bad = jnp.sum(jnp.where(l_ok, 0.0, 1.0)) + jnp.sum(jnp.where(acc_ok, 0.0, 1.0))
    o_ref[...] = (acc / l).astype(BF16)

    @pl.when(n_bad != 0.0)
    def _():
        def scores_raw(kb, slot):
            s_ref[slot] = _dot(key_block(kb), q_ref[...])

        def update_online(kb, slot, carry):
            m, l, acc = carry
            m_new = jnp.maximum(m, jnp.max(s_ref[slot], axis=0, keepdims=True))
            alpha = jnp.exp2(m - m_new)
            p = jnp.exp2(s_ref[slot] - m_new)
            l = l * alpha + jnp.sum(p, axis=0, keepdims=True)
            acc = acc * alpha + _dot(v_ref[kb], p.astype(BF16))
            return m_new, l, acc

        _, l, acc = _sweep(nkb, scores_raw, update_online,
                           (jnp.full((1, tq), NEG_INF, F32), jnp.zeros((1, tq), F32),
                            jnp.zeros((B_V, tq), F32)), SCORE_SLOTS)
        o_ref[...] = (acc / l).astype(BF16)


def _mla_attn(qt, kfull, vt, tq, tk, group=32):
    s = kfull.shape[1]
    nkb = s // tk
    nq = s // tq
    group = min(group, nkb)
    return pl.pallas_call(
        functools.partial(_mla_attn_kernel, tk=tk, nkb=nkb, group=group),
        grid=(B_HEADS, nq),
        in_specs=[
            pl.BlockSpec((HEAD_PAD, tq), lambda h, i: (h, i)),
            pl.BlockSpec((HEAD_PAD, tq), lambda h, i: (h, jnp.minimum(i + 1, nq - 1))),
            pl.BlockSpec((None, s, HEAD_PAD), lambda h, i: (h, 0, 0)),
            pl.BlockSpec((nkb, B_V, tk), lambda h, i: (0, h, 0)),
        ],
        out_specs=pl.BlockSpec((B_V, tq), lambda h, i: (h, i)),
        out_shape=jax.ShapeDtypeStruct((B_HEADS * B_V, s), BF16),
        scratch_shapes=[pltpu.VMEM((SCORE_SLOTS, tk, tq), F32), pltpu.VMEM((HEAD_PAD, tq), BF16),
                        pltpu.VMEM((1, tq), F32)],
        compiler_params=pltpu.CompilerParams(
            dimension_semantics=("arbitrary", "arbitrary"), vmem_limit_bytes=VMEM_LIMIT),
        name="mla_attn",
    )(qt, qt, kfull, vt)


def _win_attn_kernel(sink_ref, q_ref, kvp_ref, kvc_ref, kvn_ref, pq_ref, pkp_ref, pkc_ref, pkn_ref,
                     o_ref, *, seq, slopes):
    i = pl.program_id(0)
    hd = A_HEAD_DIM
    qt = q_ref[...].astype(F32).T.astype(BF16)
    kv = jnp.concatenate([kvp_ref[...], kvc_ref[...], kvn_ref[...]], axis=0)
    kw = A_KV_HEADS * hd
    vt = kv[:, kw:].astype(F32).T.astype(BF16)
    pk = jnp.concatenate([pkp_ref[...], pkc_ref[...], pkn_ref[...]], axis=0)
    dist = jnp.abs(pk - pq_ref[...]).astype(F32)
    c = lax.broadcasted_iota(jnp.int32, dist.shape, 0)
    r = lax.broadcasted_iota(jnp.int32, dist.shape, 1)
    kglob = (i - 1) * BAND + c
    mask = (jnp.abs(BAND + r - c) <= WINDOW) & (kglob >= 0) & (kglob < seq)
    dist = jnp.where(mask, dist, MASKED_DIST)
    for g in range(A_KV_HEADS):
        heads = range(g * A_GROUP, (g + 1) * A_GROUP)
        qg = jnp.concatenate([qt[h * hd:(h + 1) * hd] for h in heads], axis=1)
        st = _dot(kv[:, g * hd:(g + 1) * hd], qg)
        st = st - jnp.concatenate([slopes[h] * dist for h in heads], axis=1)
        sk = jnp.concatenate([jnp.full((1, BAND), sink_ref[h] * LOG2E, F32) for h in heads], axis=1)
        m = jnp.maximum(jnp.max(st, axis=0, keepdims=True), sk)
        p = jnp.exp2(st - m)
        denom = jnp.sum(p, axis=0, keepdims=True) + jnp.exp2(sk - m)
        og = _dot(vt[g * hd:(g + 1) * hd], p.astype(BF16)) / denom
        for j, h in enumerate(heads):
            o_ref[h * hd:(h + 1) * hd, :] = og[:, j * BAND:(j + 1) * BAND].astype(BF16)


def _win_attn(att, pos_col, pos_row, sink):
    s = att.shape[0]
    nb = s // BAND
    slopes = tuple(float(2.0 ** (-8.0 * (h + 1) / A_HEADS)) * LOG2E for h in range(A_HEADS))
    kvcol = (A_HEADS * A_HEAD_DIM) // (2 * A_KV_HEADS * A_HEAD_DIM)
    prev = lambda i, sk: jnp.maximum(i - 1, 0)
    nxt = lambda i, sk: jnp.minimum(i + 1, nb - 1)
    grid_spec = pltpu.PrefetchScalarGridSpec(
        num_scalar_prefetch=1,
        grid=(nb,),
        in_specs=[
            pl.BlockSpec((BAND, A_HEADS * A_HEAD_DIM), lambda i, sk: (i, 0)),
            pl.BlockSpec((BAND, 512), lambda i, sk: (prev(i, sk), kvcol)),
            pl.BlockSpec((BAND, 512), lambda i, sk: (i, kvcol)),
            pl.BlockSpec((BAND, 512), lambda i, sk: (nxt(i, sk), kvcol)),
            pl.BlockSpec((1, BAND), lambda i, sk: (0, i)),
            pl.BlockSpec((BAND, 1), lambda i, sk: (prev(i, sk), 0)),
            pl.BlockSpec((BAND, 1), lambda i, sk: (i, 0)),
            pl.BlockSpec((BAND, 1), lambda i, sk: (nxt(i, sk), 0)),
        ],
        out_specs=pl.BlockSpec((A_HEADS * A_HEAD_DIM, BAND), lambda i, sk: (0, i)),
    )
    return pl.pallas_call(
        functools.partial(_win_attn_kernel, seq=s, slopes=slopes),
        grid_spec=grid_spec,
        out_shape=jax.ShapeDtypeStruct((A_HEADS * A_HEAD_DIM, s), BF16),
        compiler_params=pltpu.CompilerParams(
            dimension_semantics=("arbitrary",), vmem_limit_bytes=VMEM_LIMIT),
        name="win_attn",
    )(sink, att, att, att, att, pos_row, pos_col, pos_col, pos_col)


def _merge_out_kernel(x_ref, oat_ref, obt_ref, ga_ref, gb_ref, wa_ref, wb_ref, wo_ref, g2_ref,
                      x1_ref, h2_ref):
    ya = _dot_tn(oat_ref[...], wa_ref[...])
    yb = _dot_tn(obt_ref[...], wb_ref[...])
    merged = ga_ref[...].astype(F32) * ya + gb_ref[...].astype(F32) * yb
    x1 = x_ref[...] + _dot(merged.astype(BF16), wo_ref[...])
    x1_ref[...] = x1
    h2_ref[...] = _rms(x1, g2_ref[...]).astype(BF16)


def _merge_out(x, oat, obt, gates, wa, wb, wo, g2, tm=256):
    s, d = x.shape
    const = lambda i: (0, 0)
    return pl.pallas_call(
        _merge_out_kernel,
        grid=(s // tm,),
        in_specs=[
            pl.BlockSpec((tm, d), lambda i: (i, 0)),
            pl.BlockSpec((oat.shape[0], tm), lambda i: (0, i)),
            pl.BlockSpec((obt.shape[0], tm), lambda i: (0, i)),
            pl.BlockSpec((tm, d), lambda i: (i, 0)),
            pl.BlockSpec((tm, d), lambda i: (i, 1)),
            pl.BlockSpec(wa.shape, const),
            pl.BlockSpec(wb.shape, const),
            pl.BlockSpec(wo.shape, const),
            pl.BlockSpec((1, d), const),
        ],
        out_specs=[
            pl.BlockSpec((tm, d), lambda i: (i, 0)),
            pl.BlockSpec((tm, d), lambda i: (i, 0)),
        ],
        out_shape=[
            jax.ShapeDtypeStruct((s, d), F32),
            jax.ShapeDtypeStruct((s, d), BF16),
        ],
        compiler_params=pltpu.CompilerParams(
            dimension_semantics=("arbitrary",), vmem_limit_bytes=VMEM_LIMIT),
        name="merge_out",
    )(x, oat, obt, gates, gates, wa, wb, wo, g2)


def _mlp_kernel(h2_ref, x1_ref, w1_ref, w2_ref, gf_ref, o_ref, *, nf, final_norm):
    f = pl.program_id(1)

    @pl.when(f == 0)
    def _():
        o_ref[...] = x1_ref[...]

    u = jnp.maximum(_dot(h2_ref[...], w1_ref[...]), 0.0)
    uu = (u * u).astype(BF16)
    for n in range(0, o_ref.shape[1], MLP_OUT_PIECE):
        o_ref[:, n:n + MLP_OUT_PIECE] += _dot(uu, w2_ref[:, n:n + MLP_OUT_PIECE])

    if final_norm:
        @pl.when(f == nf - 1)
        def _():
            o_ref[...] = _rms(o_ref[...], gf_ref[...])


def _mlp(h2, x1, w1, w2, gf, final_norm, tm=512, tf=1024):
    s, d = x1.shape
    nf = w1.shape[1] // tf
    return pl.pallas_call(
        functools.partial(_mlp_kernel, nf=nf, final_norm=final_norm),
        grid=(s // tm, nf),
        in_specs=[
            pl.BlockSpec((tm, d), lambda i, f: (i, 0)),
            pl.BlockSpec((tm, d), lambda i, f: (i, 0)),
            pl.BlockSpec((d, tf), lambda i, f: (0, f)),
            pl.BlockSpec((tf, d), lambda i, f: (f, 0)),
            pl.BlockSpec((1, d), lambda i, f: (0, 0)),
        ],
        out_specs=pl.BlockSpec((tm, d), lambda i, f: (i, 0)),
        out_shape=jax.ShapeDtypeStruct((s, d), F32),
        compiler_params=pltpu.CompilerParams(
            dimension_semantics=("arbitrary", "arbitrary"), vmem_limit_bytes=VMEM_LIMIT),
        name="mlp",
    )(h2, x1, w1, w2, gf)


def _arrange_w_in(w_in):
    d = w_in.shape[0]
    n_a = N_ATT + 512
    n_kv = 256 + B_ROPE
    wa = w_in[:, :n_a].astype(BF16)
    kr = w_in[:, n_a + 256:n_a + n_kv]
    kr_rot = jnp.concatenate([-kr[:, B_HALF:], kr[:, :B_HALF]], axis=1)
    pad = jnp.zeros((d, IN_TN - n_kv - B_ROPE), w_in.dtype)
    wb = jnp.concatenate([w_in[:, n_a:n_a + n_kv], kr_rot, pad], axis=1).astype(BF16)
    wg = w_in[:, n_a + n_kv:].astype(BF16)
    n_q = A_HEADS * A_HEAD_DIM
    cs = jnp.concatenate([jnp.full((1, n_q), A_HEAD_DIM ** -0.5 * LOG2E, F32),
                          jnp.ones((1, N_ATT - n_q), F32)], axis=1)
    return wa, wb, wg, cs


def _arrange_mla_weights(w_uq, w_uk, w_uv):
    c_q = w_uq.shape[0]
    c_kv = w_uk.shape[0]
    qscale = (B_NOPE + B_ROPE) ** -0.5 * math.log2(math.e)
    wq = w_uq * qscale
    wq_pad = jnp.concatenate(
        [wq, jnp.zeros((c_q, B_HEADS, HEAD_PAD - B_NOPE - B_ROPE), wq.dtype)], axis=-1)
    wq_t = wq_pad.reshape(c_q, B_HEADS * HEAD_PAD).T
    rope = wq[:, :, B_NOPE:]
    rot = jnp.concatenate([-rope[..., B_HALF:], rope[..., :B_HALF]], axis=-1)
    wqr_t = rot.reshape(c_q, B_HEADS * B_ROPE).T
    wk_ext = jnp.concatenate(
        [w_uk, jnp.zeros((c_kv, B_HEADS, HEAD_PAD - B_NOPE), w_uk.dtype)], axis=-1
    ).reshape(c_kv, B_HEADS * HEAD_PAD)
    sel = np.zeros((HEAD_PAD, B_HEADS, HEAD_PAD), np.float32)
    for r in range(B_ROPE):
        sel[r, :, B_NOPE + r] = 1.0
    wk_rope = jnp.asarray(sel.reshape(HEAD_PAD, B_HEADS * HEAD_PAD))
    wv_t = w_uv.reshape(c_kv, B_HEADS * B_V).T
    return (wq_t.astype(BF16), wqr_t.astype(BF16), wk_ext.astype(BF16),
            wk_rope.astype(BF16), wv_t.astype(BF16))


def _rope_tables():
    inv = ROPE_THETA ** (-jnp.arange(B_HALF, dtype=F32) / B_HALF)
    inv_col = jnp.concatenate([inv, inv])[:, None]
    inv_row = jnp.concatenate([inv, inv, inv, inv, jnp.zeros((HEAD_PAD - 4 * B_HALF,), F32)])[None, :]
    return inv_col, inv_row


MLA_TQ = 512
MLA_TK = 512


def kernel(x, positions, attn_norm_g, w_in, a_sink, b_q_norm_g, b_kv_norm_g, b_w_uq, b_w_uk, b_w_uv,
           w_branch_a, w_branch_b, w_out, mlp_norm_g, w_mlp_in, w_mlp_out, final_norm_g):
    b, s, d = x.shape
    depth = w_in.shape[0]
    inv_col, inv_row = _rope_tables()
    outs = []
    for bi in range(b):
        xb = x[bi]
        pos_col = positions[bi][:, None]
        pos_row = positions[bi][None, :]
        for l in range(depth):
            wa, wb, wg, cs = _arrange_w_in(w_in[l])
            wq_t, wqr_t, wk_ext, wk_rope, wv_t = _arrange_mla_weights(b_w_uq[l], b_w_uk[l], b_w_uv[l])
            att, lat, gates = _in_proj(xb, attn_norm_g[l][None, :], wa, wb, wg, cs)
            qt, kfull, vt = _mla_prep(lat, pos_col, pos_row, b_q_norm_g[l][None, :], b_kv_norm_g[l][None, :],
                                      wq_t, wqr_t, wk_ext, wk_rope, wv_t, inv_col, inv_row, MLA_TK)
            obt = _mla_attn(qt, kfull, vt, MLA_TQ, MLA_TK)
            oat = _win_attn(att, pos_col, pos_row, a_sink[l])
            x1, h2 = _merge_out(xb, oat, obt, gates, w_branch_a[l].astype(BF16), w_branch_b[l].astype(BF16),
                                w_out[l].astype(BF16), mlp_norm_g[l][None, :])
            xb = _mlp(h2, x1, w_mlp_in[l].astype(BF16), w_mlp_out[l].astype(BF16),
                      final_norm_g[None, :], final_norm=(l == depth - 1))
        outs.append(xb)
    return outs[0][None] if b == 1 else jnp.stack(outs, axis=0)
```

```python
import functools
import math

import jax
import jax.numpy as jnp
import numpy as np
from jax import lax
from jax.experimental import pallas as pl
from jax.experimental.pallas import tpu as pltpu

F32 = jnp.float32
BF16 = jnp.bfloat16

EPS = 1e-6
NEG_INF = -1e30
LOG2E = math.log2(math.e)
MASKED_DIST = 1e33
ROPE_THETA = 10000.0

A_HEADS = 16
A_KV_HEADS = 4
A_GROUP = A_HEADS // A_KV_HEADS
A_HEAD_DIM = 64
WINDOW = 128
BAND = 128

B_HEADS = 16
B_NOPE = 64
B_ROPE = 32
B_V = 64
V_ROWS = B_V + 16
B_HALF = B_ROPE // 2
HEAD_PAD = 128

VMEM_LIMIT = 52 * 1024 * 1024

N_ATT = 1536
N_LAT = 1024
N_GATE = 4096
IN_TN = 512
J_ATT = N_ATT // IN_TN
J_LAT = N_LAT // IN_TN
GATE_TN = 1024
J_GATE = N_GATE // GATE_TN
NORM_ROWS = 256
MLP_OUT_PIECE = 512


def _rms(x, g):
    ms = jnp.mean(x * x, axis=-1, keepdims=True)
    return x * lax.rsqrt(ms + EPS) * g


def _dot(a, b):
    return jnp.dot(a, b, preferred_element_type=F32)


def _dot_nt(a, b):
    return lax.dot_general(a, b, (((1,), (1,)), ((), ())), preferred_element_type=F32)


def _dot_tn(a, b):
    return lax.dot_general(a, b, (((0,), (0,)), ((), ())), preferred_element_type=F32)


def _in_proj_kernel(x_ref, g_ref, wa_ref, wb_ref, wg_ref, cs_ref, att_ref, lat_ref, gate_ref, h_ref):
    j = pl.program_id(1)

    @pl.when(j == 0)
    def _():
        def slab(c, carry):
            rows = pl.ds(pl.multiple_of(c * NORM_ROWS, NORM_ROWS), NORM_ROWS)
            h_ref[rows, :] = _rms(x_ref[rows, :], g_ref[...]).astype(BF16)
            return carry
        lax.fori_loop(0, x_ref.shape[0] // NORM_ROWS, slab, 0)

    @pl.when(j < J_ATT)
    def _():
        att_ref[...] = (_dot(h_ref[...], wa_ref[...]) * cs_ref[...]).astype(BF16)

    @pl.when(j == J_ATT)
    def _():
        lat_ref[...] = _dot(h_ref[...], wa_ref[...])

    @pl.when(j == J_ATT + 1)
    def _():
        lat_ref[...] = _dot(h_ref[...], wb_ref[...])

    @pl.when(j >= J_ATT + J_LAT)
    def _():
        gate_ref[...] = jax.nn.sigmoid(_dot(h_ref[...], wg_ref[...])).astype(BF16)


def _in_proj(x, g, wa, wb, wg, cs, tm=1024):
    s, d = x.shape
    nj = J_ATT + J_LAT + J_GATE
    return pl.pallas_call(
        _in_proj_kernel,
        grid=(s // tm, nj),
        in_specs=[
            pl.BlockSpec((tm, d), lambda i, j: (i, 0)),
            pl.BlockSpec((1, d), lambda i, j: (0, 0)),
            pl.BlockSpec((d, IN_TN), lambda i, j: (0, jnp.minimum(j, J_ATT))),
            pl.BlockSpec((d, IN_TN), lambda i, j: (0, 0)),
            pl.BlockSpec((d, GATE_TN), lambda i, j: (0, jnp.clip(j - J_ATT - J_LAT, 0, J_GATE - 1))),
            pl.BlockSpec((1, IN_TN), lambda i, j: (0, jnp.minimum(j, J_ATT - 1))),
        ],
        out_specs=[
            pl.BlockSpec((tm, IN_TN), lambda i, j: (i, jnp.minimum(j, J_ATT - 1))),
            pl.BlockSpec((tm, IN_TN), lambda i, j: (i, jnp.clip(j - J_ATT, 0, J_LAT - 1))),
            pl.BlockSpec((tm, GATE_TN), lambda i, j: (i, jnp.clip(j - J_ATT - J_LAT, 0, J_GATE - 1))),
        ],
        out_shape=[
            jax.ShapeDtypeStruct((s, N_ATT), BF16),
            jax.ShapeDtypeStruct((s, N_LAT), F32),
            jax.ShapeDtypeStruct((s, N_GATE), BF16),
        ],
        scratch_shapes=[pltpu.VMEM((tm, d), BF16)],
        compiler_params=pltpu.CompilerParams(
            dimension_semantics=("arbitrary", "arbitrary"), vmem_limit_bytes=VMEM_LIMIT),
        name="in_proj",
    )(x, g, wa, wb, wg, cs)


def _mla_prep_kernel(cq_ref, ckv_ref, kr_ref, posc_ref, posr_ref, gq_ref, gkv_ref,
                     wq_ref, wqr_ref, wk_ref, wkr_ref, wv_ref, invc_ref, invr_ref,
                     qt_ref, k_ref, vt_ref):
    cqn = _rms(cq_ref[...], gq_ref[...]).astype(BF16)
    ckvn = _rms(ckv_ref[...], gkv_ref[...]).astype(BF16)
    tm = cqn.shape[0]

    qt = _dot_nt(wq_ref[...], cqn)
    qrt = _dot_nt(wqr_ref[...], cqn)
    ang_t = invc_ref[...] * posr_ref[...].astype(F32)
    cos_t = jnp.cos(ang_t)
    sin_t = jnp.sin(ang_t)
    zeros = jnp.zeros((HEAD_PAD - B_NOPE - B_ROPE, tm), BF16)
    for h in range(B_HEADS):
        r0 = h * HEAD_PAD
        qt_ref[r0:r0 + B_NOPE, :] = qt[r0:r0 + B_NOPE].astype(BF16)
        rope = (qt[r0 + B_NOPE:r0 + B_NOPE + B_ROPE] * cos_t
                + qrt[h * B_ROPE:(h + 1) * B_ROPE] * sin_t)
        qt_ref[r0 + B_NOPE:r0 + B_NOPE + B_ROPE, :] = rope.astype(BF16)
        qt_ref[r0 + B_NOPE + B_ROPE:r0 + HEAD_PAD, :] = zeros

    ang = posc_ref[...].astype(F32) * invr_ref[...]
    lane = lax.broadcasted_iota(jnp.int32, ang.shape, 1)
    cs = jnp.where(lane < B_ROPE, jnp.cos(ang), jnp.where(lane < 2 * B_ROPE, jnp.sin(ang), 0.0))
    u = kr_ref[...] * cs
    k_rope = u + pltpu.roll(u, HEAD_PAD - B_ROPE, 1)
    kfull = _dot(ckvn, wk_ref[...]) + _dot(k_rope.astype(BF16), wkr_ref[...])
    col = lax.broadcasted_iota(jnp.int32, kfull.shape, 1)
    kfull = jnp.where((col & (HEAD_PAD - 1)) == SHIFT_ROW, 1.0, kfull).astype(BF16)
    for h in range(B_HEADS):
        k_ref[h] = kfull[:, h * HEAD_PAD:(h + 1) * HEAD_PAD]

    vt = _dot_nt(wv_ref[...], ckvn)
    pad_row = lax.broadcasted_iota(jnp.int32, (V_ROWS - B_V, tm), 0)
    ones_block = jnp.where(pad_row == 0, 1.0, 0.0).astype(BF16)
    for h in range(B_HEADS):
        vt_ref[0, h * V_ROWS:h * V_ROWS + B_V, :] = vt[h * B_V:(h + 1) * B_V].astype(BF16)
        vt_ref[0, h * V_ROWS + B_V:(h + 1) * V_ROWS, :] = ones_block


def _mla_prep(lat, pos_col, pos_row, gq, gkv, wq_t, wqr_t, wk_ext, wk_rope, wv_t, inv_col, inv_row, tm):
    s = lat.shape[0]
    nq = B_HEADS * HEAD_PAD
    nv = B_HEADS * V_ROWS
    const = lambda i: (0, 0)
    return pl.pallas_call(
        _mla_prep_kernel,
        grid=(s // tm,),
        in_specs=[
            pl.BlockSpec((tm, 512), lambda i: (i, 0)),
            pl.BlockSpec((tm, 256), lambda i: (i, 2)),
            pl.BlockSpec((tm, 128), lambda i: (i, 6)),
            pl.BlockSpec((tm, 1), lambda i: (i, 0)),
            pl.BlockSpec((1, tm), lambda i: (0, i)),
            pl.BlockSpec((1, 512), const),
            pl.BlockSpec((1, 256), const),
            pl.BlockSpec(wq_t.shape, const),
            pl.BlockSpec(wqr_t.shape, const),
            pl.BlockSpec(wk_ext.shape, const),
            pl.BlockSpec(wk_rope.shape, const),
            pl.BlockSpec(wv_t.shape, const),
            pl.BlockSpec(inv_col.shape, const),
            pl.BlockSpec(inv_row.shape, const),
        ],
        out_specs=[
            pl.BlockSpec((nq, tm), lambda i: (0, i)),
            pl.BlockSpec((B_HEADS, tm, HEAD_PAD), lambda i: (0, i, 0)),
            pl.BlockSpec((1, nv, tm), lambda i: (i, 0, 0)),
        ],
        out_shape=[
            jax.ShapeDtypeStruct((nq, s), BF16),
            jax.ShapeDtypeStruct((B_HEADS, s, HEAD_PAD), BF16),
            jax.ShapeDtypeStruct((s // tm, nv, tm), BF16),
        ],
        compiler_params=pltpu.CompilerParams(
            dimension_semantics=("arbitrary",), vmem_limit_bytes=VMEM_LIMIT),
        name="mla_prep",
    )(lat, lat, lat, pos_col, pos_row, gq, gkv, wq_t, wqr_t, wk_ext, wk_rope, wv_t, inv_col, inv_row)


SHIFT_KEYS = 256
SHIFT_ROW = B_NOPE + B_ROPE
L_MIN = 0.5
L_MAX = 2.0 ** 100


SCORE_SLOTS = 2


def _sweep(nkb, scores, update, carry, group):
    def blocks(kb0, carry, last):
        for j in range(group):
            if not (last and j == group - 1):
                scores(kb0 + j + 1, (j + 1) % SCORE_SLOTS)
            carry = update(kb0 + j, j % SCORE_SLOTS, carry)
        return carry

    scores(0, 0)
    if nkb > group:
        carry = lax.fori_loop(0, nkb // group - 1, lambda i, c: blocks(group * i, c, False), carry)
    return blocks(nkb - group, carry, True)


def _mla_attn_kernel(q_ref, qn_ref, k_ref, v_ref, o_ref, s_ref, qs_ref, shift_ref, *, tk, nkb, group):
    tq = q_ref.shape[1]

    def key_block(kb):
        if isinstance(kb, int):
            return k_ref[kb * tk:(kb + 1) * tk, :]
        return k_ref[pl.ds(pl.multiple_of(kb * tk, tk), tk), :]

    def tile_shift(qt_ref):
        s0 = _dot(k_ref[0:SHIFT_KEYS, :], qt_ref[...])
        return jnp.max(s0, axis=0, keepdims=True)

    @pl.when(pl.program_id(1) == 0)
    def _():
        shift_ref[...] = tile_shift(q_ref)

    row = lax.broadcasted_iota(jnp.int32, (16, tq), 0)
    qs_ref[...] = q_ref[...]
    qs_ref[SHIFT_ROW:SHIFT_ROW + 16, :] = jnp.where(row == 0, -shift_ref[...], 0.0).astype(BF16)
    shift_ref[...] = tile_shift(qn_ref)

    def scores_fixed(kb, slot):
        s_ref[slot] = _dot(key_block(kb), qs_ref[...])

    def update_fixed(kb, slot, acc):
        p = jnp.exp2(s_ref[slot]).astype(BF16)
        return acc + _dot(v_ref[kb], p)

    acc = _sweep(nkb, scores_fixed, update_fixed, jnp.zeros((V_ROWS, tq), F32), group)
    l = acc[B_V:B_V + 1]
    l_ok = (l >= L_MIN) & (l < L_MAX)
    acc_ok = jnp.abs(acc) < L_MAX
    n_bad = jnp.sum(jnp.where(l_ok, 0.0, 1.0)) + jnp.sum(jnp.where(acc_ok, 0.0, 1.0))
    o_ref[...] = (acc[:B_V] / l).astype(BF16)

    @pl.when(n_bad != 0.0)
    def _():
        def scores_raw(kb, slot):
            s_ref[slot] = _dot(key_block(kb), q_ref[...])

        def update_online(kb, slot, carry):
            m, acc = carry
            m_new = jnp.maximum(m, jnp.max(s_ref[slot], axis=0, keepdims=True))
            p = jnp.exp2(s_ref[slot] - m_new).astype(BF16)
            acc = acc * jnp.exp2(m - m_new) + _dot(v_ref[kb], p)
            return m_new, acc

        _, acc = _sweep(nkb, scores_raw, update_online,
                        (jnp.full((1, tq), NEG_INF, F32), jnp.zeros((V_ROWS, tq), F32)), SCORE_SLOTS)
        o_ref[...] = (acc[:B_V] / acc[B_V:B_V + 1]).astype(BF16)


def _mla_attn(qt, kfull, vt, tq, tk, group=32):
    s = kfull.shape[1]
    nkb = s // tk
    nq = s // tq
    group = min(group, nkb)
    return pl.pallas_call(
        functools.partial(_mla_attn_kernel, tk=tk, nkb=nkb, group=group),
        grid=(B_HEADS, nq),
        in_specs=[
            pl.BlockSpec((HEAD_PAD, tq), lambda h, i: (h, i)),
            pl.BlockSpec((HEAD_PAD, tq), lambda h, i: (h, jnp.minimum(i + 1, nq - 1))),
            pl.BlockSpec((None, s, HEAD_PAD), lambda h, i: (h, 0, 0)),
            pl.BlockSpec((nkb, V_ROWS, tk), lambda h, i: (0, h, 0)),
        ],
        out_specs=pl.BlockSpec((B_V, tq), lambda h, i: (h, i)),
        out_shape=jax.ShapeDtypeStruct((B_HEADS * B_V, s), BF16),
        scratch_shapes=[pltpu.VMEM((SCORE_SLOTS, tk, tq), F32), pltpu.VMEM((HEAD_PAD, tq), BF16),
                        pltpu.VMEM((1, tq), F32)],
        compiler_params=pltpu.CompilerParams(
            dimension_semantics=("arbitrary", "arbitrary"), vmem_limit_bytes=VMEM_LIMIT),
        name="mla_attn",
    )(qt, qt, kfull, vt)


def _win_attn_kernel(sink_ref, q_ref, kvp_ref, kvc_ref, kvn_ref, pq_ref, pkp_ref, pkc_ref, pkn_ref,
                     o_ref, *, seq, slopes):
    i = pl.program_id(0)
    hd = A_HEAD_DIM
    qt = q_ref[...].astype(F32).T.astype(BF16)
    kv = jnp.concatenate([kvp_ref[...], kvc_ref[...], kvn_ref[...]], axis=0)
    kw = A_KV_HEADS * hd
    vt = kv[:, kw:].astype(F32).T.astype(BF16)
    pk = jnp.concatenate([pkp_ref[...], pkc_ref[...], pkn_ref[...]], axis=0)
    dist = jnp.abs(pk - pq_ref[...]).astype(F32)
    c = lax.broadcasted_iota(jnp.int32, dist.shape, 0)
    r = lax.broadcasted_iota(jnp.int32, dist.shape, 1)
    kglob = (i - 1) * BAND + c
    mask = (jnp.abs(BAND + r - c) <= WINDOW) & (kglob >= 0) & (kglob < seq)
    dist = jnp.where(mask, dist, MASKED_DIST)
    for g in range(A_KV_HEADS):
        heads = range(g * A_GROUP, (g + 1) * A_GROUP)
        qg = jnp.concatenate([qt[h * hd:(h + 1) * hd] for h in heads], axis=1)
        st = _dot(kv[:, g * hd:(g + 1) * hd], qg)
        st = st - jnp.concatenate([slopes[h] * dist for h in heads], axis=1)
        sk = jnp.concatenate([jnp.full((1, BAND), sink_ref[h] * LOG2E, F32) for h in heads], axis=1)
        m = jnp.maximum(jnp.max(st, axis=0, keepdims=True), sk)
        p = jnp.exp2(st - m)
        denom = jnp.sum(p, axis=0, keepdims=True) + jnp.exp2(sk - m)
        og = _dot(vt[g * hd:(g + 1) * hd], p.astype(BF16)) / denom
        for j, h in enumerate(heads):
            o_ref[h * hd:(h + 1) * hd, :] = og[:, j * BAND:(j + 1) * BAND].astype(BF16)


def _win_attn(att, pos_col, pos_row, sink):
    s = att.shape[0]
    nb = s // BAND
    slopes = tuple(float(2.0 ** (-8.0 * (h + 1) / A_HEADS)) * LOG2E for h in range(A_HEADS))
    kvcol = (A_HEADS * A_HEAD_DIM) // (2 * A_KV_HEADS * A_HEAD_DIM)
    prev = lambda i, sk: jnp.maximum(i - 1, 0)
    nxt = lambda i, sk: jnp.minimum(i + 1, nb - 1)
    grid_spec = pltpu.PrefetchScalarGridSpec(
        num_scalar_prefetch=1,
        grid=(nb,),
        in_specs=[
            pl.BlockSpec((BAND, A_HEADS * A_HEAD_DIM), lambda i, sk: (i, 0)),
            pl.BlockSpec((BAND, 512), lambda i, sk: (prev(i, sk), kvcol)),
            pl.BlockSpec((BAND, 512), lambda i, sk: (i, kvcol)),
            pl.BlockSpec((BAND, 512), lambda i, sk: (nxt(i, sk), kvcol)),
            pl.BlockSpec((1, BAND), lambda i, sk: (0, i)),
            pl.BlockSpec((BAND, 1), lambda i, sk: (prev(i, sk), 0)),
            pl.BlockSpec((BAND, 1), lambda i, sk: (i, 0)),
            pl.BlockSpec((BAND, 1), lambda i, sk: (nxt(i, sk), 0)),
        ],
        out_specs=pl.BlockSpec((A_HEADS * A_HEAD_DIM, BAND), lambda i, sk: (0, i)),
    )
    return pl.pallas_call(
        functools.partial(_win_attn_kernel, seq=s, slopes=slopes),
        grid_spec=grid_spec,
        out_shape=jax.ShapeDtypeStruct((A_HEADS * A_HEAD_DIM, s), BF16),
        compiler_params=pltpu.CompilerParams(
            dimension_semantics=("arbitrary",), vmem_limit_bytes=VMEM_LIMIT),
        name="win_attn",
    )(sink, att, att, att, att, pos_row, pos_col, pos_col, pos_col)


def _merge_out_kernel(x_ref, oat_ref, obt_ref, ga_ref, gb_ref, wa_ref, wb_ref, wo_ref, g2_ref,
                      x1_ref, h2_ref):
    ya = _dot_tn(oat_ref[...], wa_ref[...])
    yb = _dot_tn(obt_ref[...], wb_ref[...])
    merged = ga_ref[...].astype(F32) * ya + gb_ref[...].astype(F32) * yb
    x1 = x_ref[...] + _dot(merged.astype(BF16), wo_ref[...])
    x1_ref[...] = x1
    h2_ref[...] = _rms(x1, g2_ref[...]).astype(BF16)


def _merge_out(x, oat, obt, gates, wa, wb, wo, g2, tm=256):
    s, d = x.shape
    const = lambda i: (0, 0)
    return pl.pallas_call(
        _merge_out_kernel,
        grid=(s // tm,),
        in_specs=[
            pl.BlockSpec((tm, d), lambda i: (i, 0)),
            pl.BlockSpec((oat.shape[0], tm), lambda i: (0, i)),
            pl.BlockSpec((obt.shape[0], tm), lambda i: (0, i)),
            pl.BlockSpec((tm, d), lambda i: (i, 0)),
            pl.BlockSpec((tm, d), lambda i: (i, 1)),
            pl.BlockSpec(wa.shape, const),
            pl.BlockSpec(wb.shape, const),
            pl.BlockSpec(wo.shape, const),
            pl.BlockSpec((1, d), const),
        ],
        out_specs=[
            pl.BlockSpec((tm, d), lambda i: (i, 0)),
            pl.BlockSpec((tm, d), lambda i: (i, 0)),
        ],
        out_shape=[
            jax.ShapeDtypeStruct((s, d), F32),
            jax.ShapeDtypeStruct((s, d), BF16),
        ],
        compiler_params=pltpu.CompilerParams(
            dimension_semantics=("arbitrary",), vmem_limit_bytes=VMEM_LIMIT),
        name="merge_out",
    )(x, oat, obt, gates, gates, wa, wb, wo, g2)


def _mlp_kernel(h2_ref, x1_ref, w1_ref, w2_ref, gf_ref, o_ref, *, nf, final_norm):
    f = pl.program_id(1)

    @pl.when(f == 0)
    def _():
        o_ref[...] = x1_ref[...]

    u = jnp.maximum(_dot(h2_ref[...], w1_ref[...]), 0.0)
    uu = (u * u).astype(BF16)
    for n in range(0, o_ref.shape[1], MLP_OUT_PIECE):
        o_ref[:, n:n + MLP_OUT_PIECE] += _dot(uu, w2_ref[:, n:n + MLP_OUT_PIECE])

    if final_norm:
        @pl.when(f == nf - 1)
        def _():
            o_ref[...] = _rms(o_ref[...], gf_ref[...])


def _mlp(h2, x1, w1, w2, gf, final_norm, tm=512, tf=1024):
    s, d = x1.shape
    nf = w1.shape[1] // tf
    return pl.pallas_call(
        functools.partial(_mlp_kernel, nf=nf, final_norm=final_norm),
        grid=(s // tm, nf),
        in_specs=[
            pl.BlockSpec((tm, d), lambda i, f: (i, 0)),
            pl.BlockSpec((tm, d), lambda i, f: (i, 0)),
            pl.BlockSpec((d, tf), lambda i, f: (0, f)),
            pl.BlockSpec((tf, d), lambda i, f: (f, 0)),
            pl.BlockSpec((1, d), lambda i, f: (0, 0)),
        ],
        out_specs=pl.BlockSpec((tm, d), lambda i, f: (i, 0)),
        out_shape=jax.ShapeDtypeStruct((s, d), F32),
        compiler_params=pltpu.CompilerParams(
            dimension_semantics=("arbitrary", "arbitrary"), vmem_limit_bytes=VMEM_LIMIT),
        name="mlp",
    )(h2, x1, w1, w2, gf)


def _arrange_w_in(w_in):
    d = w_in.shape[0]
    n_a = N_ATT + 512
    n_kv = 256 + B_ROPE
    wa = w_in[:, :n_a].astype(BF16)
    kr = w_in[:, n_a + 256:n_a + n_kv]
    kr_rot = jnp.concatenate([-kr[:, B_HALF:], kr[:, :B_HALF]], axis=1)
    pad = jnp.zeros((d, IN_TN - n_kv - B_ROPE), w_in.dtype)
    wb = jnp.concatenate([w_in[:, n_a:n_a + n_kv], kr_rot, pad], axis=1).astype(BF16)
    wg = w_in[:, n_a + n_kv:].astype(BF16)
    n_q = A_HEADS * A_HEAD_DIM
    cs = jnp.concatenate([jnp.full((1, n_q), A_HEAD_DIM ** -0.5 * LOG2E, F32),
                          jnp.ones((1, N_ATT - n_q), F32)], axis=1)
    return wa, wb, wg, cs


def _arrange_mla_weights(w_uq, w_uk, w_uv):
    c_q = w_uq.shape[0]
    c_kv = w_uk.shape[0]
    qscale = (B_NOPE + B_ROPE) ** -0.5 * math.log2(math.e)
    wq = w_uq * qscale
    wq_pad = jnp.concatenate(
        [wq, jnp.zeros((c_q, B_HEADS, HEAD_PAD - B_NOPE - B_ROPE), wq.dtype)], axis=-1)
    wq_t = wq_pad.reshape(c_q, B_HEADS * HEAD_PAD).T
    rope = wq[:, :, B_NOPE:]
    rot = jnp.concatenate([-rope[..., B_HALF:], rope[..., :B_HALF]], axis=-1)
    wqr_t = rot.reshape(c_q, B_HEADS * B_ROPE).T
    wk_ext = jnp.concatenate(
        [w_uk, jnp.zeros((c_kv, B_HEADS, HEAD_PAD - B_NOPE), w_uk.dtype)], axis=-1
    ).reshape(c_kv, B_HEADS * HEAD_PAD)
    sel = np.zeros((HEAD_PAD, B_HEADS, HEAD_PAD), np.float32)
    for r in range(B_ROPE):
        sel[r, :, B_NOPE + r] = 1.0
    wk_rope = jnp.asarray(sel.reshape(HEAD_PAD, B_HEADS * HEAD_PAD))
    wv_t = w_uv.reshape(c_kv, B_HEADS * B_V).T
    return (wq_t.astype(BF16), wqr_t.astype(BF16), wk_ext.astype(BF16),
            wk_rope.astype(BF16), wv_t.astype(BF16))


def _rope_tables():
    inv = ROPE_THETA ** (-jnp.arange(B_HALF, dtype=F32) / B_HALF)
    inv_col = jnp.concatenate([inv, inv])[:, None]
    inv_row = jnp.concatenate([inv, inv, inv, inv, jnp.zeros((HEAD_PAD - 4 * B_HALF,), F32)])[None, :]
    return inv_col, inv_row


MLA_TQ = 512
MLA_TK = 512


def kernel(x, positions, attn_norm_g, w_in, a_sink, b_q_norm_g, b_kv_norm_g, b_w_uq, b_w_uk, b_w_uv,
           w_branch_a, w_branch_b, w_out, mlp_norm_g, w_mlp_in, w_mlp_out, final_norm_g):
    b, s, d = x.shape
    depth = w_in.shape[0]
    inv_col, inv_row = _rope_tables()
    outs = []
    for bi in range(b):
        xb = x[bi]
        pos_col = positions[bi][:, None]
        pos_row = positions[bi][None, :]
        for l in range(depth):
            wa, wb, wg, cs = _arrange_w_in(w_in[l])
            wq_t, wqr_t, wk_ext, wk_rope, wv_t = _arrange_mla_weights(b_w_uq[l], b_w_uk[l], b_w_uv[l])
            att, lat, gates = _in_proj(xb, attn_norm_g[l][None, :], wa, wb, wg, cs)
            qt, kfull, vt = _mla_prep(lat, pos_col, pos_row, b_q_norm_g[l][None, :], b_kv_norm_g[l][None, :],
                                      wq_t, wqr_t, wk_ext, wk_rope, wv_t, inv_col, inv_row, MLA_TK)
            obt = _mla_attn(qt, kfull, vt, MLA_TQ, MLA_TK)
            oat = _win_attn(att, pos_col, pos_row, a_sink[l])
            x1, h2 = _merge_out(xb, oat, obt, gates, w_branch_a[l].astype(BF16), w_branch_b[l].astype(BF16),
                                w_out[l].astype(BF16), mlp_norm_g[l][None, :])
            xb = _mlp(h2, x1, w_mlp_in[l].astype(BF16), w_mlp_out[l].astype(BF16),
                      final_norm_g[None, :], final_norm=(l == depth - 1))
        outs.append(xb)
    return outs[0][None] if b == 1 else jnp.stack(outs, axis=0)
```

```python
import functools
import math

import jax
import jax.numpy as jnp
import numpy as np
from jax import lax
from jax.experimental import pallas as pl
from jax.experimental.pallas import tpu as pltpu

F32 = jnp.float32
BF16 = jnp.bfloat16

EPS = 1e-6
NEG_INF = -1e30
LOG2E = math.log2(math.e)
MASKED_DIST = 1e33
ROPE_THETA = 10000.0

A_HEADS = 16
A_KV_HEADS = 4
A_GROUP = A_HEADS // A_KV_HEADS
A_HEAD_DIM = 64
WINDOW = 128
BAND = 128

B_HEADS = 16
B_NOPE = 64
B_ROPE = 32
B_V = 64
V_ROWS = B_V + 16
B_HALF = B_ROPE // 2
HEAD_PAD = 128

VMEM_LIMIT = 52 * 1024 * 1024

N_ATT = 1536
N_LAT = 1024
N_GATE = 4096
IN_TN = 512
J_ATT = N_ATT // IN_TN
J_LAT = N_LAT // IN_TN
GATE_TN = 1024
J_GATE = N_GATE // GATE_TN
NORM_ROWS = 256
MLP_OUT_PIECE = 512


def _rms(x, g):
    ms = jnp.mean(x * x, axis=-1, keepdims=True)
    return x * lax.rsqrt(ms + EPS) * g


def _dot(a, b):
    return jnp.dot(a, b, preferred_element_type=F32)


def _dot_nt(a, b):
    return lax.dot_general(a, b, (((1,), (1,)), ((), ())), preferred_element_type=F32)


def _dot_tn(a, b):
    return lax.dot_general(a, b, (((0,), (0,)), ((), ())), preferred_element_type=F32)


def _in_proj_kernel(x_ref, g_ref, wa_ref, wb_ref, wg_ref, cs_ref, att_ref, lat_ref, gate_ref, h_ref):
    j = pl.program_id(1)

    @pl.when(j == 0)
    def _():
        def slab(c, carry):
            rows = pl.ds(pl.multiple_of(c * NORM_ROWS, NORM_ROWS), NORM_ROWS)
            h_ref[rows, :] = _rms(x_ref[rows, :], g_ref[...]).astype(BF16)
            return carry
        lax.fori_loop(0, x_ref.shape[0] // NORM_ROWS, slab, 0)

    @pl.when(j < J_ATT)
    def _():
        att_ref[...] = (_dot_nt(h_ref[...], wa_ref[...]) * cs_ref[...]).astype(BF16)

    @pl.when(j == J_ATT)
    def _():
        lat_ref[...] = _dot_nt(h_ref[...], wa_ref[...])

    @pl.when(j == J_ATT + 1)
    def _():
        lat_ref[...] = _dot_nt(h_ref[...], wb_ref[...])

    @pl.when(j >= J_ATT + J_LAT)
    def _():
        gate_ref[...] = jax.nn.sigmoid(_dot_nt(h_ref[...], wg_ref[...])).astype(BF16)


def _in_proj(x, g, wa, wb, wg, cs, tm=1024):
    s, d = x.shape
    nj = J_ATT + J_LAT + J_GATE
    return pl.pallas_call(
        _in_proj_kernel,
        grid=(s // tm, nj),
        in_specs=[
            pl.BlockSpec((tm, d), lambda i, j: (i, 0)),
            pl.BlockSpec((1, d), lambda i, j: (0, 0)),
            pl.BlockSpec((IN_TN, d), lambda i, j: (jnp.minimum(j, J_ATT), 0)),
            pl.BlockSpec((IN_TN, d), lambda i, j: (0, 0)),
            pl.BlockSpec((GATE_TN, d), lambda i, j: (jnp.clip(j - J_ATT - J_LAT, 0, J_GATE - 1), 0)),
            pl.BlockSpec((1, IN_TN), lambda i, j: (0, jnp.minimum(j, J_ATT - 1))),
        ],
        out_specs=[
            pl.BlockSpec((tm, IN_TN), lambda i, j: (i, jnp.minimum(j, J_ATT - 1))),
            pl.BlockSpec((tm, IN_TN), lambda i, j: (i, jnp.clip(j - J_ATT, 0, J_LAT - 1))),
            pl.BlockSpec((tm, GATE_TN), lambda i, j: (i, jnp.clip(j - J_ATT - J_LAT, 0, J_GATE - 1))),
        ],
        out_shape=[
            jax.ShapeDtypeStruct((s, N_ATT), BF16),
            jax.ShapeDtypeStruct((s, N_LAT), F32),
            jax.ShapeDtypeStruct((s, N_GATE), BF16),
        ],
        scratch_shapes=[pltpu.VMEM((tm, d), BF16)],
        compiler_params=pltpu.CompilerParams(
            dimension_semantics=("arbitrary", "arbitrary"), vmem_limit_bytes=VMEM_LIMIT),
        name="in_proj",
    )(x, g, wa, wb, wg, cs)


def _mla_prep_kernel(cq_ref, ckv_ref, kr_ref, posr_ref, gq_ref, gkv_ref,
                     wq_ref, wqr_ref, wk_ref, wkr_ref, wv_ref, invc_ref,
                     qt_ref, k_ref, vt_ref):
    cqn = _rms(cq_ref[...], gq_ref[...]).astype(BF16)
    ckvn = _rms(ckv_ref[...], gkv_ref[...]).astype(BF16)
    tm = cqn.shape[0]

    qt = _dot_nt(wq_ref[...], cqn)
    qrt = _dot_nt(wqr_ref[...], cqn)
    ang_t = invc_ref[...] * posr_ref[...].astype(F32)
    cos_t = jnp.cos(ang_t)
    sin_t = jnp.sin(ang_t)
    zeros = jnp.zeros((HEAD_PAD - B_NOPE - B_ROPE, tm), BF16)
    for h in range(B_HEADS):
        r0 = h * HEAD_PAD
        qt_ref[r0:r0 + B_NOPE, :] = qt[r0:r0 + B_NOPE].astype(BF16)
        rope = (qt[r0 + B_NOPE:r0 + B_NOPE + B_ROPE] * cos_t
                + qrt[h * B_ROPE:(h + 1) * B_ROPE] * sin_t)
        qt_ref[r0 + B_NOPE:r0 + B_NOPE + B_ROPE, :] = rope.astype(BF16)
        qt_ref[r0 + B_NOPE + B_ROPE:r0 + HEAD_PAD, :] = zeros

    cs = jnp.concatenate([cos_t, sin_t, jnp.zeros((HEAD_PAD - 2 * B_ROPE, tm), F32)], axis=0).T
    u = kr_ref[...] * cs
    k_rope = u + pltpu.roll(u, HEAD_PAD - B_ROPE, 1)
    kfull = _dot(ckvn, wk_ref[...]) + _dot(k_rope.astype(BF16), wkr_ref[...])
    col = lax.broadcasted_iota(jnp.int32, kfull.shape, 1)
    kfull = jnp.where((col & (HEAD_PAD - 1)) == SHIFT_ROW, 1.0, kfull).astype(BF16)
    for h in range(B_HEADS):
        k_ref[h] = kfull[:, h * HEAD_PAD:(h + 1) * HEAD_PAD]

    vt = _dot_nt(wv_ref[...], ckvn)
    pad_row = lax.broadcasted_iota(jnp.int32, (V_ROWS - B_V, tm), 0)
    ones_block = jnp.where(pad_row == 0, 1.0, 0.0).astype(BF16)
    for h in range(B_HEADS):
        vt_ref[0, h * V_ROWS:h * V_ROWS + B_V, :] = vt[h * B_V:(h + 1) * B_V].astype(BF16)
        vt_ref[0, h * V_ROWS + B_V:(h + 1) * V_ROWS, :] = ones_block


def _mla_prep(lat, pos_row, gq, gkv, wq_t, wqr_t, wk_ext, wk_rope, wv_t, inv_col, tm):
    s = lat.shape[0]
    nq = B_HEADS * HEAD_PAD
    nv = B_HEADS * V_ROWS
    const = lambda i: (0, 0)
    return pl.pallas_call(
        _mla_prep_kernel,
        grid=(s // tm,),
        in_specs=[
            pl.BlockSpec((tm, 512), lambda i: (i, 0)),
            pl.BlockSpec((tm, 256), lambda i: (i, 2)),
            pl.BlockSpec((tm, 128), lambda i: (i, 6)),
            pl.BlockSpec((1, tm), lambda i: (0, i)),
            pl.BlockSpec((1, 512), const),
            pl.BlockSpec((1, 256), const),
            pl.BlockSpec(wq_t.shape, const),
            pl.BlockSpec(wqr_t.shape, const),
            pl.BlockSpec(wk_ext.shape, const),
            pl.BlockSpec(wk_rope.shape, const),
            pl.BlockSpec(wv_t.shape, const),
            pl.BlockSpec(inv_col.shape, const),
        ],
        out_specs=[
            pl.BlockSpec((nq, tm), lambda i: (0, i)),
            pl.BlockSpec((B_HEADS, tm, HEAD_PAD), lambda i: (0, i, 0)),
            pl.BlockSpec((1, nv, tm), lambda i: (i, 0, 0)),
        ],
        out_shape=[
            jax.ShapeDtypeStruct((nq, s), BF16),
            jax.ShapeDtypeStruct((B_HEADS, s, HEAD_PAD), BF16),
            jax.ShapeDtypeStruct((s // tm, nv, tm), BF16),
        ],
        compiler_params=pltpu.CompilerParams(
            dimension_semantics=("arbitrary",), vmem_limit_bytes=VMEM_LIMIT),
        name="mla_prep",
    )(lat, lat, lat, pos_row, gq, gkv, wq_t, wqr_t, wk_ext, wk_rope, wv_t, inv_col)


SHIFT_KEYS = 256
SHIFT_ROW = B_NOPE + B_ROPE
L_MIN = 0.5
L_MAX = 2.0 ** 100


SCORE_SLOTS = 2


def _sweep(nkb, scores, update, carry, group):
    def blocks(kb0, carry, last):
        for j in range(group):
            if not (last and j == group - 1):
                scores(kb0 + j + 1, (j + 1) % SCORE_SLOTS)
            carry = update(kb0 + j, j % SCORE_SLOTS, carry)
        return carry

    scores(0, 0)
    if nkb > group:
        carry = lax.fori_loop(0, nkb // group - 1, lambda i, c: blocks(group * i, c, False), carry)
    return blocks(nkb - group, carry, True)


def _mla_attn_kernel(q_ref, qn_ref, k_ref, v_ref, o_ref, s_ref, qs_ref, shift_ref, *, tk, nkb, group):
    tq = q_ref.shape[1]

    def key_block(kb):
        if isinstance(kb, int):
            return k_ref[kb * tk:(kb + 1) * tk, :]
        return k_ref[pl.ds(pl.multiple_of(kb * tk, tk), tk), :]

    def tile_shift(qt_ref):
        s0 = _dot(k_ref[0:SHIFT_KEYS, :], qt_ref[...])
        return jnp.max(s0, axis=0, keepdims=True)

    @pl.when(pl.program_id(1) == 0)
    def _():
        shift_ref[...] = tile_shift(q_ref)

    row = lax.broadcasted_iota(jnp.int32, (16, tq), 0)
    qs_ref[...] = q_ref[...]
    qs_ref[SHIFT_ROW:SHIFT_ROW + 16, :] = jnp.where(row == 0, -shift_ref[...], 0.0).astype(BF16)
    shift_ref[...] = tile_shift(qn_ref)

    def scores_fixed(kb, slot):
        s_ref[slot] = _dot(key_block(kb), qs_ref[...])

    def update_fixed(kb, slot, acc):
        p = jnp.exp2(s_ref[slot]).astype(BF16)
        return acc + _dot(v_ref[kb], p)

    acc = _sweep(nkb, scores_fixed, update_fixed, jnp.zeros((V_ROWS, tq), F32), group)
    l = acc[B_V:B_V + 1]
    l_ok = (l >= L_MIN) & (l < L_MAX)
    acc_ok = jnp.abs(acc) < L_MAX
    n_bad = jnp.sum(jnp.where(l_ok, 0.0, 1.0)) + jnp.sum(jnp.where(acc_ok, 0.0, 1.0))
    o_ref[...] = (acc[:B_V] / l).astype(BF16)

    @pl.when(n_bad != 0.0)
    def _():
        def scores_raw(kb, slot):
            s_ref[slot] = _dot(key_block(kb), q_ref[...])

        def update_online(kb, slot, carry):
            m, acc = carry
            m_new = jnp.maximum(m, jnp.max(s_ref[slot], axis=0, keepdims=True))
            p = jnp.exp2(s_ref[slot] - m_new).astype(BF16)
            acc = acc * jnp.exp2(m - m_new) + _dot(v_ref[kb], p)
            return m_new, acc

        _, acc = _sweep(nkb, scores_raw, update_online,
                        (jnp.full((1, tq), NEG_INF, F32), jnp.zeros((V_ROWS, tq), F32)), SCORE_SLOTS)
        o_ref[...] = (acc[:B_V] / acc[B_V:B_V + 1]).astype(BF16)


def _mla_attn(qt, kfull, vt, tq, tk, group=32):
    s = kfull.shape[1]
    nkb = s // tk
    nq = s // tq
    group = min(group, nkb)
    return pl.pallas_call(
        functools.partial(_mla_attn_kernel, tk=tk, nkb=nkb, group=group),
        grid=(B_HEADS, nq),
        in_specs=[
            pl.BlockSpec((HEAD_PAD, tq), lambda h, i: (h, i)),
            pl.BlockSpec((HEAD_PAD, tq), lambda h, i: (h, jnp.minimum(i + 1, nq - 1))),
            pl.BlockSpec((None, s, HEAD_PAD), lambda h, i: (h, 0, 0)),
            pl.BlockSpec((nkb, V_ROWS, tk), lambda h, i: (0, h, 0)),
        ],
        out_specs=pl.BlockSpec((B_V, tq), lambda h, i: (h, i)),
        out_shape=jax.ShapeDtypeStruct((B_HEADS * B_V, s), BF16),
        scratch_shapes=[pltpu.VMEM((SCORE_SLOTS, tk, tq), F32), pltpu.VMEM((HEAD_PAD, tq), BF16),
                        pltpu.VMEM((1, tq), F32)],
        compiler_params=pltpu.CompilerParams(
            dimension_semantics=("arbitrary", "arbitrary"), vmem_limit_bytes=VMEM_LIMIT),
        name="mla_attn",
    )(qt, qt, kfull, vt)


def _win_attn_kernel(sink_ref, q_ref, kvp_ref, kvc_ref, kvn_ref, pq_ref, pkp_ref, pkc_ref, pkn_ref,
                     o_ref, *, seq, slopes):
    i = pl.program_id(0)
    hd = A_HEAD_DIM
    qt = q_ref[...].astype(F32).T.astype(BF16)
    kv = jnp.concatenate([kvp_ref[...], kvc_ref[...], kvn_ref[...]], axis=0)
    kw = A_KV_HEADS * hd
    vt = kv[:, kw:].astype(F32).T.astype(BF16)
    pk = jnp.concatenate([pkp_ref[...], pkc_ref[...], pkn_ref[...]], axis=0)
    dist = jnp.abs(pk - pq_ref[...]).astype(F32)
    c = lax.broadcasted_iota(jnp.int32, dist.shape, 0)
    r = lax.broadcasted_iota(jnp.int32, dist.shape, 1)
    kglob = (i - 1) * BAND + c
    mask = (jnp.abs(BAND + r - c) <= WINDOW) & (kglob >= 0) & (kglob < seq)
    dist = jnp.where(mask, dist, MASKED_DIST)
    for g in range(A_KV_HEADS):
        heads = range(g * A_GROUP, (g + 1) * A_GROUP)
        qg = jnp.concatenate([qt[h * hd:(h + 1) * hd] for h in heads], axis=1)
        st = _dot(kv[:, g * hd:(g + 1) * hd], qg)
        st = st - jnp.concatenate([slopes[h] * dist for h in heads], axis=1)
        sk = jnp.concatenate([jnp.full((1, BAND), sink_ref[h] * LOG2E, F32) for h in heads], axis=1)
        m = jnp.maximum(jnp.max(st, axis=0, keepdims=True), sk)
        p = jnp.exp2(st - m)
        denom = jnp.sum(p, axis=0, keepdims=True) + jnp.exp2(sk - m)
        og = _dot(vt[g * hd:(g + 1) * hd], p.astype(BF16)) / denom
        for j, h in enumerate(heads):
            o_ref[h * hd:(h + 1) * hd, :] = og[:, j * BAND:(j + 1) * BAND].astype(BF16)


def _win_attn(att, pos_col, pos_row, sink):
    s = att.shape[0]
    nb = s // BAND
    slopes = tuple(float(2.0 ** (-8.0 * (h + 1) / A_HEADS)) * LOG2E for h in range(A_HEADS))
    kvcol = (A_HEADS * A_HEAD_DIM) // (2 * A_KV_HEADS * A_HEAD_DIM)
    prev = lambda i, sk: jnp.maximum(i - 1, 0)
    nxt = lambda i, sk: jnp.minimum(i + 1, nb - 1)
    grid_spec = pltpu.PrefetchScalarGridSpec(
        num_scalar_prefetch=1,
        grid=(nb,),
        in_specs=[
            pl.BlockSpec((BAND, A_HEADS * A_HEAD_DIM), lambda i, sk: (i, 0)),
            pl.BlockSpec((BAND, 512), lambda i, sk: (prev(i, sk), kvcol)),
            pl.BlockSpec((BAND, 512), lambda i, sk: (i, kvcol)),
            pl.BlockSpec((BAND, 512), lambda i, sk: (nxt(i, sk), kvcol)),
            pl.BlockSpec((1, BAND), lambda i, sk: (0, i)),
            pl.BlockSpec((BAND, 1), lambda i, sk: (prev(i, sk), 0)),
            pl.BlockSpec((BAND, 1), lambda i, sk: (i, 0)),
            pl.BlockSpec((BAND, 1), lambda i, sk: (nxt(i, sk), 0)),
        ],
        out_specs=pl.BlockSpec((A_HEADS * A_HEAD_DIM, BAND), lambda i, sk: (0, i)),
    )
    return pl.pallas_call(
        functools.partial(_win_attn_kernel, seq=s, slopes=slopes),
        grid_spec=grid_spec,
        out_shape=jax.ShapeDtypeStruct((A_HEADS * A_HEAD_DIM, s), BF16),
        compiler_params=pltpu.CompilerParams(
            dimension_semantics=("arbitrary",), vmem_limit_bytes=VMEM_LIMIT),
        name="win_attn",
    )(sink, att, att, att, att, pos_row, pos_col, pos_col, pos_col)


def _merge_out_kernel(x_ref, oat_ref, obt_ref, ga_ref, gb_ref, wa_ref, wb_ref, wo_ref, g2_ref,
                      x1_ref, h2_ref):
    ya = _dot_tn(oat_ref[...], wa_ref[...])
    yb = _dot_tn(obt_ref[...], wb_ref[...])
    merged = ga_ref[...].astype(F32) * ya + gb_ref[...].astype(F32) * yb
    x1 = x_ref[...] + _dot(merged.astype(BF16), wo_ref[...])
    x1_ref[...] = x1
    h2_ref[...] = _rms(x1, g2_ref[...]).astype(BF16)


def _merge_out(x, oat, obt, gates, wa, wb, wo, g2, tm=256):
    s, d = x.shape
    const = lambda i: (0, 0)
    return pl.pallas_call(
        _merge_out_kernel,
        grid=(s // tm,),
        in_specs=[
            pl.BlockSpec((tm, d), lambda i: (i, 0)),
            pl.BlockSpec((oat.shape[0], tm), lambda i: (0, i)),
            pl.BlockSpec((obt.shape[0], tm), lambda i: (0, i)),
            pl.BlockSpec((tm, d), lambda i: (i, 0)),
            pl.BlockSpec((tm, d), lambda i: (i, 1)),
            pl.BlockSpec(wa.shape, const),
            pl.BlockSpec(wb.shape, const),
            pl.BlockSpec(wo.shape, const),
            pl.BlockSpec((1, d), const),
        ],
        out_specs=[
            pl.BlockSpec((tm, d), lambda i: (i, 0)),
            pl.BlockSpec((tm, d), lambda i: (i, 0)),
        ],
        out_shape=[
            jax.ShapeDtypeStruct((s, d), F32),
            jax.ShapeDtypeStruct((s, d), BF16),
        ],
        compiler_params=pltpu.CompilerParams(
            dimension_semantics=("arbitrary",), vmem_limit_bytes=VMEM_LIMIT),
        name="merge_out",
    )(x, oat, obt, gates, gates, wa, wb, wo, g2)


def _mlp_kernel(h2_ref, x1_ref, w1_ref, w2_ref, gf_ref, o_ref, *, nf, final_norm):
    f = pl.program_id(1)

    @pl.when(f == 0)
    def _():
        o_ref[...] = x1_ref[...]

    u = jnp.maximum(_dot(h2_ref[...], w1_ref[...]), 0.0)
    uu = (u * u).astype(BF16)
    for n in range(0, o_ref.shape[1], MLP_OUT_PIECE):
        o_ref[:, n:n + MLP_OUT_PIECE] += _dot(uu, w2_ref[:, n:n + MLP_OUT_PIECE])

    if final_norm:
        @pl.when(f == nf - 1)
        def _():
            o_ref[...] = _rms(o_ref[...], gf_ref[...])


def _mlp(h2, x1, w1, w2, gf, final_norm, tm=512, tf=1024):
    s, d = x1.shape
    nf = w1.shape[1] // tf
    return pl.pallas_call(
        functools.partial(_mlp_kernel, nf=nf, final_norm=final_norm),
        grid=(s // tm, nf),
        in_specs=[
            pl.BlockSpec((tm, d), lambda i, f: (i, 0)),
            pl.BlockSpec((tm, d), lambda i, f: (i, 0)),
            pl.BlockSpec((d, tf), lambda i, f: (0, f)),
            pl.BlockSpec((tf, d), lambda i, f: (f, 0)),
            pl.BlockSpec((1, d), lambda i, f: (0, 0)),
        ],
        out_specs=pl.BlockSpec((tm, d), lambda i, f: (i, 0)),
        out_shape=jax.ShapeDtypeStruct((s, d), F32),
        compiler_params=pltpu.CompilerParams(
            dimension_semantics=("arbitrary", "arbitrary"), vmem_limit_bytes=VMEM_LIMIT),
        name="mlp",
    )(h2, x1, w1, w2, gf)


def _arrange_w_in(w_in):
    d = w_in.shape[0]
    wt = w_in.T
    n_a = N_ATT + 512
    n_kv = 256 + B_ROPE
    wa = wt[:n_a].astype(BF16)
    kr = wt[n_a + 256:n_a + n_kv]
    kr_rot = jnp.concatenate([-kr[B_HALF:], kr[:B_HALF]], axis=0)
    pad = jnp.zeros((IN_TN - n_kv - B_ROPE, d), w_in.dtype)
    wb = jnp.concatenate([wt[n_a:n_a + n_kv], kr_rot, pad], axis=0).astype(BF16)
    wg = wt[n_a + n_kv:].astype(BF16)
    n_q = A_HEADS * A_HEAD_DIM
    cs = jnp.concatenate([jnp.full((1, n_q), A_HEAD_DIM ** -0.5 * LOG2E, F32),
                          jnp.ones((1, N_ATT - n_q), F32)], axis=1)
    return wa, wb, wg, cs


def _arrange_mla_weights(w_uq, w_uk, w_uv):
    c_q = w_uq.shape[0]
    c_kv = w_uk.shape[0]
    qscale = (B_NOPE + B_ROPE) ** -0.5 * math.log2(math.e)
    wq = w_uq * qscale
    wq_pad = jnp.concatenate(
        [wq, jnp.zeros((c_q, B_HEADS, HEAD_PAD - B_NOPE - B_ROPE), wq.dtype)], axis=-1)
    wq_t = wq_pad.reshape(c_q, B_HEADS * HEAD_PAD).T
    rope = wq[:, :, B_NOPE:]
    rot = jnp.concatenate([-rope[..., B_HALF:], rope[..., :B_HALF]], axis=-1)
    wqr_t = rot.reshape(c_q, B_HEADS * B_ROPE).T
    wk_ext = jnp.concatenate(
        [w_uk, jnp.zeros((c_kv, B_HEADS, HEAD_PAD - B_NOPE), w_uk.dtype)], axis=-1
    ).reshape(c_kv, B_HEADS * HEAD_PAD)
    sel = np.zeros((HEAD_PAD, B_HEADS, HEAD_PAD), np.float32)
    for r in range(B_ROPE):
        sel[r, :, B_NOPE + r] = 1.0
    wk_rope = jnp.asarray(sel.reshape(HEAD_PAD, B_HEADS * HEAD_PAD))
    wv_t = w_uv.reshape(c_kv, B_HEADS * B_V).T
    return (wq_t.astype(BF16), wqr_t.astype(BF16), wk_ext.astype(BF16),
            wk_rope.astype(BF16), wv_t.astype(BF16))


def _rope_inv_freq():
    inv = ROPE_THETA ** (-jnp.arange(B_HALF, dtype=F32) / B_HALF)
    return jnp.concatenate([inv, inv])[:, None]


MLA_TQ = 512
MLA_TK = 512


def kernel(x, positions, attn_norm_g, w_in, a_sink, b_q_norm_g, b_kv_norm_g, b_w_uq, b_w_uk, b_w_uv,
           w_branch_a, w_branch_b, w_out, mlp_norm_g, w_mlp_in, w_mlp_out, final_norm_g):
    b, s, d = x.shape
    depth = w_in.shape[0]
    inv_col = _rope_inv_freq()
    outs = []
    for bi in range(b):
        xb = x[bi]
        pos_col = positions[bi][:, None]
        pos_row = positions[bi][None, :]
        for l in range(depth):
            wa, wb, wg, cs = _arrange_w_in(w_in[l])
            wq_t, wqr_t, wk_ext, wk_rope, wv_t = _arrange_mla_weights(b_w_uq[l], b_w_uk[l], b_w_uv[l])
            att, lat, gates = _in_proj(xb, attn_norm_g[l][None, :], wa, wb, wg, cs)
            qt, kfull, vt = _mla_prep(lat, pos_row, b_q_norm_g[l][None, :], b_kv_norm_g[l][None, :],
                                      wq_t, wqr_t, wk_ext, wk_rope, wv_t, inv_col, MLA_TK)
            obt = _mla_attn(qt, kfull, vt, MLA_TQ, MLA_TK)
            oat = _win_attn(att, pos_col, pos_row, a_sink[l])
            x1, h2 = _merge_out(xb, oat, obt, gates, w_branch_a[l].astype(BF16), w_branch_b[l].astype(BF16),
                                w_out[l].astype(BF16), mlp_norm_g[l][None, :])
            xb = _mlp(h2, x1, w_mlp_in[l].astype(BF16), w_mlp_out[l].astype(BF16),
                      final_norm_g[None, :], final_norm=(l == depth - 1))
        outs.append(xb)
    return outs[0][None] if b == 1 else jnp.stack(outs, axis=0)
```

```python
import functools
import math

import jax
import jax.numpy as jnp
import numpy as np
from jax import lax
from jax.experimental import pallas as pl
from jax.experimental.pallas import tpu as pltpu

F32 = jnp.float32
BF16 = jnp.bfloat16

EPS = 1e-6
NEG_INF = -1e30
LOG2E = math.log2(math.e)
MASKED_DIST = 1e33
ROPE_THETA = 10000.0

A_HEADS = 16
A_KV_HEADS = 4
A_GROUP = A_HEADS // A_KV_HEADS
A_HEAD_DIM = 64
WINDOW = 128
BAND = 128

B_HEADS = 16
B_NOPE = 64
B_ROPE = 32
B_V = 64
V_ROWS = B_V + 16
B_HALF = B_ROPE // 2
HEAD_PAD = 128

VMEM_LIMIT = 52 * 1024 * 1024

N_ATT = 1536
N_LAT = 1024
N_GATE = 4096
IN_TN = 512
J_ATT = N_ATT // IN_TN
J_LAT = N_LAT // IN_TN
GATE_TN = 1024
J_GATE = N_GATE // GATE_TN
NORM_ROWS = 256
MLP_OUT_PIECE = 512


def _rms(x, g):
    ms = jnp.mean(x * x, axis=-1, keepdims=True)
    return x * lax.rsqrt(ms + EPS) * g


def _dot(a, b):
    return jnp.dot(a, b, preferred_element_type=F32)


def _dot_nt(a, b):
    return lax.dot_general(a, b, (((1,), (1,)), ((), ())), preferred_element_type=F32)


def _dot_tn(a, b):
    return lax.dot_general(a, b, (((0,), (0,)), ((), ())), preferred_element_type=F32)


def _in_proj_kernel(x_ref, g_ref, wa_ref, wb_ref, wg_ref, cs_ref, att_ref, lat_ref, gate_ref, h_ref):
    j = pl.program_id(1)

    @pl.when(j == 0)
    def _():
        def slab(c, carry):
            rows = pl.ds(pl.multiple_of(c * NORM_ROWS, NORM_ROWS), NORM_ROWS)
            h_ref[rows, :] = _rms(x_ref[rows, :], g_ref[...]).astype(BF16)
            return carry
        lax.fori_loop(0, x_ref.shape[0] // NORM_ROWS, slab, 0)

    @pl.when(j < J_ATT)
    def _():
        att_ref[...] = (_dot_nt(h_ref[...], wa_ref[...]) * cs_ref[...]).astype(BF16)

    @pl.when(j == J_ATT)
    def _():
        lat_ref[...] = _dot_nt(h_ref[...], wa_ref[...])

    @pl.when(j == J_ATT + 1)
    def _():
        lat_ref[...] = _dot_nt(h_ref[...], wb_ref[...])

    @pl.when(j >= J_ATT + J_LAT)
    def _():
        gate_ref[...] = jax.nn.sigmoid(_dot_nt(h_ref[...], wg_ref[...])).astype(BF16)


def _in_proj(x, g, wa, wb, wg, cs, tm=1024):
    s, d = x.shape
    nj = J_ATT + J_LAT + J_GATE
    return pl.pallas_call(
        _in_proj_kernel,
        grid=(s // tm, nj),
        in_specs=[
            pl.BlockSpec((tm, d), lambda i, j: (i, 0)),
            pl.BlockSpec((1, d), lambda i, j: (0, 0)),
            pl.BlockSpec((IN_TN, d), lambda i, j: (jnp.minimum(j, J_ATT), 0)),
            pl.BlockSpec((IN_TN, d), lambda i, j: (0, 0)),
            pl.BlockSpec((GATE_TN, d), lambda i, j: (jnp.clip(j - J_ATT - J_LAT, 0, J_GATE - 1), 0)),
            pl.BlockSpec((1, IN_TN), lambda i, j: (0, jnp.minimum(j, J_ATT - 1))),
        ],
        out_specs=[
            pl.BlockSpec((tm, IN_TN), lambda i, j: (i, jnp.minimum(j, J_ATT - 1))),
            pl.BlockSpec((tm, IN_TN), lambda i, j: (i, jnp.clip(j - J_ATT, 0, J_LAT - 1))),
            pl.BlockSpec((tm, GATE_TN), lambda i, j: (i, jnp.clip(j - J_ATT - J_LAT, 0, J_GATE - 1))),
        ],
        out_shape=[
            jax.ShapeDtypeStruct((s, N_ATT), BF16),
            jax.ShapeDtypeStruct((s, N_LAT), F32),
            jax.ShapeDtypeStruct((s, N_GATE), BF16),
        ],
        scratch_shapes=[pltpu.VMEM((tm, d), BF16)],
        compiler_params=pltpu.CompilerParams(
            dimension_semantics=("arbitrary", "arbitrary"), vmem_limit_bytes=VMEM_LIMIT),
        name="in_proj",
    )(x, g, wa, wb, wg, cs)


def _mla_prep_kernel(cq_ref, ckv_ref, kr_ref, posr_ref, gq_ref, gkv_ref,
                     wq_ref, wqr_ref, wk_ref, wkr_ref, wv_ref, invc_ref,
                     qt_ref, k_ref, vt_ref):
    cqn = _rms(cq_ref[...], gq_ref[...]).astype(BF16)
    ckvn = _rms(ckv_ref[...], gkv_ref[...]).astype(BF16)
    tm = cqn.shape[0]

    qt = _dot_nt(wq_ref[...], cqn)
    qrt = _dot_nt(wqr_ref[...], cqn)
    ang_t = invc_ref[...] * posr_ref[...].astype(F32)
    cos_t = jnp.cos(ang_t)
    sin_t = jnp.sin(ang_t)
    zeros = jnp.zeros((HEAD_PAD - B_NOPE - B_ROPE, tm), BF16)
    for h in range(B_HEADS):
        r0 = h * HEAD_PAD
        qt_ref[r0:r0 + B_NOPE, :] = qt[r0:r0 + B_NOPE].astype(BF16)
        rope = (qt[r0 + B_NOPE:r0 + B_NOPE + B_ROPE] * cos_t
                + qrt[h * B_ROPE:(h + 1) * B_ROPE] * sin_t)
        qt_ref[r0 + B_NOPE:r0 + B_NOPE + B_ROPE, :] = rope.astype(BF16)
        qt_ref[r0 + B_NOPE + B_ROPE:r0 + HEAD_PAD, :] = zeros

    cs = jnp.concatenate([cos_t, sin_t, jnp.zeros((HEAD_PAD - 2 * B_ROPE, tm), F32)], axis=0).T
    u = kr_ref[...] * cs
    k_rope = u + pltpu.roll(u, HEAD_PAD - B_ROPE, 1)
    kfull = _dot(ckvn, wk_ref[...]) + _dot(k_rope.astype(BF16), wkr_ref[...])
    col = lax.broadcasted_iota(jnp.int32, kfull.shape, 1)
    kfull = jnp.where((col & (HEAD_PAD - 1)) == SHIFT_ROW, 1.0, kfull).astype(BF16)
    for h in range(B_HEADS):
        k_ref[h] = kfull[:, h * HEAD_PAD:(h + 1) * HEAD_PAD]

    vt = _dot_nt(wv_ref[...], ckvn)
    pad_row = lax.broadcasted_iota(jnp.int32, (V_ROWS - B_V, tm), 0)
    ones_block = jnp.where(pad_row == 0, 1.0, 0.0).astype(BF16)
    for h in range(B_HEADS):
        vt_ref[0, h * V_ROWS:h * V_ROWS + B_V, :] = vt[h * B_V:(h + 1) * B_V].astype(BF16)
        vt_ref[0, h * V_ROWS + B_V:(h + 1) * V_ROWS, :] = ones_block


def _mla_prep(lat, pos_row, gq, gkv, wq_t, wqr_t, wk_ext, wk_rope, wv_t, inv_col, tm):
    s = lat.shape[0]
    nq = B_HEADS * HEAD_PAD
    nv = B_HEADS * V_ROWS
    const = lambda i: (0, 0)
    return pl.pallas_call(
        _mla_prep_kernel,
        grid=(s // tm,),
        in_specs=[
            pl.BlockSpec((tm, 512), lambda i: (i, 0)),
            pl.BlockSpec((tm, 256), lambda i: (i, 2)),
            pl.BlockSpec((tm, 128), lambda i: (i, 6)),
            pl.BlockSpec((1, tm), lambda i: (0, i)),
            pl.BlockSpec((1, 512), const),
            pl.BlockSpec((1, 256), const),
            pl.BlockSpec(wq_t.shape, const),
            pl.BlockSpec(wqr_t.shape, const),
            pl.BlockSpec(wk_ext.shape, const),
            pl.BlockSpec(wk_rope.shape, const),
            pl.BlockSpec(wv_t.shape, const),
            pl.BlockSpec(inv_col.shape, const),
        ],
        out_specs=[
            pl.BlockSpec((nq, tm), lambda i: (0, i)),
            pl.BlockSpec((B_HEADS, tm, HEAD_PAD), lambda i: (0, i, 0)),
            pl.BlockSpec((1, nv, tm), lambda i: (i, 0, 0)),
        ],
        out_shape=[
            jax.ShapeDtypeStruct((nq, s), BF16),
            jax.ShapeDtypeStruct((B_HEADS, s, HEAD_PAD), BF16),
            jax.ShapeDtypeStruct((s // tm, nv, tm), BF16),
        ],
        compiler_params=pltpu.CompilerParams(
            dimension_semantics=("arbitrary",), vmem_limit_bytes=VMEM_LIMIT),
        name="mla_prep",
    )(lat, lat, lat, pos_row, gq, gkv, wq_t, wqr_t, wk_ext, wk_rope, wv_t, inv_col)


SHIFT_KEYS = 256
SHIFT_ROW = B_NOPE + B_ROPE
L_MIN = 0.5
L_MAX = 2.0 ** 100


SCORE_SLOTS = 2


def _sweep(nkb, scores, update, carry, group):
    def blocks(kb0, carry, last):
        for j in range(group):
            if not (last and j == group - 1):
                scores(kb0 + j + 1, (j + 1) % SCORE_SLOTS)
            carry = update(kb0 + j, j % SCORE_SLOTS, carry)
        return carry

    scores(0, 0)
    if nkb > group:
        carry = lax.fori_loop(0, nkb // group - 1, lambda i, c: blocks(group * i, c, False), carry)
    return blocks(nkb - group, carry, True)


def _mla_attn_kernel(q_ref, qn_ref, k_ref, v_ref, o_ref, s_ref, qs_ref, shift_ref, *, tk, nkb, group):
    tq = q_ref.shape[1]

    def key_block(kb):
        if isinstance(kb, int):
            return k_ref[kb * tk:(kb + 1) * tk, :]
        return k_ref[pl.ds(pl.multiple_of(kb * tk, tk), tk), :]

    def tile_shift(qt_ref):
        s0 = _dot(k_ref[0:SHIFT_KEYS, :], qt_ref[...])
        return jnp.max(s0, axis=0, keepdims=True)

    @pl.when(pl.program_id(1) == 0)
    def _():
        shift_ref[...] = tile_shift(q_ref)

    row = lax.broadcasted_iota(jnp.int32, (16, tq), 0)
    qs_ref[...] = q_ref[...]
    qs_ref[SHIFT_ROW:SHIFT_ROW + 16, :] = jnp.where(row == 0, -shift_ref[...], 0.0).astype(BF16)
    shift_ref[...] = tile_shift(qn_ref)

    def scores_fixed(kb, slot):
        s_ref[slot] = _dot(key_block(kb).astype(F32), qs_ref[...].astype(F32))

    def update_fixed(kb, slot, acc):
        p = jnp.exp2(s_ref[slot]).astype(BF16)
        return acc + _dot(v_ref[kb], p)

    acc = _sweep(nkb, scores_fixed, update_fixed, jnp.zeros((V_ROWS, tq), F32), group)
    l = acc[B_V:B_V + 1]
    l_ok = (l >= L_MIN) & (l < L_MAX)
    acc_ok = jnp.abs(acc) < L_MAX
    n_bad = jnp.sum(jnp.where(l_ok, 0.0, 1.0)) + jnp.sum(jnp.where(acc_ok, 0.0, 1.0))
    o_ref[...] = (acc[:B_V] / l).astype(BF16)

    @pl.when(n_bad != 0.0)
    def _():
        def scores_raw(kb, slot):
            s_ref[slot] = _dot(key_block(kb), q_ref[...])

        def update_online(kb, slot, carry):
            m, acc = carry
            m_new = jnp.maximum(m, jnp.max(s_ref[slot], axis=0, keepdims=True))
            p = jnp.exp2(s_ref[slot] - m_new).astype(BF16)
            acc = acc * jnp.exp2(m - m_new) + _dot(v_ref[kb], p)
            return m_new, acc

        _, acc = _sweep(nkb, scores_raw, update_online,
                        (jnp.full((1, tq), NEG_INF, F32), jnp.zeros((V_ROWS, tq), F32)), SCORE_SLOTS)
        o_ref[...] = (acc[:B_V] / acc[B_V:B_V + 1]).astype(BF16)


def _mla_attn(qt, kfull, vt, tq, tk, group=32):
    s = kfull.shape[1]
    nkb = s // tk
    nq = s // tq
    group = min(group, nkb)
    return pl.pallas_call(
        functools.partial(_mla_attn_kernel, tk=tk, nkb=nkb, group=group),
        grid=(B_HEADS, nq),
        in_specs=[
            pl.BlockSpec((HEAD_PAD, tq), lambda h, i: (h, i)),
            pl.BlockSpec((HEAD_PAD, tq), lambda h, i: (h, jnp.minimum(i + 1, nq - 1))),
            pl.BlockSpec((None, s, HEAD_PAD), lambda h, i: (h, 0, 0)),
            pl.BlockSpec((nkb, V_ROWS, tk), lambda h, i: (0, h, 0)),
        ],
        out_specs=pl.BlockSpec((B_V, tq), lambda h, i: (h, i)),
        out_shape=jax.ShapeDtypeStruct((B_HEADS * B_V, s), BF16),
        scratch_shapes=[pltpu.VMEM((SCORE_SLOTS, tk, tq), F32), pltpu.VMEM((HEAD_PAD, tq), BF16),
                        pltpu.VMEM((1, tq), F32)],
        compiler_params=pltpu.CompilerParams(
            dimension_semantics=("arbitrary", "arbitrary"), vmem_limit_bytes=VMEM_LIMIT),
        name="mla_attn",
    )(qt, qt, kfull, vt)


def _win_attn_kernel(sink_ref, q_ref, kvp_ref, kvc_ref, kvn_ref, pq_ref, pkp_ref, pkc_ref, pkn_ref,
                     o_ref, *, seq, slopes):
    i = pl.program_id(0)
    hd = A_HEAD_DIM
    qt = q_ref[...].astype(F32).T.astype(BF16)
    kv = jnp.concatenate([kvp_ref[...], kvc_ref[...], kvn_ref[...]], axis=0)
    kw = A_KV_HEADS * hd
    vt = kv[:, kw:].astype(F32).T.astype(BF16)
    pk = jnp.concatenate([pkp_ref[...], pkc_ref[...], pkn_ref[...]], axis=0)
    dist = jnp.abs(pk - pq_ref[...]).astype(F32)
    c = lax.broadcasted_iota(jnp.int32, dist.shape, 0)
    r = lax.broadcasted_iota(jnp.int32, dist.shape, 1)
    kglob = (i - 1) * BAND + c
    mask = (jnp.abs(BAND + r - c) <= WINDOW) & (kglob >= 0) & (kglob < seq)
    dist = jnp.where(mask, dist, MASKED_DIST)
    for g in range(A_KV_HEADS):
        heads = range(g * A_GROUP, (g + 1) * A_GROUP)
        qg = jnp.concatenate([qt[h * hd:(h + 1) * hd] for h in heads], axis=1)
        st = _dot(kv[:, g * hd:(g + 1) * hd].astype(F32), qg.astype(F32))
        st = st - jnp.concatenate([slopes[h] * dist for h in heads], axis=1)
        sk = jnp.concatenate([jnp.full((1, BAND), sink_ref[h] * LOG2E, F32) for h in heads], axis=1)
        m = jnp.maximum(jnp.max(st, axis=0, keepdims=True), sk)
        p = jnp.exp2(st - m)
        denom = jnp.sum(p, axis=0, keepdims=True) + jnp.exp2(sk - m)
        og = _dot(vt[g * hd:(g + 1) * hd], p.astype(BF16)) / denom
        for j, h in enumerate(heads):
            o_ref[h * hd:(h + 1) * hd, :] = og[:, j * BAND:(j + 1) * BAND].astype(BF16)


def _win_attn(att, pos_col, pos_row, sink):
    s = att.shape[0]
    nb = s // BAND
    slopes = tuple(float(2.0 ** (-8.0 * (h + 1) / A_HEADS)) * LOG2E for h in range(A_HEADS))
    kvcol = (A_HEADS * A_HEAD_DIM) // (2 * A_KV_HEADS * A_HEAD_DIM)
    prev = lambda i, sk: jnp.maximum(i - 1, 0)
    nxt = lambda i, sk: jnp.minimum(i + 1, nb - 1)
    grid_spec = pltpu.PrefetchScalarGridSpec(
        num_scalar_prefetch=1,
        grid=(nb,),
        in_specs=[
            pl.BlockSpec((BAND, A_HEADS * A_HEAD_DIM), lambda i, sk: (i, 0)),
            pl.BlockSpec((BAND, 512), lambda i, sk: (prev(i, sk), kvcol)),
            pl.BlockSpec((BAND, 512), lambda i, sk: (i, kvcol)),
            pl.BlockSpec((BAND, 512), lambda i, sk: (nxt(i, sk), kvcol)),
            pl.BlockSpec((1, BAND), lambda i, sk: (0, i)),
            pl.BlockSpec((BAND, 1), lambda i, sk: (prev(i, sk), 0)),
            pl.BlockSpec((BAND, 1), lambda i, sk: (i, 0)),
            pl.BlockSpec((BAND, 1), lambda i, sk: (nxt(i, sk), 0)),
        ],
        out_specs=pl.BlockSpec((A_HEADS * A_HEAD_DIM, BAND), lambda i, sk: (0, i)),
    )
    return pl.pallas_call(
        functools.partial(_win_attn_kernel, seq=s, slopes=slopes),
        grid_spec=grid_spec,
        out_shape=jax.ShapeDtypeStruct((A_HEADS * A_HEAD_DIM, s), BF16),
        compiler_params=pltpu.CompilerParams(
            dimension_semantics=("arbitrary",), vmem_limit_bytes=VMEM_LIMIT),
        name="win_attn",
    )(sink, att, att, att, att, pos_row, pos_col, pos_col, pos_col)


def _merge_out_kernel(x_ref, oat_ref, obt_ref, ga_ref, gb_ref, wa_ref, wb_ref, wo_ref, g2_ref,
                      x1_ref, h2_ref):
    ya = _dot_tn(oat_ref[...], wa_ref[...])
    yb = _dot_tn(obt_ref[...], wb_ref[...])
    merged = ga_ref[...].astype(F32) * ya + gb_ref[...].astype(F32) * yb
    x1 = x_ref[...] + _dot(merged.astype(BF16), wo_ref[...])
    x1_ref[...] = x1
    h2_ref[...] = _rms(x1, g2_ref[...]).astype(BF16)


def _merge_out(x, oat, obt, gates, wa, wb, wo, g2, tm=256):
    s, d = x.shape
    const = lambda i: (0, 0)
    return pl.pallas_call(
        _merge_out_kernel,
        grid=(s // tm,),
        in_specs=[
            pl.BlockSpec((tm, d), lambda i: (i, 0)),
            pl.BlockSpec((oat.shape[0], tm), lambda i: (0, i)),
            pl.BlockSpec((obt.shape[0], tm), lambda i: (0, i)),
            pl.BlockSpec((tm, d), lambda i: (i, 0)),
            pl.BlockSpec((tm, d), lambda i: (i, 1)),
            pl.BlockSpec(wa.shape, const),
            pl.BlockSpec(wb.shape, const),
            pl.BlockSpec(wo.shape, const),
            pl.BlockSpec((1, d), const),
        ],
        out_specs=[
            pl.BlockSpec((tm, d), lambda i: (i, 0)),
            pl.BlockSpec((tm, d), lambda i: (i, 0)),
        ],
        out_shape=[
            jax.ShapeDtypeStruct((s, d), F32),
            jax.ShapeDtypeStruct((s, d), BF16),
        ],
        compiler_params=pltpu.CompilerParams(
            dimension_semantics=("arbitrary",), vmem_limit_bytes=VMEM_LIMIT),
        name="merge_out",
    )(x, oat, obt, gates, gates, wa, wb, wo, g2)


def _mlp_kernel(h2_ref, x1_ref, w1_ref, w2_ref, gf_ref, o_ref, *, nf, final_norm):
    f = pl.program_id(1)

    @pl.when(f == 0)
    def _():
        o_ref[...] = x1_ref[...]

    u = jnp.maximum(_dot(h2_ref[...], w1_ref[...]), 0.0)
    uu = (u * u).astype(BF16)
    for n in range(0, o_ref.shape[1], MLP_OUT_PIECE):
        o_ref[:, n:n + MLP_OUT_PIECE] += _dot(uu, w2_ref[:, n:n + MLP_OUT_PIECE])

    if final_norm:
        @pl.when(f == nf - 1)
        def _():
            o_ref[...] = _rms(o_ref[...], gf_ref[...])


def _mlp(h2, x1, w1, w2, gf, final_norm, tm=512, tf=1024):
    s, d = x1.shape
    nf = w1.shape[1] // tf
    return pl.pallas_call(
        functools.partial(_mlp_kernel, nf=nf, final_norm=final_norm),
        grid=(s // tm, nf),
        in_specs=[
            pl.BlockSpec((tm, d), lambda i, f: (i, 0)),
            pl.BlockSpec((tm, d), lambda i, f: (i, 0)),
            pl.BlockSpec((d, tf), lambda i, f: (0, f)),
            pl.BlockSpec((tf, d), lambda i, f: (f, 0)),
            pl.BlockSpec((1, d), lambda i, f: (0, 0)),
        ],
        out_specs=pl.BlockSpec((tm, d), lambda i, f: (i, 0)),
        out_shape=jax.ShapeDtypeStruct((s, d), F32),
        compiler_params=pltpu.CompilerParams(
            dimension_semantics=("arbitrary", "arbitrary"), vmem_limit_bytes=VMEM_LIMIT),
        name="mlp",
    )(h2, x1, w1, w2, gf)


def _arrange_w_in(w_in):
    d = w_in.shape[0]
    wt = w_in.T
    n_a = N_ATT + 512
    n_kv = 256 + B_ROPE
    wa = wt[:n_a].astype(BF16)
    kr = wt[n_a + 256:n_a + n_kv]
    kr_rot = jnp.concatenate([-kr[B_HALF:], kr[:B_HALF]], axis=0)
    pad = jnp.zeros((IN_TN - n_kv - B_ROPE, d), w_in.dtype)
    wb = jnp.concatenate([wt[n_a:n_a + n_kv], kr_rot, pad], axis=0).astype(BF16)
    wg = wt[n_a + n_kv:].astype(BF16)
    n_q = A_HEADS * A_HEAD_DIM
    cs = jnp.concatenate([jnp.full((1, n_q), A_HEAD_DIM ** -0.5 * LOG2E, F32),
                          jnp.ones((1, N_ATT - n_q), F32)], axis=1)
    return wa, wb, wg, cs


def _arrange_mla_weights(w_uq, w_uk, w_uv):
    c_q = w_uq.shape[0]
    c_kv = w_uk.shape[0]
    qscale = (B_NOPE + B_ROPE) ** -0.5 * math.log2(math.e)
    wq = w_uq * qscale
    wq_pad = jnp.concatenate(
        [wq, jnp.zeros((c_q, B_HEADS, HEAD_PAD - B_NOPE - B_ROPE), wq.dtype)], axis=-1)
    wq_t = wq_pad.reshape(c_q, B_HEADS * HEAD_PAD).T
    rope = wq[:, :, B_NOPE:]
    rot = jnp.concatenate([-rope[..., B_HALF:], rope[..., :B_HALF]], axis=-1)
    wqr_t = rot.reshape(c_q, B_HEADS * B_ROPE).T
    wk_ext = jnp.concatenate(
        [w_uk, jnp.zeros((c_kv, B_HEADS, HEAD_PAD - B_NOPE), w_uk.dtype)], axis=-1
    ).reshape(c_kv, B_HEADS * HEAD_PAD)
    sel = np.zeros((HEAD_PAD, B_HEADS, HEAD_PAD), np.float32)
    for r in range(B_ROPE):
        sel[r, :, B_NOPE + r] = 1.0
    wk_rope = jnp.asarray(sel.reshape(HEAD_PAD, B_HEADS * HEAD_PAD))
    wv_t = w_uv.reshape(c_kv, B_HEADS * B_V).T
    return (wq_t.astype(BF16), wqr_t.astype(BF16), wk_ext.astype(BF16),
            wk_rope.astype(BF16), wv_t.astype(BF16))


def _rope_inv_freq():
    inv = ROPE_THETA ** (-jnp.arange(B_HALF, dtype=F32) / B_HALF)
    return jnp.concatenate([inv, inv])[:, None]


MLA_TQ = 512
MLA_TK = 512


def kernel(x, positions, attn_norm_g, w_in, a_sink, b_q_norm_g, b_kv_norm_g, b_w_uq, b_w_uk, b_w_uv,
           w_branch_a, w_branch_b, w_out, mlp_norm_g, w_mlp_in, w_mlp_out, final_norm_g):
    b, s, d = x.shape
    depth = w_in.shape[0]
    inv_col = _rope_inv_freq()
    outs = []
    for bi in range(b):
        xb = x[bi]
        pos_col = positions[bi][:, None]
        pos_row = positions[bi][None, :]
        for l in range(depth):
            wa, wb, wg, cs = _arrange_w_in(w_in[l])
            wq_t, wqr_t, wk_ext, wk_rope, wv_t = _arrange_mla_weights(b_w_uq[l], b_w_uk[l], b_w_uv[l])
            att, lat, gates = _in_proj(xb, attn_norm_g[l][None, :], wa, wb, wg, cs)
            qt, kfull, vt = _mla_prep(lat, pos_row, b_q_norm_g[l][None, :], b_kv_norm_g[l][None, :],
                                      wq_t, wqr_t, wk_ext, wk_rope, wv_t, inv_col, MLA_TK)
            obt = _mla_attn(qt, kfull, vt, MLA_TQ, MLA_TK)
            oat = _win_attn(att, pos_col, pos_row, a_sink[l])
            x1, h2 = _merge_out(xb, oat, obt, gates, w_branch_a[l].astype(BF16), w_branch_b[l].astype(BF16),
                                w_out[l].astype(BF16), mlp_norm_g[l][None, :])
            xb = _mlp(h2, x1, w_mlp_in[l].astype(BF16), w_mlp_out[l].astype(BF16),
                      final_norm_g[None, :], final_norm=(l == depth - 1))
        outs.append(xb)
    return outs[0][None] if b == 1 else jnp.stack(outs, axis=0)
```

```python
import functools
import math

import jax
import jax.numpy as jnp
import numpy as np
from jax import lax
from jax.experimental import pallas as pl
from jax.experimental.pallas import tpu as pltpu

F32 = jnp.float32
BF16 = jnp.bfloat16

EPS = 1e-6
NEG_INF = -1e30
LOG2E = math.log2(math.e)
MASKED_DIST = 1e33
ROPE_THETA = 10000.0

A_HEADS = 16
A_KV_HEADS = 4
A_GROUP = A_HEADS // A_KV_HEADS
A_HEAD_DIM = 64
WINDOW = 128
BAND = 128

B_HEADS = 16
B_NOPE = 64
B_ROPE = 32
B_V = 64
V_ROWS = B_V + 16
B_HALF = B_ROPE // 2
HEAD_PAD = 128

VMEM_LIMIT = 52 * 1024 * 1024

N_ATT = 1536
N_LAT = 1024
N_GATE = 4096
IN_TN = 512
J_ATT = N_ATT // IN_TN
J_LAT = N_LAT // IN_TN
GATE_TN = 1024
J_GATE = N_GATE // GATE_TN
NORM_ROWS = 256
MLP_OUT_PIECE = 512


def _rms(x, g):
    ms = jnp.mean(x * x, axis=-1, keepdims=True)
    return x * lax.rsqrt(ms + EPS) * g


def _dot(a, b):
    return jnp.dot(a, b, preferred_element_type=F32)


def _dot_nt(a, b):
    return lax.dot_general(a, b, (((1,), (1,)), ((), ())), preferred_element_type=F32)


def _dot_tn(a, b):
    return lax.dot_general(a, b, (((0,), (0,)), ((), ())), preferred_element_type=F32)


def _in_proj_kernel(x_ref, g_ref, wa_ref, wb_ref, wg_ref, cs_ref, att_ref, lat_ref, gate_ref, h_ref):
    j = pl.program_id(1)

    @pl.when(j == 0)
    def _():
        def slab(c, carry):
            rows = pl.ds(pl.multiple_of(c * NORM_ROWS, NORM_ROWS), NORM_ROWS)
            h_ref[rows, :] = _rms(x_ref[rows, :], g_ref[...]).astype(BF16)
            return carry
        lax.fori_loop(0, x_ref.shape[0] // NORM_ROWS, slab, 0)

    @pl.when(j < J_ATT)
    def _():
        att_ref[...] = (_dot_nt(h_ref[...], wa_ref[...]) * cs_ref[...]).astype(BF16)

    @pl.when(j == J_ATT)
    def _():
        lat_ref[...] = _dot_nt(h_ref[...], wa_ref[...])

    @pl.when(j == J_ATT + 1)
    def _():
        lat_ref[...] = _dot_nt(h_ref[...], wb_ref[...])

    @pl.when(j >= J_ATT + J_LAT)
    def _():
        gate_ref[...] = jax.nn.sigmoid(_dot_nt(h_ref[...], wg_ref[...])).astype(BF16)


def _in_proj(x, g, wa, wb, wg, cs, tm=1024):
    s, d = x.shape
    nj = J_ATT + J_LAT + J_GATE
    return pl.pallas_call(
        _in_proj_kernel,
        grid=(s // tm, nj),
        in_specs=[
            pl.BlockSpec((tm, d), lambda i, j: (i, 0)),
            pl.BlockSpec((1, d), lambda i, j: (0, 0)),
            pl.BlockSpec((IN_TN, d), lambda i, j: (jnp.minimum(j, J_ATT), 0)),
            pl.BlockSpec((IN_TN, d), lambda i, j: (0, 0)),
            pl.BlockSpec((GATE_TN, d), lambda i, j: (jnp.clip(j - J_ATT - J_LAT, 0, J_GATE - 1), 0)),
            pl.BlockSpec((1, IN_TN), lambda i, j: (0, jnp.minimum(j, J_ATT - 1))),
        ],
        out_specs=[
            pl.BlockSpec((tm, IN_TN), lambda i, j: (i, jnp.minimum(j, J_ATT - 1))),
            pl.BlockSpec((tm, IN_TN), lambda i, j: (i, jnp.clip(j - J_ATT, 0, J_LAT - 1))),
            pl.BlockSpec((tm, GATE_TN), lambda i, j: (i, jnp.clip(j - J_ATT - J_LAT, 0, J_GATE - 1))),
        ],
        out_shape=[
            jax.ShapeDtypeStruct((s, N_ATT), BF16),
            jax.ShapeDtypeStruct((s, N_LAT), F32),
            jax.ShapeDtypeStruct((s, N_GATE), BF16),
        ],
        scratch_shapes=[pltpu.VMEM((tm, d), BF16)],
        compiler_params=pltpu.CompilerParams(
            dimension_semantics=("arbitrary", "arbitrary"), vmem_limit_bytes=VMEM_LIMIT),
        name="in_proj",
    )(x, g, wa, wb, wg, cs)


def _mla_prep_kernel(cq_ref, ckv_ref, kr_ref, posr_ref, gq_ref, gkv_ref,
                     wq_ref, wqr_ref, wk_ref, wkr_ref, wv_ref, invc_ref,
                     qt_ref, k_ref, vt_ref):
    cqn = _rms(cq_ref[...], gq_ref[...]).astype(BF16)
    ckvn = _rms(ckv_ref[...], gkv_ref[...]).astype(BF16)
    tm = cqn.shape[0]

    qt = _dot_nt(wq_ref[...], cqn)
    qrt = _dot_nt(wqr_ref[...], cqn)
    ang_t = invc_ref[...] * posr_ref[...].astype(F32)
    cos_t = jnp.cos(ang_t)
    sin_t = jnp.sin(ang_t)
    zeros = jnp.zeros((HEAD_PAD - B_NOPE - B_ROPE, tm), BF16)
    for h in range(B_HEADS):
        r0 = h * HEAD_PAD
        qt_ref[r0:r0 + B_NOPE, :] = qt[r0:r0 + B_NOPE].astype(BF16)
        rope = (qt[r0 + B_NOPE:r0 + B_NOPE + B_ROPE] * cos_t
                + qrt[h * B_ROPE:(h + 1) * B_ROPE] * sin_t)
        qt_ref[r0 + B_NOPE:r0 + B_NOPE + B_ROPE, :] = rope.astype(BF16)
        qt_ref[r0 + B_NOPE + B_ROPE:r0 + HEAD_PAD, :] = zeros

    cs = jnp.concatenate([cos_t, sin_t, jnp.zeros((HEAD_PAD - 2 * B_ROPE, tm), F32)], axis=0).T
    u = kr_ref[...] * cs
    k_rope = u + pltpu.roll(u, HEAD_PAD - B_ROPE, 1)
    kfull = _dot(ckvn, wk_ref[...]) + _dot(k_rope.astype(BF16), wkr_ref[...])
    col = lax.broadcasted_iota(jnp.int32, kfull.shape, 1)
    kfull = jnp.where((col & (HEAD_PAD - 1)) == SHIFT_ROW, 1.0, kfull).astype(BF16)
    for h in range(B_HEADS):
        k_ref[h] = kfull[:, h * HEAD_PAD:(h + 1) * HEAD_PAD]

    vt = _dot_nt(wv_ref[...], ckvn)
    pad_row = lax.broadcasted_iota(jnp.int32, (V_ROWS - B_V, tm), 0)
    ones_block = jnp.where(pad_row == 0, 1.0, 0.0).astype(BF16)
    for h in range(B_HEADS):
        vt_ref[0, h * V_ROWS:h * V_ROWS + B_V, :] = vt[h * B_V:(h + 1) * B_V].astype(BF16)
        vt_ref[0, h * V_ROWS + B_V:(h + 1) * V_ROWS, :] = ones_block


def _mla_prep(lat, pos_row, gq, gkv, wq_t, wqr_t, wk_ext, wk_rope, wv_t, inv_col, tm):
    s = lat.shape[0]
    nq = B_HEADS * HEAD_PAD
    nv = B_HEADS * V_ROWS
    const = lambda i: (0, 0)
    return pl.pallas_call(
        _mla_prep_kernel,
        grid=(s // tm,),
        in_specs=[
            pl.BlockSpec((tm, 512), lambda i: (i, 0)),
            pl.BlockSpec((tm, 256), lambda i: (i, 2)),
            pl.BlockSpec((tm, 128), lambda i: (i, 6)),
            pl.BlockSpec((1, tm), lambda i: (0, i)),
            pl.BlockSpec((1, 512), const),
            pl.BlockSpec((1, 256), const),
            pl.BlockSpec(wq_t.shape, const),
            pl.BlockSpec(wqr_t.shape, const),
            pl.BlockSpec(wk_ext.shape, const),
            pl.BlockSpec(wk_rope.shape, const),
            pl.BlockSpec(wv_t.shape, const),
            pl.BlockSpec(inv_col.shape, const),
        ],
        out_specs=[
            pl.BlockSpec((nq, tm), lambda i: (0, i)),
            pl.BlockSpec((B_HEADS, tm, HEAD_PAD), lambda i: (0, i, 0)),
            pl.BlockSpec((1, nv, tm), lambda i: (i, 0, 0)),
        ],
        out_shape=[
            jax.ShapeDtypeStruct((nq, s), BF16),
            jax.ShapeDtypeStruct((B_HEADS, s, HEAD_PAD), BF16),
            jax.ShapeDtypeStruct((s // tm, nv, tm), BF16),
        ],
        compiler_params=pltpu.CompilerParams(
            dimension_semantics=("arbitrary",), vmem_limit_bytes=VMEM_LIMIT),
        name="mla_prep",
    )(lat, lat, lat, pos_row, gq, gkv, wq_t, wqr_t, wk_ext, wk_rope, wv_t, inv_col)


SHIFT_KEYS = 256
SHIFT_ROW = B_NOPE + B_ROPE
L_MIN = 0.5
L_MAX = 2.0 ** 100


SCORE_SLOTS = 2


def _sweep(nkb, scores, update, carry, group):
    def blocks(kb0, carry, last):
        for j in range(group):
            if not (last and j == group - 1):
                scores(kb0 + j + 1, (j + 1) % SCORE_SLOTS)
            carry = update(kb0 + j, j % SCORE_SLOTS, carry)
        return carry

    scores(0, 0)
    if nkb > group:
        carry = lax.fori_loop(0, nkb // group - 1, lambda i, c: blocks(group * i, c, False), carry)
    return blocks(nkb - group, carry, True)


def _mla_attn_kernel(q_ref, qn_ref, k_ref, v_ref, o_ref, s_ref, qs_ref, shift_ref, *, tk, nkb, group):
    tq = q_ref.shape[1]

    def key_block(kb):
        if isinstance(kb, int):
            return k_ref[kb * tk:(kb + 1) * tk, :]
        return k_ref[pl.ds(pl.multiple_of(kb * tk, tk), tk), :]

    def tile_shift(qt_ref):
        s0 = _dot(k_ref[0:SHIFT_KEYS, :], qt_ref[...])
        return jnp.max(s0, axis=0, keepdims=True)

    @pl.when(pl.program_id(1) == 0)
    def _():
        shift_ref[...] = tile_shift(q_ref)

    row = lax.broadcasted_iota(jnp.int32, (16, tq), 0)
    qs_ref[...] = q_ref[...]
    qs_ref[SHIFT_ROW:SHIFT_ROW + 16, :] = jnp.where(row == 0, -shift_ref[...], 0.0).astype(BF16)
    shift_ref[...] = tile_shift(qn_ref)

    def scores_fixed(kb, slot):
        s_ref[slot] = _dot(key_block(kb).astype(F32), qs_ref[...].astype(F32))

    def update_fixed(kb, slot, acc):
        p = jnp.exp2(s_ref[slot]).astype(BF16).astype(F32)
        return acc + _dot(v_ref[kb].astype(F32), p)

    acc = _sweep(nkb, scores_fixed, update_fixed, jnp.zeros((V_ROWS, tq), F32), group)
    l = acc[B_V:B_V + 1]
    l_ok = (l >= L_MIN) & (l < L_MAX)
    acc_ok = jnp.abs(acc) < L_MAX
    n_bad = jnp.sum(jnp.where(l_ok, 0.0, 1.0)) + jnp.sum(jnp.where(acc_ok, 0.0, 1.0))
    o_ref[...] = (acc[:B_V] / l).astype(BF16)

    @pl.when(n_bad != 0.0)
    def _():
        def scores_raw(kb, slot):
            s_ref[slot] = _dot(key_block(kb), q_ref[...])

        def update_online(kb, slot, carry):
            m, acc = carry
            m_new = jnp.maximum(m, jnp.max(s_ref[slot], axis=0, keepdims=True))
            p = jnp.exp2(s_ref[slot] - m_new).astype(BF16)
            acc = acc * jnp.exp2(m - m_new) + _dot(v_ref[kb], p)
            return m_new, acc

        _, acc = _sweep(nkb, scores_raw, update_online,
                        (jnp.full((1, tq), NEG_INF, F32), jnp.zeros((V_ROWS, tq), F32)), SCORE_SLOTS)
        o_ref[...] = (acc[:B_V] / acc[B_V:B_V + 1]).astype(BF16)


def _mla_attn(qt, kfull, vt, tq, tk, group=32):
    s = kfull.shape[1]
    nkb = s // tk
    nq = s // tq
    group = min(group, nkb)
    return pl.pallas_call(
        functools.partial(_mla_attn_kernel, tk=tk, nkb=nkb, group=group),
        grid=(B_HEADS, nq),
        in_specs=[
            pl.BlockSpec((HEAD_PAD, tq), lambda h, i: (h, i)),
            pl.BlockSpec((HEAD_PAD, tq), lambda h, i: (h, jnp.minimum(i + 1, nq - 1))),
            pl.BlockSpec((None, s, HEAD_PAD), lambda h, i: (h, 0, 0)),
            pl.BlockSpec((nkb, V_ROWS, tk), lambda h, i: (0, h, 0)),
        ],
        out_specs=pl.BlockSpec((B_V, tq), lambda h, i: (h, i)),
        out_shape=jax.ShapeDtypeStruct((B_HEADS * B_V, s), BF16),
        scratch_shapes=[pltpu.VMEM((SCORE_SLOTS, tk, tq), F32), pltpu.VMEM((HEAD_PAD, tq), BF16),
                        pltpu.VMEM((1, tq), F32)],
        compiler_params=pltpu.CompilerParams(
            dimension_semantics=("arbitrary", "arbitrary"), vmem_limit_bytes=VMEM_LIMIT),
        name="mla_attn",
    )(qt, qt, kfull, vt)


def _win_attn_kernel(sink_ref, q_ref, kvp_ref, kvc_ref, kvn_ref, pq_ref, pkp_ref, pkc_ref, pkn_ref,
                     o_ref, *, seq, slopes):
    i = pl.program_id(0)
    hd = A_HEAD_DIM
    qt = q_ref[...].astype(F32).T.astype(BF16)
    kv = jnp.concatenate([kvp_ref[...], kvc_ref[...], kvn_ref[...]], axis=0)
    kw = A_KV_HEADS * hd
    vt = kv[:, kw:].astype(F32).T.astype(BF16)
    pk = jnp.concatenate([pkp_ref[...], pkc_ref[...], pkn_ref[...]], axis=0)
    dist = jnp.abs(pk - pq_ref[...]).astype(F32)
    c = lax.broadcasted_iota(jnp.int32, dist.shape, 0)
    r = lax.broadcasted_iota(jnp.int32, dist.shape, 1)
    kglob = (i - 1) * BAND + c
    mask = (jnp.abs(BAND + r - c) <= WINDOW) & (kglob >= 0) & (kglob < seq)
    dist = jnp.where(mask, dist, MASKED_DIST)
    for g in range(A_KV_HEADS):
        heads = range(g * A_GROUP, (g + 1) * A_GROUP)
        qg = jnp.concatenate([qt[h * hd:(h + 1) * hd] for h in heads], axis=1)
        st = _dot(kv[:, g * hd:(g + 1) * hd], qg)
        st = st - jnp.concatenate([slopes[h] * dist for h in heads], axis=1)
        sk = jnp.concatenate([jnp.full((1, BAND), sink_ref[h] * LOG2E, F32) for h in heads], axis=1)
        m = jnp.maximum(jnp.max(st, axis=0, keepdims=True), sk)
        p = jnp.exp2(st - m)
        denom = jnp.sum(p, axis=0, keepdims=True) + jnp.exp2(sk - m)
        og = _dot(vt[g * hd:(g + 1) * hd], p.astype(BF16)) / denom
        for j, h in enumerate(heads):
            o_ref[h * hd:(h + 1) * hd, :] = og[:, j * BAND:(j + 1) * BAND].astype(BF16)


def _win_attn(att, pos_col, pos_row, sink):
    s = att.shape[0]
    nb = s // BAND
    slopes = tuple(float(2.0 ** (-8.0 * (h + 1) / A_HEADS)) * LOG2E for h in range(A_HEADS))
    kvcol = (A_HEADS * A_HEAD_DIM) // (2 * A_KV_HEADS * A_HEAD_DIM)
    prev = lambda i, sk: jnp.maximum(i - 1, 0)
    nxt = lambda i, sk: jnp.minimum(i + 1, nb - 1)
    grid_spec = pltpu.PrefetchScalarGridSpec(
        num_scalar_prefetch=1,
        grid=(nb,),
        in_specs=[
            pl.BlockSpec((BAND, A_HEADS * A_HEAD_DIM), lambda i, sk: (i, 0)),
            pl.BlockSpec((BAND, 512), lambda i, sk: (prev(i, sk), kvcol)),
            pl.BlockSpec((BAND, 512), lambda i, sk: (i, kvcol)),
            pl.BlockSpec((BAND, 512), lambda i, sk: (nxt(i, sk), kvcol)),
            pl.BlockSpec((1, BAND), lambda i, sk: (0, i)),
            pl.BlockSpec((BAND, 1), lambda i, sk: (prev(i, sk), 0)),
            pl.BlockSpec((BAND, 1), lambda i, sk: (i, 0)),
            pl.BlockSpec((BAND, 1), lambda i, sk: (nxt(i, sk), 0)),
        ],
        out_specs=pl.BlockSpec((A_HEADS * A_HEAD_DIM, BAND), lambda i, sk: (0, i)),
    )
    return pl.pallas_call(
        functools.partial(_win_attn_kernel, seq=s, slopes=slopes),
        grid_spec=grid_spec,
        out_shape=jax.ShapeDtypeStruct((A_HEADS * A_HEAD_DIM, s), BF16),
        compiler_params=pltpu.CompilerParams(
            dimension_semantics=("arbitrary",), vmem_limit_bytes=VMEM_LIMIT),
        name="win_attn",
    )(sink, att, att, att, att, pos_row, pos_col, pos_col, pos_col)


def _merge_out_kernel(x_ref, oat_ref, obt_ref, ga_ref, gb_ref, wa_ref, wb_ref, wo_ref, g2_ref,
                      x1_ref, h2_ref):
    ya = _dot_tn(oat_ref[...], wa_ref[...])
    yb = _dot_tn(obt_ref[...], wb_ref[...])
    merged = ga_ref[...].astype(F32) * ya + gb_ref[...].astype(F32) * yb
    x1 = x_ref[...] + _dot(merged.astype(BF16), wo_ref[...])
    x1_ref[...] = x1
    h2_ref[...] = _rms(x1, g2_ref[...]).astype(BF16)


def _merge_out(x, oat, obt, gates, wa, wb, wo, g2, tm=256):
    s, d = x.shape
    const = lambda i: (0, 0)
    return pl.pallas_call(
        _merge_out_kernel,
        grid=(s // tm,),
        in_specs=[
            pl.BlockSpec((tm, d), lambda i: (i, 0)),
            pl.BlockSpec((oat.shape[0], tm), lambda i: (0, i)),
            pl.BlockSpec((obt.shape[0], tm), lambda i: (0, i)),
            pl.BlockSpec((tm, d), lambda i: (i, 0)),
            pl.BlockSpec((tm, d), lambda i: (i, 1)),
            pl.BlockSpec(wa.shape, const),
            pl.BlockSpec(wb.shape, const),
            pl.BlockSpec(wo.shape, const),
            pl.BlockSpec((1, d), const),
        ],
        out_specs=[
            pl.BlockSpec((tm, d), lambda i: (i, 0)),
            pl.BlockSpec((tm, d), lambda i: (i, 0)),
        ],
        out_shape=[
            jax.ShapeDtypeStruct((s, d), F32),
            jax.ShapeDtypeStruct((s, d), BF16),
        ],
        compiler_params=pltpu.CompilerParams(
            dimension_semantics=("arbitrary",), vmem_limit_bytes=VMEM_LIMIT),
        name="merge_out",
    )(x, oat, obt, gates, gates, wa, wb, wo, g2)


def _mlp_kernel(h2_ref, x1_ref, w1_ref, w2_ref, gf_ref, o_ref, *, nf, final_norm):
    f = pl.program_id(1)

    @pl.when(f == 0)
    def _():
        o_ref[...] = x1_ref[...]

    u = jnp.maximum(_dot(h2_ref[...], w1_ref[...]), 0.0)
    uu = (u * u).astype(BF16)
    for n in range(0, o_ref.shape[1], MLP_OUT_PIECE):
        o_ref[:, n:n + MLP_OUT_PIECE] += _dot(uu, w2_ref[:, n:n + MLP_OUT_PIECE])

    if final_norm:
        @pl.when(f == nf - 1)
        def _():
            o_ref[...] = _rms(o_ref[...], gf_ref[...])


def _mlp(h2, x1, w1, w2, gf, final_norm, tm=512, tf=1024):
    s, d = x1.shape
    nf = w1.shape[1] // tf
    return pl.pallas_call(
        functools.partial(_mlp_kernel, nf=nf, final_norm=final_norm),
        grid=(s // tm, nf),
        in_specs=[
            pl.BlockSpec((tm, d), lambda i, f: (i, 0)),
            pl.BlockSpec((tm, d), lambda i, f: (i, 0)),
            pl.BlockSpec((d, tf), lambda i, f: (0, f)),
            pl.BlockSpec((tf, d), lambda i, f: (f, 0)),
            pl.BlockSpec((1, d), lambda i, f: (0, 0)),
        ],
        out_specs=pl.BlockSpec((tm, d), lambda i, f: (i, 0)),
        out_shape=jax.ShapeDtypeStruct((s, d), F32),
        compiler_params=pltpu.CompilerParams(
            dimension_semantics=("arbitrary", "arbitrary"), vmem_limit_bytes=VMEM_LIMIT),
        name="mlp",
    )(h2, x1, w1, w2, gf)


def _arrange_w_in(w_in):
    d = w_in.shape[0]
    wt = w_in.T
    n_a = N_ATT + 512
    n_kv = 256 + B_ROPE
    wa = wt[:n_a].astype(BF16)
    kr = wt[n_a + 256:n_a + n_kv]
    kr_rot = jnp.concatenate([-kr[B_HALF:], kr[:B_HALF]], axis=0)
    pad = jnp.zeros((IN_TN - n_kv - B_ROPE, d), w_in.dtype)
    wb = jnp.concatenate([wt[n_a:n_a + n_kv], kr_rot, pad], axis=0).astype(BF16)
    wg = wt[n_a + n_kv:].astype(BF16)
    n_q = A_HEADS * A_HEAD_DIM
    cs = jnp.concatenate([jnp.full((1, n_q), A_HEAD_DIM ** -0.5 * LOG2E, F32),
                          jnp.ones((1, N_ATT - n_q), F32)], axis=1)
    return wa, wb, wg, cs


def _arrange_mla_weights(w_uq, w_uk, w_uv):
    c_q = w_uq.shape[0]
    c_kv = w_uk.shape[0]
    qscale = (B_NOPE + B_ROPE) ** -0.5 * math.log2(math.e)
    wq = w_uq * qscale
    wq_pad = jnp.concatenate(
        [wq, jnp.zeros((c_q, B_HEADS, HEAD_PAD - B_NOPE - B_ROPE), wq.dtype)], axis=-1)
    wq_t = wq_pad.reshape(c_q, B_HEADS * HEAD_PAD).T
    rope = wq[:, :, B_NOPE:]
    rot = jnp.concatenate([-rope[..., B_HALF:], rope[..., :B_HALF]], axis=-1)
    wqr_t = rot.reshape(c_q, B_HEADS * B_ROPE).T
    wk_ext = jnp.concatenate(
        [w_uk, jnp.zeros((c_kv, B_HEADS, HEAD_PAD - B_NOPE), w_uk.dtype)], axis=-1
    ).reshape(c_kv, B_HEADS * HEAD_PAD)
    sel = np.zeros((HEAD_PAD, B_HEADS, HEAD_PAD), np.float32)
    for r in range(B_ROPE):
        sel[r, :, B_NOPE + r] = 1.0
    wk_rope = jnp.asarray(sel.reshape(HEAD_PAD, B_HEADS * HEAD_PAD))
    wv_t = w_uv.reshape(c_kv, B_HEADS * B_V).T
    return (wq_t.astype(BF16), wqr_t.astype(BF16), wk_ext.astype(BF16),
            wk_rope.astype(BF16), wv_t.astype(BF16))


def _rope_inv_freq():
    inv = ROPE_THETA ** (-jnp.arange(B_HALF, dtype=F32) / B_HALF)
    return jnp.concatenate([inv, inv])[:, None]


MLA_TQ = 512
MLA_TK = 512


def kernel(x, positions, attn_norm_g, w_in, a_sink, b_q_norm_g, b_kv_norm_g, b_w_uq, b_w_uk, b_w_uv,
           w_branch_a, w_branch_b, w_out, mlp_norm_g, w_mlp_in, w_mlp_out, final_norm_g):
    b, s, d = x.shape
    depth = w_in.shape[0]
    inv_col = _rope_inv_freq()
    outs = []
    for bi in range(b):
        xb = x[bi]
        pos_col = positions[bi][:, None]
        pos_row = positions[bi][None, :]
        for l in range(depth):
            wa, wb, wg, cs = _arrange_w_in(w_in[l])
            wq_t, wqr_t, wk_ext, wk_rope, wv_t = _arrange_mla_weights(b_w_uq[l], b_w_uk[l], b_w_uv[l])
            att, lat, gates = _in_proj(xb, attn_norm_g[l][None, :], wa, wb, wg, cs)
            qt, kfull, vt = _mla_prep(lat, pos_row, b_q_norm_g[l][None, :], b_kv_norm_g[l][None, :],
                                      wq_t, wqr_t, wk_ext, wk_rope, wv_t, inv_col, MLA_TK)
            obt = _mla_attn(qt, kfull, vt, MLA_TQ, MLA_TK)
            oat = _win_attn(att, pos_col, pos_row, a_sink[l])
            x1, h2 = _merge_out(xb, oat, obt, gates, w_branch_a[l].astype(BF16), w_branch_b[l].astype(BF16),
                                w_out[l].astype(BF16), mlp_norm_g[l][None, :])
            xb = _mlp(h2, x1, w_mlp_in[l].astype(BF16), w_mlp_out[l].astype(BF16),
                      final_norm_g[None, :], final_norm=(l == depth - 1))
        outs.append(xb)
    return outs[0][None] if b == 1 else jnp.stack(outs, axis=0)
```

```python
import functools
import math

import jax
import jax.numpy as jnp
import numpy as np
from jax import lax
from jax.experimental import pallas as pl
from jax.experimental.pallas import tpu as pltpu

F32 = jnp.float32
BF16 = jnp.bfloat16

EPS = 1e-6
NEG_INF = -1e30
LOG2E = math.log2(math.e)
MASKED_DIST = 1e33
ROPE_THETA = 10000.0

A_HEADS = 16
A_KV_HEADS = 4
A_GROUP = A_HEADS // A_KV_HEADS
A_HEAD_DIM = 64
WINDOW = 128
BAND = 128

B_HEADS = 16
B_NOPE = 64
B_ROPE = 32
B_V = 64
V_ROWS = B_V + 16
B_HALF = B_ROPE // 2
HEAD_PAD = 128

VMEM_LIMIT = 52 * 1024 * 1024
MERGE_VMEM_LIMIT = 56 * 1024 * 1024

N_ATT = 1536
N_LAT = 1024
N_GATE = 4096
IN_TN = 512
J_ATT = N_ATT // IN_TN
J_LAT = N_LAT // IN_TN
GATE_TN = 1024
J_GATE = N_GATE // GATE_TN
NORM_ROWS = 256
MLP_OUT_PIECE = 512


def _rms(x, g):
    ms = jnp.mean(x * x, axis=-1, keepdims=True)
    return x * lax.rsqrt(ms + EPS) * g


def _dot(a, b):
    return jnp.dot(a, b, preferred_element_type=F32)


def _dot_nt(a, b):
    return lax.dot_general(a, b, (((1,), (1,)), ((), ())), preferred_element_type=F32)


def _dot_tn(a, b):
    return lax.dot_general(a, b, (((0,), (0,)), ((), ())), preferred_element_type=F32)


def _in_proj_kernel(x_ref, g_ref, wa_ref, wb_ref, wg_ref, cs_ref, att_ref, lat_ref, gate_ref, h_ref):
    j = pl.program_id(1)

    @pl.when(j == 0)
    def _():
        def slab(c, carry):
            rows = pl.ds(pl.multiple_of(c * NORM_ROWS, NORM_ROWS), NORM_ROWS)
            h_ref[rows, :] = _rms(x_ref[rows, :], g_ref[...]).astype(BF16)
            return carry
        lax.fori_loop(0, x_ref.shape[0] // NORM_ROWS, slab, 0)

    @pl.when(j < J_ATT)
    def _():
        att_ref[...] = (_dot_nt(h_ref[...], wa_ref[...]) * cs_ref[...]).astype(BF16)

    @pl.when(j == J_ATT)
    def _():
        lat_ref[...] = _dot_nt(h_ref[...], wa_ref[...])

    @pl.when(j == J_ATT + 1)
    def _():
        lat_ref[...] = _dot_nt(h_ref[...], wb_ref[...])

    @pl.when(j >= J_ATT + J_LAT)
    def _():
        gate_ref[...] = jax.nn.sigmoid(_dot_nt(h_ref[...], wg_ref[...])).astype(BF16)


def _in_proj(x, g, wa, wb, wg, cs, tm=1024):
    s, d = x.shape
    nj = J_ATT + J_LAT + J_GATE
    return pl.pallas_call(
        _in_proj_kernel,
        grid=(s // tm, nj),
        in_specs=[
            pl.BlockSpec((tm, d), lambda i, j: (i, 0)),
            pl.BlockSpec((1, d), lambda i, j: (0, 0)),
            pl.BlockSpec((IN_TN, d), lambda i, j: (jnp.minimum(j, J_ATT), 0)),
            pl.BlockSpec((IN_TN, d), lambda i, j: (0, 0)),
            pl.BlockSpec((GATE_TN, d), lambda i, j: (jnp.clip(j - J_ATT - J_LAT, 0, J_GATE - 1), 0)),
            pl.BlockSpec((1, IN_TN), lambda i, j: (0, jnp.minimum(j, J_ATT - 1))),
        ],
        out_specs=[
            pl.BlockSpec((tm, IN_TN), lambda i, j: (i, jnp.minimum(j, J_ATT - 1))),
            pl.BlockSpec((tm, IN_TN), lambda i, j: (i, jnp.clip(j - J_ATT, 0, J_LAT - 1))),
            pl.BlockSpec((tm, GATE_TN), lambda i, j: (i, jnp.clip(j - J_ATT - J_LAT, 0, J_GATE - 1))),
        ],
        out_shape=[
            jax.ShapeDtypeStruct((s, N_ATT), BF16),
            jax.ShapeDtypeStruct((s, N_LAT), F32),
            jax.ShapeDtypeStruct((s, N_GATE), BF16),
        ],
        scratch_shapes=[pltpu.VMEM((tm, d), BF16)],
        compiler_params=pltpu.CompilerParams(
            dimension_semantics=("arbitrary", "arbitrary"), vmem_limit_bytes=VMEM_LIMIT),
        name="in_proj",
    )(x, g, wa, wb, wg, cs)


def _mla_prep_kernel(cq_ref, ckv_ref, kr_ref, posr_ref, gq_ref, gkv_ref,
                     wq_ref, wqr_ref, wk_ref, wkr_ref, wv_ref, invc_ref,
                     qt_ref, k_ref, vt_ref):
    cqn = _rms(cq_ref[...], gq_ref[...]).astype(BF16)
    ckvn = _rms(ckv_ref[...], gkv_ref[...]).astype(BF16)
    tm = cqn.shape[0]

    qt = _dot_nt(wq_ref[...], cqn)
    qrt = _dot_nt(wqr_ref[...], cqn)
    ang_t = invc_ref[...] * posr_ref[...].astype(F32)
    cos_t = jnp.cos(ang_t)
    sin_t = jnp.sin(ang_t)
    zeros = jnp.zeros((HEAD_PAD - B_NOPE - B_ROPE, tm), BF16)
    for h in range(B_HEADS):
        r0 = h * HEAD_PAD
        qt_ref[r0:r0 + B_NOPE, :] = qt[r0:r0 + B_NOPE].astype(BF16)
        rope = (qt[r0 + B_NOPE:r0 + B_NOPE + B_ROPE] * cos_t
                + qrt[h * B_ROPE:(h + 1) * B_ROPE] * sin_t)
        qt_ref[r0 + B_NOPE:r0 + B_NOPE + B_ROPE, :] = rope.astype(BF16)
        qt_ref[r0 + B_NOPE + B_ROPE:r0 + HEAD_PAD, :] = zeros

    cs = jnp.concatenate([cos_t, sin_t, jnp.zeros((HEAD_PAD - 2 * B_ROPE, tm), F32)], axis=0).T
    u = kr_ref[...] * cs
    k_rope = u + pltpu.roll(u, HEAD_PAD - B_ROPE, 1)
    kfull = _dot(ckvn, wk_ref[...]) + _dot(k_rope.astype(BF16), wkr_ref[...])
    col = lax.broadcasted_iota(jnp.int32, kfull.shape, 1)
    kfull = jnp.where((col & (HEAD_PAD - 1)) == SHIFT_ROW, 1.0, kfull).astype(BF16)
    for h in range(B_HEADS):
        k_ref[h] = kfull[:, h * HEAD_PAD:(h + 1) * HEAD_PAD]

    vt = _dot_nt(wv_ref[...], ckvn)
    pad_row = lax.broadcasted_iota(jnp.int32, (V_ROWS - B_V, tm), 0)
    ones_block = jnp.where(pad_row == 0, 1.0, 0.0).astype(BF16)
    for h in range(B_HEADS):
        vt_ref[0, h * V_ROWS:h * V_ROWS + B_V, :] = vt[h * B_V:(h + 1) * B_V].astype(BF16)
        vt_ref[0, h * V_ROWS + B_V:(h + 1) * V_ROWS, :] = ones_block


def _mla_prep(lat, pos_row, gq, gkv, wq_t, wqr_t, wk_ext, wk_rope, wv_t, inv_col, tm):
    s = lat.shape[0]
    nq = B_HEADS * HEAD_PAD
    nv = B_HEADS * V_ROWS
    const = lambda i: (0, 0)
    return pl.pallas_call(
        _mla_prep_kernel,
        grid=(s // tm,),
        in_specs=[
            pl.BlockSpec((tm, 512), lambda i: (i, 0)),
            pl.BlockSpec((tm, 256), lambda i: (i, 2)),
            pl.BlockSpec((tm, 128), lambda i: (i, 6)),
            pl.BlockSpec((1, tm), lambda i: (0, i)),
            pl.BlockSpec((1, 512), const),
            pl.BlockSpec((1, 256), const),
            pl.BlockSpec(wq_t.shape, const),
            pl.BlockSpec(wqr_t.shape, const),
            pl.BlockSpec(wk_ext.shape, const),
            pl.BlockSpec(wk_rope.shape, const),
            pl.BlockSpec(wv_t.shape, const),
            pl.BlockSpec(inv_col.shape, const),
        ],
        out_specs=[
            pl.BlockSpec((nq, tm), lambda i: (0, i)),
            pl.BlockSpec((B_HEADS, tm, HEAD_PAD), lambda i: (0, i, 0)),
            pl.BlockSpec((1, nv, tm), lambda i: (i, 0, 0)),
        ],
        out_shape=[
            jax.ShapeDtypeStruct((nq, s), BF16),
            jax.ShapeDtypeStruct((B_HEADS, s, HEAD_PAD), BF16),
            jax.ShapeDtypeStruct((s // tm, nv, tm), BF16),
        ],
        compiler_params=pltpu.CompilerParams(
            dimension_semantics=("arbitrary",), vmem_limit_bytes=VMEM_LIMIT),
        name="mla_prep",
    )(lat, lat, lat, pos_row, gq, gkv, wq_t, wqr_t, wk_ext, wk_rope, wv_t, inv_col)


SHIFT_KEYS = 256
SHIFT_ROW = B_NOPE + B_ROPE
L_MIN = 0.5
L_MAX = 2.0 ** 100


SCORE_SLOTS = 2


def _sweep(nkb, scores, update, carry, group):
    def blocks(kb0, carry, last):
        for j in range(group):
            if not (last and j == group - 1):
                scores(kb0 + j + 1, (j + 1) % SCORE_SLOTS)
            carry = update(kb0 + j, j % SCORE_SLOTS, carry)
        return carry

    scores(0, 0)
    if nkb > group:
        carry = lax.fori_loop(0, nkb // group - 1, lambda i, c: blocks(group * i, c, False), carry)
    return blocks(nkb - group, carry, True)


def _mla_attn_kernel(q_ref, qn_ref, k_ref, v_ref, o_ref, s_ref, qs_ref, shift_ref, *, tk, nkb, group):
    tq = q_ref.shape[1]

    def key_block(kb):
        if isinstance(kb, int):
            return k_ref[kb * tk:(kb + 1) * tk, :]
        return k_ref[pl.ds(pl.multiple_of(kb * tk, tk), tk), :]

    def tile_shift(qt_ref):
        s0 = _dot(k_ref[0:SHIFT_KEYS, :], qt_ref[...])
        return jnp.max(s0, axis=0, keepdims=True)

    @pl.when(pl.program_id(1) == 0)
    def _():
        shift_ref[...] = tile_shift(q_ref)

    row = lax.broadcasted_iota(jnp.int32, (16, tq), 0)
    qs_ref[...] = q_ref[...]
    qs_ref[SHIFT_ROW:SHIFT_ROW + 16, :] = jnp.where(row == 0, -shift_ref[...], 0.0).astype(BF16)
    shift_ref[...] = tile_shift(qn_ref)

    def scores_fixed(kb, slot):
        s_ref[slot] = _dot(key_block(kb).astype(F32), qs_ref[...].astype(F32))

    def update_fixed(kb, slot, acc):
        p = jnp.exp2(s_ref[slot]).astype(BF16).astype(F32)
        return acc + _dot(v_ref[kb].astype(F32), p)

    acc = _sweep(nkb, scores_fixed, update_fixed, jnp.zeros((V_ROWS, tq), F32), group)
    l = acc[B_V:B_V + 1]
    l_ok = (l >= L_MIN) & (l < L_MAX)
    acc_ok = jnp.abs(acc) < L_MAX
    n_bad = jnp.sum(jnp.where(l_ok, 0.0, 1.0)) + jnp.sum(jnp.where(acc_ok, 0.0, 1.0))
    o_ref[...] = (acc[:B_V] / l).astype(BF16)

    @pl.when(n_bad != 0.0)
    def _():
        def scores_raw(kb, slot):
            s_ref[slot] = _dot(key_block(kb), q_ref[...])

        def update_online(kb, slot, carry):
            m, acc = carry
            m_new = jnp.maximum(m, jnp.max(s_ref[slot], axis=0, keepdims=True))
            p = jnp.exp2(s_ref[slot] - m_new).astype(BF16)
            acc = acc * jnp.exp2(m - m_new) + _dot(v_ref[kb], p)
            return m_new, acc

        _, acc = _sweep(nkb, scores_raw, update_online,
                        (jnp.full((1, tq), NEG_INF, F32), jnp.zeros((V_ROWS, tq), F32)), SCORE_SLOTS)
        o_ref[...] = (acc[:B_V] / acc[B_V:B_V + 1]).astype(BF16)


def _mla_attn(qt, kfull, vt, tq, tk, group=32):
    s = kfull.shape[1]
    nkb = s // tk
    nq = s // tq
    group = min(group, nkb)
    return pl.pallas_call(
        functools.partial(_mla_attn_kernel, tk=tk, nkb=nkb, group=group),
        grid=(B_HEADS, nq),
        in_specs=[
            pl.BlockSpec((HEAD_PAD, tq), lambda h, i: (h, i)),
            pl.BlockSpec((HEAD_PAD, tq), lambda h, i: (h, jnp.minimum(i + 1, nq - 1))),
            pl.BlockSpec((None, s, HEAD_PAD), lambda h, i: (h, 0, 0)),
            pl.BlockSpec((nkb, V_ROWS, tk), lambda h, i: (0, h, 0)),
        ],
        out_specs=pl.BlockSpec((B_V, tq), lambda h, i: (h, i)),
        out_shape=jax.ShapeDtypeStruct((B_HEADS * B_V, s), BF16),
        scratch_shapes=[pltpu.VMEM((SCORE_SLOTS, tk, tq), F32), pltpu.VMEM((HEAD_PAD, tq), BF16),
                        pltpu.VMEM((1, tq), F32)],
        compiler_params=pltpu.CompilerParams(
            dimension_semantics=("arbitrary", "arbitrary"), vmem_limit_bytes=VMEM_LIMIT),
        name="mla_attn",
    )(qt, qt, kfull, vt)


def _win_attn_kernel(sink_ref, q_ref, kvp_ref, kvc_ref, kvn_ref, pq_ref, pkp_ref, pkc_ref, pkn_ref,
                     o_ref, *, seq, slopes):
    i = pl.program_id(0)
    hd = A_HEAD_DIM
    qt = q_ref[...].astype(F32).T.astype(BF16)
    kv = jnp.concatenate([kvp_ref[...], kvc_ref[...], kvn_ref[...]], axis=0)
    kw = A_KV_HEADS * hd
    vt = kv[:, kw:].astype(F32).T.astype(BF16)
    pk = jnp.concatenate([pkp_ref[...], pkc_ref[...], pkn_ref[...]], axis=0)
    dist = jnp.abs(pk - pq_ref[...]).astype(F32)
    c = lax.broadcasted_iota(jnp.int32, dist.shape, 0)
    r = lax.broadcasted_iota(jnp.int32, dist.shape, 1)
    kglob = (i - 1) * BAND + c
    mask = (jnp.abs(BAND + r - c) <= WINDOW) & (kglob >= 0) & (kglob < seq)
    dist = jnp.where(mask, dist, MASKED_DIST)
    for g in range(A_KV_HEADS):
        heads = range(g * A_GROUP, (g + 1) * A_GROUP)
        qg = jnp.concatenate([qt[h * hd:(h + 1) * hd] for h in heads], axis=1)
        st = _dot(kv[:, g * hd:(g + 1) * hd], qg)
        st = st - jnp.concatenate([slopes[h] * dist for h in heads], axis=1)
        sk = jnp.concatenate([jnp.full((1, BAND), sink_ref[h] * LOG2E, F32) for h in heads], axis=1)
        m = jnp.maximum(jnp.max(st, axis=0, keepdims=True), sk)
        p = jnp.exp2(st - m)
        denom = jnp.sum(p, axis=0, keepdims=True) + jnp.exp2(sk - m)
        og = _dot(vt[g * hd:(g + 1) * hd], p.astype(BF16)) / denom
        for j, h in enumerate(heads):
            o_ref[h * hd:(h + 1) * hd, :] = og[:, j * BAND:(j + 1) * BAND].astype(BF16)


def _win_attn(att, pos_col, pos_row, sink):
    s = att.shape[0]
    nb = s // BAND
    slopes = tuple(float(2.0 ** (-8.0 * (h + 1) / A_HEADS)) * LOG2E for h in range(A_HEADS))
    kvcol = (A_HEADS * A_HEAD_DIM) // (2 * A_KV_HEADS * A_HEAD_DIM)
    prev = lambda i, sk: jnp.maximum(i - 1, 0)
    nxt = lambda i, sk: jnp.minimum(i + 1, nb - 1)
    grid_spec = pltpu.PrefetchScalarGridSpec(
        num_scalar_prefetch=1,
        grid=(nb,),
        in_specs=[
            pl.BlockSpec((BAND, A_HEADS * A_HEAD_DIM), lambda i, sk: (i, 0)),
            pl.BlockSpec((BAND, 512), lambda i, sk: (prev(i, sk), kvcol)),
            pl.BlockSpec((BAND, 512), lambda i, sk: (i, kvcol)),
            pl.BlockSpec((BAND, 512), lambda i, sk: (nxt(i, sk), kvcol)),
            pl.BlockSpec((1, BAND), lambda i, sk: (0, i)),
            pl.BlockSpec((BAND, 1), lambda i, sk: (prev(i, sk), 0)),
            pl.BlockSpec((BAND, 1), lambda i, sk: (i, 0)),
            pl.BlockSpec((BAND, 1), lambda i, sk: (nxt(i, sk), 0)),
        ],
        out_specs=pl.BlockSpec((A_HEADS * A_HEAD_DIM, BAND), lambda i, sk: (0, i)),
    )
    return pl.pallas_call(
        functools.partial(_win_attn_kernel, seq=s, slopes=slopes),
        grid_spec=grid_spec,
        out_shape=jax.ShapeDtypeStruct((A_HEADS * A_HEAD_DIM, s), BF16),
        compiler_params=pltpu.CompilerParams(
            dimension_semantics=("arbitrary",), vmem_limit_bytes=VMEM_LIMIT),
        name="win_attn",
    )(sink, att, att, att, att, pos_row, pos_col, pos_col, pos_col)


def _merge_out_kernel(x_ref, oat_ref, obt_ref, ga_ref, gb_ref, wa_ref, wb_ref, wo_ref, g2_ref,
                      x1_ref, h2_ref):
    ya = _dot_tn(oat_ref[...], wa_ref[...])
    yb = _dot_tn(obt_ref[...], wb_ref[...])
    merged = ga_ref[...].astype(F32) * ya + gb_ref[...].astype(F32) * yb
    x1 = x_ref[...] + _dot(merged.astype(BF16), wo_ref[...])
    x1_ref[...] = x1
    h2_ref[...] = _rms(x1, g2_ref[...]).astype(BF16)


def _merge_out(x, oat, obt, gates, wa, wb, wo, g2, tm=512):
    s, d = x.shape
    const = lambda i: (0, 0)
    return pl.pallas_call(
        _merge_out_kernel,
        grid=(s // tm,),
        in_specs=[
            pl.BlockSpec((tm, d), lambda i: (i, 0)),
            pl.BlockSpec((oat.shape[0], tm), lambda i: (0, i)),
            pl.BlockSpec((obt.shape[0], tm), lambda i: (0, i)),
            pl.BlockSpec((tm, d), lambda i: (i, 0)),
            pl.BlockSpec((tm, d), lambda i: (i, 1)),
            pl.BlockSpec(wa.shape, const, pipeline_mode=pl.Buffered(1)),
            pl.BlockSpec(wb.shape, const, pipeline_mode=pl.Buffered(1)),
            pl.BlockSpec(wo.shape, const, pipeline_mode=pl.Buffered(1)),
            pl.BlockSpec((1, d), const),
        ],
        out_specs=[
            pl.BlockSpec((tm, d), lambda i: (i, 0)),
            pl.BlockSpec((tm, d), lambda i: (i, 0)),
        ],
        out_shape=[
            jax.ShapeDtypeStruct((s, d), F32),
            jax.ShapeDtypeStruct((s, d), BF16),
        ],
        compiler_params=pltpu.CompilerParams(
            dimension_semantics=("arbitrary",), vmem_limit_bytes=MERGE_VMEM_LIMIT),
        name="merge_out",
    )(x, oat, obt, gates, gates, wa, wb, wo, g2)


def _mlp_kernel(h2_ref, x1_ref, w1_ref, w2_ref, gf_ref, o_ref, *, nf, final_norm):
    f = pl.program_id(1)

    @pl.when(f == 0)
    def _():
        o_ref[...] = x1_ref[...]

    u = jnp.maximum(_dot(h2_ref[...], w1_ref[...]), 0.0)
    uu = (u * u).astype(BF16)
    for n in range(0, o_ref.shape[1], MLP_OUT_PIECE):
        o_ref[:, n:n + MLP_OUT_PIECE] += _dot(uu, w2_ref[:, n:n + MLP_OUT_PIECE])

    if final_norm:
        @pl.when(f == nf - 1)
        def _():
            o_ref[...] = _rms(o_ref[...], gf_ref[...])


def _mlp(h2, x1, w1, w2, gf, final_norm, tm=512, tf=1024):
    s, d = x1.shape
    nf = w1.shape[1] // tf
    return pl.pallas_call(
        functools.partial(_mlp_kernel, nf=nf, final_norm=final_norm),
        grid=(s // tm, nf),
        in_specs=[
            pl.BlockSpec((tm, d), lambda i, f: (i, 0)),
            pl.BlockSpec((tm, d), lambda i, f: (i, 0)),
            pl.BlockSpec((d, tf), lambda i, f: (0, f)),
            pl.BlockSpec((tf, d), lambda i, f: (f, 0)),
            pl.BlockSpec((1, d), lambda i, f: (0, 0)),
        ],
        out_specs=pl.BlockSpec((tm, d), lambda i, f: (i, 0)),
        out_shape=jax.ShapeDtypeStruct((s, d), F32),
        compiler_params=pltpu.CompilerParams(
            dimension_semantics=("arbitrary", "arbitrary"), vmem_limit_bytes=VMEM_LIMIT),
        name="mlp",
    )(h2, x1, w1, w2, gf)


def _arrange_w_in(w_in):
    d = w_in.shape[0]
    wt = w_in.T
    n_a = N_ATT + 512
    n_kv = 256 + B_ROPE
    wa = wt[:n_a].astype(BF16)
    kr = wt[n_a + 256:n_a + n_kv]
    kr_rot = jnp.concatenate([-kr[B_HALF:], kr[:B_HALF]], axis=0)
    pad = jnp.zeros((IN_TN - n_kv - B_ROPE, d), w_in.dtype)
    wb = jnp.concatenate([wt[n_a:n_a + n_kv], kr_rot, pad], axis=0).astype(BF16)
    wg = wt[n_a + n_kv:].astype(BF16)
    n_q = A_HEADS * A_HEAD_DIM
    cs = jnp.concatenate([jnp.full((1, n_q), A_HEAD_DIM ** -0.5 * LOG2E, F32),
                          jnp.ones((1, N_ATT - n_q), F32)], axis=1)
    return wa, wb, wg, cs


def _arrange_mla_weights(w_uq, w_uk, w_uv):
    c_q = w_uq.shape[0]
    c_kv = w_uk.shape[0]
    qscale = (B_NOPE + B_ROPE) ** -0.5 * math.log2(math.e)
    wq = w_uq * qscale
    wq_pad = jnp.concatenate(
        [wq, jnp.zeros((c_q, B_HEADS, HEAD_PAD - B_NOPE - B_ROPE), wq.dtype)], axis=-1)
    wq_t = wq_pad.reshape(c_q, B_HEADS * HEAD_PAD).T
    rope = wq[:, :, B_NOPE:]
    rot = jnp.concatenate([-rope[..., B_HALF:], rope[..., :B_HALF]], axis=-1)
    wqr_t = rot.reshape(c_q, B_HEADS * B_ROPE).T
    wk_ext = jnp.concatenate(
        [w_uk, jnp.zeros((c_kv, B_HEADS, HEAD_PAD - B_NOPE), w_uk.dtype)], axis=-1
    ).reshape(c_kv, B_HEADS * HEAD_PAD)
    sel = np.zeros((HEAD_PAD, B_HEADS, HEAD_PAD), np.float32)
    for r in range(B_ROPE):
        sel[r, :, B_NOPE + r] = 1.0
    wk_rope = jnp.asarray(sel.reshape(HEAD_PAD, B_HEADS * HEAD_PAD))
    wv_t = w_uv.reshape(c_kv, B_HEADS * B_V).T
    return (wq_t.astype(BF16), wqr_t.astype(BF16), wk_ext.astype(BF16),
            wk_rope.astype(BF16), wv_t.astype(BF16))


def _rope_inv_freq():
    inv = ROPE_THETA ** (-jnp.arange(B_HALF, dtype=F32) / B_HALF)
    return jnp.concatenate([inv, inv])[:, None]


MLA_TQ = 512
MLA_TK = 512


def kernel(x, positions, attn_norm_g, w_in, a_sink, b_q_norm_g, b_kv_norm_g, b_w_uq, b_w_uk, b_w_uv,
           w_branch_a, w_branch_b, w_out, mlp_norm_g, w_mlp_in, w_mlp_out, final_norm_g):
    b, s, d = x.shape
    depth = w_in.shape[0]
    inv_col = _rope_inv_freq()
    outs = []
    for bi in range(b):
        xb = x[bi]
        pos_col = positions[bi][:, None]
        pos_row = positions[bi][None, :]
        for l in range(depth):
            wa, wb, wg, cs = _arrange_w_in(w_in[l])
            wq_t, wqr_t, wk_ext, wk_rope, wv_t = _arrange_mla_weights(b_w_uq[l], b_w_uk[l], b_w_uv[l])
            att, lat, gates = _in_proj(xb, attn_norm_g[l][None, :], wa, wb, wg, cs)
            qt, kfull, vt = _mla_prep(lat, pos_row, b_q_norm_g[l][None, :], b_kv_norm_g[l][None, :],
                                      wq_t, wqr_t, wk_ext, wk_rope, wv_t, inv_col, MLA_TK)
            obt = _mla_attn(qt, kfull, vt, MLA_TQ, MLA_TK)
            oat = _win_attn(att, pos_col, pos_row, a_sink[l])
            x1, h2 = _merge_out(xb, oat, obt, gates, w_branch_a[l].astype(BF16), w_branch_b[l].astype(BF16),
                                w_out[l].astype(BF16), mlp_norm_g[l][None, :])
            xb = _mlp(h2, x1, w_mlp_in[l].astype(BF16), w_mlp_out[l].astype(BF16),
                      final_norm_g[None, :], final_norm=(l == depth - 1))
        outs.append(xb)
    return outs[0][None] if b == 1 else jnp.stack(outs, axis=0)
```

```python
import functools
import math

import jax
import jax.numpy as jnp
import numpy as np
from jax import lax
from jax.experimental import pallas as pl
from jax.experimental.pallas import tpu as pltpu

F32 = jnp.float32
BF16 = jnp.bfloat16

EPS = 1e-6
NEG_INF = -1e30
LOG2E = math.log2(math.e)
MASKED_DIST = 1e33
ROPE_THETA = 10000.0

A_HEADS = 16
A_KV_HEADS = 4
A_GROUP = A_HEADS // A_KV_HEADS
A_HEAD_DIM = 64
WINDOW = 128
BAND = 128

B_HEADS = 16
B_NOPE = 64
B_ROPE = 32
B_V = 64
V_ROWS = B_V + 16
B_HALF = B_ROPE // 2
HEAD_PAD = 128

VMEM_LIMIT = 52 * 1024 * 1024
MERGE_VMEM_LIMIT = 56 * 1024 * 1024

N_ATT = 1536
N_LAT = 1024
N_GATE = 4096
IN_TN = 512
J_ATT = N_ATT // IN_TN
J_LAT = N_LAT // IN_TN
GATE_TN = 1024
J_GATE = N_GATE // GATE_TN
NORM_ROWS = 256
MLP_OUT_PIECE = 512


def _rms(x, g):
    ms = jnp.mean(x * x, axis=-1, keepdims=True)
    return x * lax.rsqrt(ms + EPS) * g


def _dot(a, b):
    return jnp.dot(a, b, preferred_element_type=F32)


def _dot_nt(a, b):
    return lax.dot_general(a, b, (((1,), (1,)), ((), ())), preferred_element_type=F32)


def _dot_tn(a, b):
    return lax.dot_general(a, b, (((0,), (0,)), ((), ())), preferred_element_type=F32)


def _in_proj_kernel(x_ref, g_ref, wa_ref, wb_ref, wg_ref, cs_ref, att_ref, lat_ref, gate_ref, h_ref):
    j = pl.program_id(1)

    @pl.when(j == 0)
    def _():
        def slab(c, carry):
            rows = pl.ds(pl.multiple_of(c * NORM_ROWS, NORM_ROWS), NORM_ROWS)
            h_ref[rows, :] = _rms(x_ref[rows, :], g_ref[...]).astype(BF16)
            return carry
        lax.fori_loop(0, x_ref.shape[0] // NORM_ROWS, slab, 0)

    @pl.when(j < J_ATT)
    def _():
        att_ref[...] = (_dot_nt(h_ref[...], wa_ref[...]) * cs_ref[...]).astype(BF16)

    @pl.when(j == J_ATT)
    def _():
        lat_ref[...] = _dot_nt(h_ref[...], wa_ref[...])

    @pl.when(j == J_ATT + 1)
    def _():
        lat_ref[...] = _dot_nt(h_ref[...], wb_ref[...])

    @pl.when(j >= J_ATT + J_LAT)
    def _():
        gate_ref[...] = jax.nn.sigmoid(_dot_nt(h_ref[...], wg_ref[...])).astype(BF16)


def _in_proj(x, g, wa, wb, wg, cs, tm=1024):
    s, d = x.shape
    nj = J_ATT + J_LAT + J_GATE
    return pl.pallas_call(
        _in_proj_kernel,
        grid=(s // tm, nj),
        in_specs=[
            pl.BlockSpec((tm, d), lambda i, j: (i, 0)),
            pl.BlockSpec((1, d), lambda i, j: (0, 0)),
            pl.BlockSpec((IN_TN, d), lambda i, j: (jnp.minimum(j, J_ATT), 0)),
            pl.BlockSpec((IN_TN, d), lambda i, j: (0, 0)),
            pl.BlockSpec((GATE_TN, d), lambda i, j: (jnp.clip(j - J_ATT - J_LAT, 0, J_GATE - 1), 0)),
            pl.BlockSpec((1, IN_TN), lambda i, j: (0, jnp.minimum(j, J_ATT - 1))),
        ],
        out_specs=[
            pl.BlockSpec((tm, IN_TN), lambda i, j: (i, jnp.minimum(j, J_ATT - 1))),
            pl.BlockSpec((tm, IN_TN), lambda i, j: (i, jnp.clip(j - J_ATT, 0, J_LAT - 1))),
            pl.BlockSpec((tm, GATE_TN), lambda i, j: (i, jnp.clip(j - J_ATT - J_LAT, 0, J_GATE - 1))),
        ],
        out_shape=[
            jax.ShapeDtypeStruct((s, N_ATT), BF16),
            jax.ShapeDtypeStruct((s, N_LAT), F32),
            jax.ShapeDtypeStruct((s, N_GATE), BF16),
        ],
        scratch_shapes=[pltpu.VMEM((tm, d), BF16)],
        compiler_params=pltpu.CompilerParams(
            dimension_semantics=("arbitrary", "arbitrary"), vmem_limit_bytes=VMEM_LIMIT),
        name="in_proj",
    )(x, g, wa, wb, wg, cs)


def _mla_prep_kernel(cq_ref, ckv_ref, kr_ref, posr_ref, gq_ref, gkv_ref,
                     wq_ref, wqr_ref, wk_ref, wkr_ref, wv_ref, invc_ref,
                     qt_ref, k_ref, vt_ref):
    cqn = _rms(cq_ref[...], gq_ref[...]).astype(BF16)
    ckvn = _rms(ckv_ref[...], gkv_ref[...]).astype(BF16)
    tm = cqn.shape[0]

    qt = _dot_nt(wq_ref[...], cqn)
    qrt = _dot_nt(wqr_ref[...], cqn)
    ang_t = invc_ref[...] * posr_ref[...].astype(F32)
    cos_t = jnp.cos(ang_t)
    sin_t = jnp.sin(ang_t)
    zeros = jnp.zeros((HEAD_PAD - B_NOPE - B_ROPE, tm), BF16)
    for h in range(B_HEADS):
        r0 = h * HEAD_PAD
        qt_ref[r0:r0 + B_NOPE, :] = qt[r0:r0 + B_NOPE].astype(BF16)
        rope = (qt[r0 + B_NOPE:r0 + B_NOPE + B_ROPE] * cos_t
                + qrt[h * B_ROPE:(h + 1) * B_ROPE] * sin_t)
        qt_ref[r0 + B_NOPE:r0 + B_NOPE + B_ROPE, :] = rope.astype(BF16)
        qt_ref[r0 + B_NOPE + B_ROPE:r0 + HEAD_PAD, :] = zeros

    cs = jnp.concatenate([cos_t, sin_t, jnp.zeros((HEAD_PAD - 2 * B_ROPE, tm), F32)], axis=0).T
    u = kr_ref[...] * cs
    k_rope = u + pltpu.roll(u, HEAD_PAD - B_ROPE, 1)
    kfull = _dot(ckvn, wk_ref[...]) + _dot(k_rope.astype(BF16), wkr_ref[...])
    col = lax.broadcasted_iota(jnp.int32, kfull.shape, 1)
    kfull = jnp.where((col & (HEAD_PAD - 1)) == SHIFT_ROW, 1.0, kfull).astype(BF16)
    for h in range(B_HEADS):
        k_ref[h] = kfull[:, h * HEAD_PAD:(h + 1) * HEAD_PAD]

    vt = _dot_nt(wv_ref[...], ckvn)
    pad_row = lax.broadcasted_iota(jnp.int32, (V_ROWS - B_V, tm), 0)
    ones_block = jnp.where(pad_row == 0, 1.0, 0.0).astype(BF16)
    for h in range(B_HEADS):
        vt_ref[0, h * V_ROWS:h * V_ROWS + B_V, :] = vt[h * B_V:(h + 1) * B_V].astype(BF16)
        vt_ref[0, h * V_ROWS + B_V:(h + 1) * V_ROWS, :] = ones_block


def _mla_prep(lat, pos_row, gq, gkv, wq_t, wqr_t, wk_ext, wk_rope, wv_t, inv_col, tm):
    s = lat.shape[0]
    nq = B_HEADS * HEAD_PAD
    nv = B_HEADS * V_ROWS
    const = lambda i: (0, 0)
    return pl.pallas_call(
        _mla_prep_kernel,
        grid=(s // tm,),
        in_specs=[
            pl.BlockSpec((tm, 512), lambda i: (i, 0)),
            pl.BlockSpec((tm, 256), lambda i: (i, 2)),
            pl.BlockSpec((tm, 128), lambda i: (i, 6)),
            pl.BlockSpec((1, tm), lambda i: (0, i)),
            pl.BlockSpec((1, 512), const),
            pl.BlockSpec((1, 256), const),
            pl.BlockSpec(wq_t.shape, const),
            pl.BlockSpec(wqr_t.shape, const),
            pl.BlockSpec(wk_ext.shape, const),
            pl.BlockSpec(wk_rope.shape, const),
            pl.BlockSpec(wv_t.shape, const),
            pl.BlockSpec(inv_col.shape, const),
        ],
        out_specs=[
            pl.BlockSpec((nq, tm), lambda i: (0, i)),
            pl.BlockSpec((B_HEADS, tm, HEAD_PAD), lambda i: (0, i, 0)),
            pl.BlockSpec((1, nv, tm), lambda i: (i, 0, 0)),
        ],
        out_shape=[
            jax.ShapeDtypeStruct((nq, s), BF16),
            jax.ShapeDtypeStruct((B_HEADS, s, HEAD_PAD), BF16),
            jax.ShapeDtypeStruct((s // tm, nv, tm), BF16),
        ],
        compiler_params=pltpu.CompilerParams(
            dimension_semantics=("arbitrary",), vmem_limit_bytes=VMEM_LIMIT),
        name="mla_prep",
    )(lat, lat, lat, pos_row, gq, gkv, wq_t, wqr_t, wk_ext, wk_rope, wv_t, inv_col)


SHIFT_KEYS = 256
SHIFT_ROW = B_NOPE + B_ROPE
L_MIN = 0.5
L_MAX = 2.0 ** 100


SCORE_SLOTS = 2


def _sweep(nkb, scores, update, carry, group):
    def blocks(kb0, carry, last):
        for j in range(group):
            if not (last and j == group - 1):
                scores(kb0 + j + 1, (j + 1) % SCORE_SLOTS)
            carry = update(kb0 + j, j % SCORE_SLOTS, carry)
        return carry

    scores(0, 0)
    if nkb > group:
        carry = lax.fori_loop(0, nkb // group - 1, lambda i, c: blocks(group * i, c, False), carry)
    return blocks(nkb - group, carry, True)


def _mla_attn_kernel(q_ref, qn_ref, k_ref, v_ref, o_ref, bad_ref, s_ref, qs_ref, shift_ref, *, tk, nkb,
                     group):
    tq = q_ref.shape[1]

    def key_block(kb):
        if isinstance(kb, int):
            return k_ref[kb * tk:(kb + 1) * tk, :]
        return k_ref[pl.ds(pl.multiple_of(kb * tk, tk), tk), :]

    def tile_shift(qt_ref):
        s0 = _dot(k_ref[0:SHIFT_KEYS, :], qt_ref[...])
        return jnp.max(s0, axis=0, keepdims=True)

    @pl.when(pl.program_id(1) == 0)
    def _():
        shift_ref[...] = tile_shift(q_ref)

    row = lax.broadcasted_iota(jnp.int32, (16, tq), 0)
    qs_ref[...] = q_ref[...]
    qs_ref[SHIFT_ROW:SHIFT_ROW + 16, :] = jnp.where(row == 0, -shift_ref[...], 0.0).astype(BF16)
    shift_ref[...] = tile_shift(qn_ref)

    def scores_fixed(kb, slot):
        s_ref[slot] = _dot(key_block(kb).astype(F32), qs_ref[...].astype(F32))

    def update_fixed(kb, slot, acc):
        p = jnp.exp2(s_ref[slot]).astype(BF16).astype(F32)
        return acc + _dot(v_ref[kb].astype(F32), p)

    acc = _sweep(nkb, scores_fixed, update_fixed, jnp.zeros((V_ROWS, tq), F32), group)
    l = acc[B_V:B_V + 1]
    l_ok = (l >= L_MIN) & (l < L_MAX)
    acc_ok = jnp.abs(acc) < L_MAX
    n_bad = jnp.sum(jnp.where(l_ok, 0.0, 1.0)) + jnp.sum(jnp.where(acc_ok, 0.0, 1.0))
    o_ref[...] = (acc[:B_V] / l).astype(BF16)
    bad_ref[...] = jnp.full(bad_ref.shape, n_bad, F32)


def _mla_redo_kernel(bad_ref, q_ref, k_ref, v_ref, oin_ref, o_ref, s_ref, *, tk, nkb, nq):
    tq = q_ref.shape[1]
    flagged = bad_ref[pl.program_id(0) * nq + pl.program_id(1)] != 0

    @pl.when(jnp.logical_not(flagged))
    def _():
        o_ref[...] = oin_ref[...]

    @pl.when(flagged)
    def _():
        def scores_raw(kb, slot):
            s_ref[slot] = _dot(k_ref[pl.ds(pl.multiple_of(kb * tk, tk), tk), :], q_ref[...])

        def update_online(kb, slot, carry):
            m, acc = carry
            m_new = jnp.maximum(m, jnp.max(s_ref[slot], axis=0, keepdims=True))
            p = jnp.exp2(s_ref[slot] - m_new).astype(BF16)
            acc = acc * jnp.exp2(m - m_new) + _dot(v_ref[kb], p)
            return m_new, acc

        _, acc = _sweep(nkb, scores_raw, update_online,
                        (jnp.full((1, tq), NEG_INF, F32), jnp.zeros((V_ROWS, tq), F32)), SCORE_SLOTS)
        o_ref[...] = (acc[:B_V] / acc[B_V:B_V + 1]).astype(BF16)


def _mla_attn(qt, kfull, vt, tq, tk, group=32):
    s = kfull.shape[1]
    nkb = s // tk
    nq = s // tq
    group = min(group, nkb)
    obt, flags = pl.pallas_call(
        functools.partial(_mla_attn_kernel, tk=tk, nkb=nkb, group=group),
        grid=(B_HEADS, nq),
        in_specs=[
            pl.BlockSpec((HEAD_PAD, tq), lambda h, i: (h, i)),
            pl.BlockSpec((HEAD_PAD, tq), lambda h, i: (h, jnp.minimum(i + 1, nq - 1))),
            pl.BlockSpec((None, s, HEAD_PAD), lambda h, i: (h, 0, 0)),
            pl.BlockSpec((nkb, V_ROWS, tk), lambda h, i: (0, h, 0)),
        ],
        out_specs=[
            pl.BlockSpec((B_V, tq), lambda h, i: (h, i)),
            pl.BlockSpec((1, 8, 128), lambda h, i: (h * nq + i, 0, 0)),
        ],
        out_shape=[
            jax.ShapeDtypeStruct((B_HEADS * B_V, s), BF16),
            jax.ShapeDtypeStruct((B_HEADS * nq, 8, 128), F32),
        ],
        scratch_shapes=[pltpu.VMEM((SCORE_SLOTS, tk, tq), F32), pltpu.VMEM((HEAD_PAD, tq), BF16),
                        pltpu.VMEM((1, tq), F32)],
        compiler_params=pltpu.CompilerParams(
            dimension_semantics=("arbitrary", "arbitrary"), vmem_limit_bytes=VMEM_LIMIT),
        name="mla_attn",
    )(qt, qt, kfull, vt)
    bad = (flags[:, 0, 0] != 0.0).astype(jnp.int32)

    def redo(operands):
        bad, qt, kfull, vt, obt = operands
        grid_spec = pltpu.PrefetchScalarGridSpec(
            num_scalar_prefetch=1,
            grid=(B_HEADS, nq),
            in_specs=[
                pl.BlockSpec((HEAD_PAD, tq), lambda h, i, b: (h, i)),
                pl.BlockSpec((None, s, HEAD_PAD), lambda h, i, b: (h, 0, 0)),
                pl.BlockSpec((nkb, V_ROWS, tk), lambda h, i, b: (0, h, 0)),
                pl.BlockSpec((B_V, tq), lambda h, i, b: (h, i)),
            ],
            out_specs=pl.BlockSpec((B_V, tq), lambda h, i, b: (h, i)),
            scratch_shapes=[pltpu.VMEM((SCORE_SLOTS, tk, tq), F32)],
        )
        return pl.pallas_call(
            functools.partial(_mla_redo_kernel, tk=tk, nkb=nkb, nq=nq),
            grid_spec=grid_spec,
            out_shape=jax.ShapeDtypeStruct(obt.shape, obt.dtype),
            compiler_params=pltpu.CompilerParams(
                dimension_semantics=("arbitrary", "arbitrary"), vmem_limit_bytes=VMEM_LIMIT),
            name="mla_redo",
        )(bad, qt, kfull, vt, obt)

    return lax.cond(jnp.any(bad != 0), redo, lambda operands: operands[4], (bad, qt, kfull, vt, obt))


def _win_attn_kernel(sink_ref, q_ref, kvp_ref, kvc_ref, kvn_ref, pq_ref, pkp_ref, pkc_ref, pkn_ref,
                     o_ref, *, seq, slopes):
    i = pl.program_id(0)
    hd = A_HEAD_DIM
    qt = q_ref[...].astype(F32).T.astype(BF16)
    kv = jnp.concatenate([kvp_ref[...], kvc_ref[...], kvn_ref[...]], axis=0)
    kw = A_KV_HEADS * hd
    vt = kv[:, kw:].astype(F32).T.astype(BF16)
    pk = jnp.concatenate([pkp_ref[...], pkc_ref[...], pkn_ref[...]], axis=0)
    dist = jnp.abs(pk - pq_ref[...]).astype(F32)
    c = lax.broadcasted_iota(jnp.int32, dist.shape, 0)
    r = lax.broadcasted_iota(jnp.int32, dist.shape, 1)
    kglob = (i - 1) * BAND + c
    mask = (jnp.abs(BAND + r - c) <= WINDOW) & (kglob >= 0) & (kglob < seq)
    dist = jnp.where(mask, dist, MASKED_DIST)
    for g in range(A_KV_HEADS):
        heads = range(g * A_GROUP, (g + 1) * A_GROUP)
        qg = jnp.concatenate([qt[h * hd:(h + 1) * hd] for h in heads], axis=1)
        st = _dot(kv[:, g * hd:(g + 1) * hd], qg)
        st = st - jnp.concatenate([slopes[h] * dist for h in heads], axis=1)
        sk = jnp.concatenate([jnp.full((1, BAND), sink_ref[h] * LOG2E, F32) for h in heads], axis=1)
        m = jnp.maximum(jnp.max(st, axis=0, keepdims=True), sk)
        p = jnp.exp2(st - m)
        denom = jnp.sum(p, axis=0, keepdims=True) + jnp.exp2(sk - m)
        og = _dot(vt[g * hd:(g + 1) * hd], p.astype(BF16)) / denom
        for j, h in enumerate(heads):
            o_ref[h * hd:(h + 1) * hd, :] = og[:, j * BAND:(j + 1) * BAND].astype(BF16)


def _win_attn(att, pos_col, pos_row, sink):
    s = att.shape[0]
    nb = s // BAND
    slopes = tuple(float(2.0 ** (-8.0 * (h + 1) / A_HEADS)) * LOG2E for h in range(A_HEADS))
    kvcol = (A_HEADS * A_HEAD_DIM) // (2 * A_KV_HEADS * A_HEAD_DIM)
    prev = lambda i, sk: jnp.maximum(i - 1, 0)
    nxt = lambda i, sk: jnp.minimum(i + 1, nb - 1)
    grid_spec = pltpu.PrefetchScalarGridSpec(
        num_scalar_prefetch=1,
        grid=(nb,),
        in_specs=[
            pl.BlockSpec((BAND, A_HEADS * A_HEAD_DIM), lambda i, sk: (i, 0)),
            pl.BlockSpec((BAND, 512), lambda i, sk: (prev(i, sk), kvcol)),
            pl.BlockSpec((BAND, 512), lambda i, sk: (i, kvcol)),
            pl.BlockSpec((BAND, 512), lambda i, sk: (nxt(i, sk), kvcol)),
            pl.BlockSpec((1, BAND), lambda i, sk: (0, i)),
            pl.BlockSpec((BAND, 1), lambda i, sk: (prev(i, sk), 0)),
            pl.BlockSpec((BAND, 1), lambda i, sk: (i, 0)),
            pl.BlockSpec((BAND, 1), lambda i, sk: (nxt(i, sk), 0)),
        ],
        out_specs=pl.BlockSpec((A_HEADS * A_HEAD_DIM, BAND), lambda i, sk: (0, i)),
    )
    return pl.pallas_call(
        functools.partial(_win_attn_kernel, seq=s, slopes=slopes),
        grid_spec=grid_spec,
        out_shape=jax.ShapeDtypeStruct((A_HEADS * A_HEAD_DIM, s), BF16),
        compiler_params=pltpu.CompilerParams(
            dimension_semantics=("arbitrary",), vmem_limit_bytes=VMEM_LIMIT),
        name="win_attn",
    )(sink, att, att, att, att, pos_row, pos_col, pos_col, pos_col)


def _merge_out_kernel(x_ref, oat_ref, obt_ref, ga_ref, gb_ref, wa_ref, wb_ref, wo_ref, g2_ref,
                      x1_ref, h2_ref):
    ya = _dot_tn(oat_ref[...], wa_ref[...])
    yb = _dot_tn(obt_ref[...], wb_ref[...])
    merged = ga_ref[...].astype(F32) * ya + gb_ref[...].astype(F32) * yb
    x1 = x_ref[...] + _dot(merged.astype(BF16), wo_ref[...])
    x1_ref[...] = x1
    h2_ref[...] = _rms(x1, g2_ref[...]).astype(BF16)


def _merge_out(x, oat, obt, gates, wa, wb, wo, g2, tm=512):
    s, d = x.shape
    const = lambda i: (0, 0)
    return pl.pallas_call(
        _merge_out_kernel,
        grid=(s // tm,),
        in_specs=[
            pl.BlockSpec((tm, d), lambda i: (i, 0)),
            pl.BlockSpec((oat.shape[0], tm), lambda i: (0, i)),
            pl.BlockSpec((obt.shape[0], tm), lambda i: (0, i)),
            pl.BlockSpec((tm, d), lambda i: (i, 0)),
            pl.BlockSpec((tm, d), lambda i: (i, 1)),
            pl.BlockSpec(wa.shape, const, pipeline_mode=pl.Buffered(1)),
            pl.BlockSpec(wb.shape, const, pipeline_mode=pl.Buffered(1)),
            pl.BlockSpec(wo.shape, const, pipeline_mode=pl.Buffered(1)),
            pl.BlockSpec((1, d), const),
        ],
        out_specs=[
            pl.BlockSpec((tm, d), lambda i: (i, 0)),
            pl.BlockSpec((tm, d), lambda i: (i, 0)),
        ],
        out_shape=[
            jax.ShapeDtypeStruct((s, d), F32),
            jax.ShapeDtypeStruct((s, d), BF16),
        ],
        compiler_params=pltpu.CompilerParams(
            dimension_semantics=("arbitrary",), vmem_limit_bytes=MERGE_VMEM_LIMIT),
        name="merge_out",
    )(x, oat, obt, gates, gates, wa, wb, wo, g2)


def _mlp_kernel(h2_ref, x1_ref, w1_ref, w2_ref, gf_ref, o_ref, *, nf, final_norm):
    f = pl.program_id(1)

    @pl.when(f == 0)
    def _():
        o_ref[...] = x1_ref[...]

    u = jnp.maximum(_dot(h2_ref[...], w1_ref[...]), 0.0)
    uu = (u * u).astype(BF16)
    for n in range(0, o_ref.shape[1], MLP_OUT_PIECE):
        o_ref[:, n:n + MLP_OUT_PIECE] += _dot(uu, w2_ref[:, n:n + MLP_OUT_PIECE])

    if final_norm:
        @pl.when(f == nf - 1)
        def _():
            o_ref[...] = _rms(o_ref[...], gf_ref[...])


def _mlp(h2, x1, w1, w2, gf, final_norm, tm=512, tf=1024):
    s, d = x1.shape
    nf = w1.shape[1] // tf
    return pl.pallas_call(
        functools.partial(_mlp_kernel, nf=nf, final_norm=final_norm),
        grid=(s // tm, nf),
        in_specs=[
            pl.BlockSpec((tm, d), lambda i, f: (i, 0)),
            pl.BlockSpec((tm, d), lambda i, f: (i, 0)),
            pl.BlockSpec((d, tf), lambda i, f: (0, f)),
            pl.BlockSpec((tf, d), lambda i, f: (f, 0)),
            pl.BlockSpec((1, d), lambda i, f: (0, 0)),
        ],
        out_specs=pl.BlockSpec((tm, d), lambda i, f: (i, 0)),
        out_shape=jax.ShapeDtypeStruct((s, d), F32),
        compiler_params=pltpu.CompilerParams(
            dimension_semantics=("arbitrary", "arbitrary"), vmem_limit_bytes=VMEM_LIMIT),
        name="mlp",
    )(h2, x1, w1, w2, gf)


def _arrange_w_in(w_in):
    d = w_in.shape[0]
    wt = w_in.T
    n_a = N_ATT + 512
    n_kv = 256 + B_ROPE
    wa = wt[:n_a].astype(BF16)
    kr = wt[n_a + 256:n_a + n_kv]
    kr_rot = jnp.concatenate([-kr[B_HALF:], kr[:B_HALF]], axis=0)
    pad = jnp.zeros((IN_TN - n_kv - B_ROPE, d), w_in.dtype)
    wb = jnp.concatenate([wt[n_a:n_a + n_kv], kr_rot, pad], axis=0).astype(BF16)
    wg = wt[n_a + n_kv:].astype(BF16)
    n_q = A_HEADS * A_HEAD_DIM
    cs = jnp.concatenate([jnp.full((1, n_q), A_HEAD_DIM ** -0.5 * LOG2E, F32),
                          jnp.ones((1, N_ATT - n_q), F32)], axis=1)
    return wa, wb, wg, cs


def _arrange_mla_weights(w_uq, w_uk, w_uv):
    c_q = w_uq.shape[0]
    c_kv = w_uk.shape[0]
    qscale = (B_NOPE + B_ROPE) ** -0.5 * math.log2(math.e)
    wq = w_uq * qscale
    wq_pad = jnp.concatenate(
        [wq, jnp.zeros((c_q, B_HEADS, HEAD_PAD - B_NOPE - B_ROPE), wq.dtype)], axis=-1)
    wq_t = wq_pad.reshape(c_q, B_HEADS * HEAD_PAD).T
    rope = wq[:, :, B_NOPE:]
    rot = jnp.concatenate([-rope[..., B_HALF:], rope[..., :B_HALF]], axis=-1)
    wqr_t = rot.reshape(c_q, B_HEADS * B_ROPE).T
    wk_ext = jnp.concatenate(
        [w_uk, jnp.zeros((c_kv, B_HEADS, HEAD_PAD - B_NOPE), w_uk.dtype)], axis=-1
    ).reshape(c_kv, B_HEADS * HEAD_PAD)
    sel = np.zeros((HEAD_PAD, B_HEADS, HEAD_PAD), np.float32)
    for r in range(B_ROPE):
        sel[r, :, B_NOPE + r] = 1.0
    wk_rope = jnp.asarray(sel.reshape(HEAD_PAD, B_HEADS * HEAD_PAD))
    wv_t = w_uv.reshape(c_kv, B_HEADS * B_V).T
    return (wq_t.astype(BF16), wqr_t.astype(BF16), wk_ext.astype(BF16),
            wk_rope.astype(BF16), wv_t.astype(BF16))


def _rope_inv_freq():
    inv = ROPE_THETA ** (-jnp.arange(B_HALF, dtype=F32) / B_HALF)
    return jnp.concatenate([inv, inv])[:, None]


MLA_TQ = 512
MLA_TK = 512


def kernel(x, positions, attn_norm_g, w_in, a_sink, b_q_norm_g, b_kv_norm_g, b_w_uq, b_w_uk, b_w_uv,
           w_branch_a, w_branch_b, w_out, mlp_norm_g, w_mlp_in, w_mlp_out, final_norm_g):
    b, s, d = x.shape
    depth = w_in.shape[0]
    inv_col = _rope_inv_freq()
    outs = []
    for bi in range(b):
        xb = x[bi]
        pos_col = positions[bi][:, None]
        pos_row = positions[bi][None, :]
        for l in range(depth):
            wa, wb, wg, cs = _arrange_w_in(w_in[l])
            wq_t, wqr_t, wk_ext, wk_rope, wv_t = _arrange_mla_weights(b_w_uq[l], b_w_uk[l], b_w_uv[l])
            att, lat, gates = _in_proj(xb, attn_norm_g[l][None, :], wa, wb, wg, cs)
            qt, kfull, vt = _mla_prep(lat, pos_row, b_q_norm_g[l][None, :], b_kv_norm_g[l][None, :],
                                      wq_t, wqr_t, wk_ext, wk_rope, wv_t, inv_col, MLA_TK)
            obt = _mla_attn(qt, kfull, vt, MLA_TQ, MLA_TK)
            oat = _win_attn(att, pos_col, pos_row, a_sink[l])
            x1, h2 = _merge_out(xb, oat, obt, gates, w_branch_a[l].astype(BF16), w_branch_b[l].astype(BF16),
                                w_out[l].astype(BF16), mlp_norm_g[l][None, :])
            xb = _mlp(h2, x1, w_mlp_in[l].astype(BF16), w_mlp_out[l].astype(BF16),
                      final_norm_g[None, :], final_norm=(l == depth - 1))
        outs.append(xb)
    return outs[0][None] if b == 1 else jnp.stack(outs, axis=0)
```

```python
import functools
import math

import jax
import jax.numpy as jnp
import numpy as np
from jax import lax
from jax.experimental import pallas as pl
from jax.experimental.pallas import tpu as pltpu

F32 = jnp.float32
BF16 = jnp.bfloat16

EPS = 1e-6
NEG_INF = -1e30
LOG2E = math.log2(math.e)
MASKED_DIST = 1e33
ROPE_THETA = 10000.0

A_HEADS = 16
A_KV_HEADS = 4
A_GROUP = A_HEADS // A_KV_HEADS
A_HEAD_DIM = 64
WINDOW = 128
BAND = 128

B_HEADS = 16
B_NOPE = 64
B_ROPE = 32
B_V = 64
V_ROWS = B_V + 16
B_HALF = B_ROPE // 2
HEAD_PAD = 128

VMEM_LIMIT = 52 * 1024 * 1024
MERGE_VMEM_LIMIT = 56 * 1024 * 1024

N_ATT = 1536
N_LAT = 1024
N_GATE = 4096
IN_TN = 512
J_ATT = N_ATT // IN_TN
J_LAT = N_LAT // IN_TN
GATE_TN = 1024
J_GATE = N_GATE // GATE_TN
NORM_ROWS = 256
MLP_OUT_PIECE = 512


def _rms(x, g):
    ms = jnp.mean(x * x, axis=-1, keepdims=True)
    return x * lax.rsqrt(ms + EPS) * g


def _dot(a, b):
    return jnp.dot(a, b, preferred_element_type=F32)


def _dot_nt(a, b):
    return lax.dot_general(a, b, (((1,), (1,)), ((), ())), preferred_element_type=F32)


def _dot_tn(a, b):
    return lax.dot_general(a, b, (((0,), (0,)), ((), ())), preferred_element_type=F32)


def _in_proj_kernel(x_ref, g_ref, wa_ref, wb_ref, wg_ref, cs_ref, att_ref, lat_ref, gate_ref, h_ref):
    j = pl.program_id(1)

    @pl.when(j == 0)
    def _():
        def slab(c, carry):
            rows = pl.ds(pl.multiple_of(c * NORM_ROWS, NORM_ROWS), NORM_ROWS)
            h_ref[rows, :] = _rms(x_ref[rows, :], g_ref[...]).astype(BF16)
            return carry
        lax.fori_loop(0, x_ref.shape[0] // NORM_ROWS, slab, 0)

    @pl.when(j < J_ATT)
    def _():
        att_ref[...] = (_dot_nt(h_ref[...], wa_ref[...]) * cs_ref[...]).astype(BF16)

    @pl.when(j == J_ATT)
    def _():
        lat_ref[...] = _dot_nt(h_ref[...], wa_ref[...])

    @pl.when(j == J_ATT + 1)
    def _():
        lat_ref[...] = _dot_nt(h_ref[...], wb_ref[...])

    @pl.when(j >= J_ATT + J_LAT)
    def _():
        gate_ref[...] = jax.nn.sigmoid(_dot_nt(h_ref[...], wg_ref[...])).astype(BF16)


def _in_proj(x, g, wa, wb, wg, cs, tm=1024):
    s, d = x.shape
    nj = J_ATT + J_LAT + J_GATE
    return pl.pallas_call(
        _in_proj_kernel,
        grid=(s // tm, nj),
        in_specs=[
            pl.BlockSpec((tm, d), lambda i, j: (i, 0)),
            pl.BlockSpec((1, d), lambda i, j: (0, 0)),
            pl.BlockSpec((IN_TN, d), lambda i, j: (jnp.minimum(j, J_ATT), 0)),
            pl.BlockSpec((IN_TN, d), lambda i, j: (0, 0)),
            pl.BlockSpec((GATE_TN, d), lambda i, j: (jnp.clip(j - J_ATT - J_LAT, 0, J_GATE - 1), 0)),
            pl.BlockSpec((1, IN_TN), lambda i, j: (0, jnp.minimum(j, J_ATT - 1))),
        ],
        out_specs=[
            pl.BlockSpec((tm, IN_TN), lambda i, j: (i, jnp.minimum(j, J_ATT - 1))),
            pl.BlockSpec((tm, IN_TN), lambda i, j: (i, jnp.clip(j - J_ATT, 0, J_LAT - 1))),
            pl.BlockSpec((tm, GATE_TN), lambda i, j: (i, jnp.clip(j - J_ATT - J_LAT, 0, J_GATE - 1))),
        ],
        out_shape=[
            jax.ShapeDtypeStruct((s, N_ATT), BF16),
            jax.ShapeDtypeStruct((s, N_LAT), F32),
            jax.ShapeDtypeStruct((s, N_GATE), BF16),
        ],
        scratch_shapes=[pltpu.VMEM((tm, d), BF16)],
        compiler_params=pltpu.CompilerParams(
            dimension_semantics=("arbitrary", "arbitrary"), vmem_limit_bytes=VMEM_LIMIT),
        name="in_proj",
    )(x, g, wa, wb, wg, cs)


def _mla_prep_kernel(cq_ref, ckv_ref, kr_ref, posr_ref, gq_ref, gkv_ref,
                     wq_ref, wqr_ref, wk_ref, wkr_ref, wv_ref, invc_ref,
                     qt_ref, k_ref, vt_ref):
    cqn = _rms(cq_ref[...], gq_ref[...]).astype(BF16)
    ckvn = _rms(ckv_ref[...], gkv_ref[...]).astype(BF16)
    tm = cqn.shape[0]

    qt = _dot_nt(wq_ref[...], cqn)
    qrt = _dot_nt(wqr_ref[...], cqn)
    ang_t = invc_ref[...] * posr_ref[...].astype(F32)
    cos_t = jnp.cos(ang_t)
    sin_t = jnp.sin(ang_t)
    zeros = jnp.zeros((HEAD_PAD - B_NOPE - B_ROPE, tm), BF16)
    for h in range(B_HEADS):
        r0 = h * HEAD_PAD
        qt_ref[r0:r0 + B_NOPE, :] = qt[r0:r0 + B_NOPE].astype(BF16)
        rope = (qt[r0 + B_NOPE:r0 + B_NOPE + B_ROPE] * cos_t
                + qrt[h * B_ROPE:(h + 1) * B_ROPE] * sin_t)
        qt_ref[r0 + B_NOPE:r0 + B_NOPE + B_ROPE, :] = rope.astype(BF16)
        qt_ref[r0 + B_NOPE + B_ROPE:r0 + HEAD_PAD, :] = zeros

    cs = jnp.concatenate([cos_t, sin_t, jnp.zeros((HEAD_PAD - 2 * B_ROPE, tm), F32)], axis=0).T
    u = kr_ref[...] * cs
    k_rope = u + pltpu.roll(u, HEAD_PAD - B_ROPE, 1)
    kfull = _dot(ckvn, wk_ref[...]) + _dot(k_rope.astype(BF16), wkr_ref[...])
    col = lax.broadcasted_iota(jnp.int32, kfull.shape, 1)
    kfull = jnp.where((col & (HEAD_PAD - 1)) == SHIFT_ROW, 1.0, kfull).astype(BF16)
    for h in range(B_HEADS):
        k_ref[h] = kfull[:, h * HEAD_PAD:(h + 1) * HEAD_PAD]

    vt = _dot_nt(wv_ref[...], ckvn)
    pad_row = lax.broadcasted_iota(jnp.int32, (V_ROWS - B_V, tm), 0)
    ones_block = jnp.where(pad_row == 0, 1.0, 0.0).astype(BF16)
    for h in range(B_HEADS):
        vt_ref[0, h * V_ROWS:h * V_ROWS + B_V, :] = vt[h * B_V:(h + 1) * B_V].astype(BF16)
        vt_ref[0, h * V_ROWS + B_V:(h + 1) * V_ROWS, :] = ones_block


def _mla_prep(lat, pos_row, gq, gkv, wq_t, wqr_t, wk_ext, wk_rope, wv_t, inv_col, tm):
    s = lat.shape[0]
    nq = B_HEADS * HEAD_PAD
    nv = B_HEADS * V_ROWS
    const = lambda i: (0, 0)
    return pl.pallas_call(
        _mla_prep_kernel,
        grid=(s // tm,),
        in_specs=[
            pl.BlockSpec((tm, 512), lambda i: (i, 0)),
            pl.BlockSpec((tm, 256), lambda i: (i, 2)),
            pl.BlockSpec((tm, 128), lambda i: (i, 6)),
            pl.BlockSpec((1, tm), lambda i: (0, i)),
            pl.BlockSpec((1, 512), const),
            pl.BlockSpec((1, 256), const),
            pl.BlockSpec(wq_t.shape, const),
            pl.BlockSpec(wqr_t.shape, const),
            pl.BlockSpec(wk_ext.shape, const),
            pl.BlockSpec(wk_rope.shape, const),
            pl.BlockSpec(wv_t.shape, const),
            pl.BlockSpec(inv_col.shape, const),
        ],
        out_specs=[
            pl.BlockSpec((nq, tm), lambda i: (0, i)),
            pl.BlockSpec((B_HEADS, tm, HEAD_PAD), lambda i: (0, i, 0)),
            pl.BlockSpec((1, nv, tm), lambda i: (i, 0, 0)),
        ],
        out_shape=[
            jax.ShapeDtypeStruct((nq, s), BF16),
            jax.ShapeDtypeStruct((B_HEADS, s, HEAD_PAD), BF16),
            jax.ShapeDtypeStruct((s // tm, nv, tm), BF16),
        ],
        compiler_params=pltpu.CompilerParams(
            dimension_semantics=("arbitrary",), vmem_limit_bytes=VMEM_LIMIT),
        name="mla_prep",
    )(lat, lat, lat, pos_row, gq, gkv, wq_t, wqr_t, wk_ext, wk_rope, wv_t, inv_col)


SHIFT_KEYS = 256
SHIFT_ROW = B_NOPE + B_ROPE
L_MIN = 0.5
L_MAX = 2.0 ** 100


SCORE_SLOTS = 2


def _sweep(nkb, scores, update, carry, group, primed=False):
    def blocks(kb0, carry, last):
        for j in range(group):
            if not (last and j == group - 1):
                scores(kb0 + j + 1, (j + 1) % SCORE_SLOTS)
            carry = update(kb0 + j, j % SCORE_SLOTS, carry)
        return carry

    if not primed:
        scores(0, 0)
    if nkb > group:
        carry = lax.fori_loop(0, nkb // group - 1, lambda i, c: blocks(group * i, c, False), carry)
    return blocks(nkb - group, carry, True)


def _mla_attn_kernel(q_ref, qn_ref, k_ref, v_ref, o_ref, bad_ref, s_ref, qs_ref, *, tk, nkb, group):
    tq = q_ref.shape[1]

    def key_block(kb):
        if isinstance(kb, int):
            return k_ref[kb * tk:(kb + 1) * tk, :]
        return k_ref[pl.ds(pl.multiple_of(kb * tk, tk), tk), :]

    def tile_shift(qt_ref):
        s0 = _dot(k_ref[0:SHIFT_KEYS, :], qt_ref[...])
        return jnp.max(s0, axis=0, keepdims=True)

    def prime(qt_ref):
        row = lax.broadcasted_iota(jnp.int32, (16, tq), 0)
        shift = tile_shift(qt_ref)
        qs_ref[...] = qt_ref[...]
        qs_ref[SHIFT_ROW:SHIFT_ROW + 16, :] = jnp.where(row == 0, -shift, 0.0).astype(BF16)
        scores_fixed(0, 0)

    def scores_fixed(kb, slot):
        s_ref[slot] = _dot(key_block(kb).astype(F32), qs_ref[...].astype(F32))

    def update_fixed(kb, slot, acc):
        p = jnp.exp2(s_ref[slot]).astype(BF16).astype(F32)
        return acc + _dot(v_ref[kb].astype(F32), p)

    @pl.when(pl.program_id(1) == 0)
    def _():
        prime(q_ref)

    acc = _sweep(nkb, scores_fixed, update_fixed, jnp.zeros((V_ROWS, tq), F32), group, primed=True)
    prime(qn_ref)
    l = acc[B_V:B_V + 1]
    l_ok = (l >= L_MIN) & (l < L_MAX)
    acc_ok = jnp.abs(acc) < L_MAX
    n_bad = jnp.sum(jnp.where(l_ok, 0.0, 1.0)) + jnp.sum(jnp.where(acc_ok, 0.0, 1.0))
    o_ref[...] = (acc[:B_V] / l).astype(BF16)
    bad_ref[...] = jnp.full(bad_ref.shape, n_bad, F32)


def _mla_redo_kernel(bad_ref, q_ref, k_ref, v_ref, oin_ref, o_ref, s_ref, *, tk, nkb, nq):
    tq = q_ref.shape[1]
    flagged = bad_ref[pl.program_id(0) * nq + pl.program_id(1)] != 0

    @pl.when(jnp.logical_not(flagged))
    def _():
        o_ref[...] = oin_ref[...]

    @pl.when(flagged)
    def _():
        def scores_raw(kb, slot):
            s_ref[slot] = _dot(k_ref[pl.ds(pl.multiple_of(kb * tk, tk), tk), :], q_ref[...])

        def update_online(kb, slot, carry):
            m, acc = carry
            m_new = jnp.maximum(m, jnp.max(s_ref[slot], axis=0, keepdims=True))
            p = jnp.exp2(s_ref[slot] - m_new).astype(BF16)
            acc = acc * jnp.exp2(m - m_new) + _dot(v_ref[kb], p)
            return m_new, acc

        _, acc = _sweep(nkb, scores_raw, update_online,
                        (jnp.full((1, tq), NEG_INF, F32), jnp.zeros((V_ROWS, tq), F32)), SCORE_SLOTS)
        o_ref[...] = (acc[:B_V] / acc[B_V:B_V + 1]).astype(BF16)


def _mla_attn(qt, kfull, vt, tq, tk, group=32):
    s = kfull.shape[1]
    nkb = s // tk
    nq = s // tq
    group = min(group, nkb)
    obt, flags = pl.pallas_call(
        functools.partial(_mla_attn_kernel, tk=tk, nkb=nkb, group=group),
        grid=(B_HEADS, nq),
        in_specs=[
            pl.BlockSpec((HEAD_PAD, tq), lambda h, i: (h, i)),
            pl.BlockSpec((HEAD_PAD, tq), lambda h, i: (h, jnp.minimum(i + 1, nq - 1))),
            pl.BlockSpec((None, s, HEAD_PAD), lambda h, i: (h, 0, 0)),
            pl.BlockSpec((nkb, V_ROWS, tk), lambda h, i: (0, h, 0)),
        ],
        out_specs=[
            pl.BlockSpec((B_V, tq), lambda h, i: (h, i)),
            pl.BlockSpec((1, 8, 128), lambda h, i: (h * nq + i, 0, 0)),
        ],
        out_shape=[
            jax.ShapeDtypeStruct((B_HEADS * B_V, s), BF16),
            jax.ShapeDtypeStruct((B_HEADS * nq, 8, 128), F32),
        ],
        scratch_shapes=[pltpu.VMEM((SCORE_SLOTS, tk, tq), F32), pltpu.VMEM((HEAD_PAD, tq), BF16)],
        compiler_params=pltpu.CompilerParams(
            dimension_semantics=("arbitrary", "arbitrary"), vmem_limit_bytes=VMEM_LIMIT),
        name="mla_attn",
    )(qt, qt, kfull, vt)
    bad = (flags[:, 0, 0] != 0.0).astype(jnp.int32)

    def redo(operands):
        bad, qt, kfull, vt, obt = operands
        grid_spec = pltpu.PrefetchScalarGridSpec(
            num_scalar_prefetch=1,
            grid=(B_HEADS, nq),
            in_specs=[
                pl.BlockSpec((HEAD_PAD, tq), lambda h, i, b: (h, i)),
                pl.BlockSpec((None, s, HEAD_PAD), lambda h, i, b: (h, 0, 0)),
                pl.BlockSpec((nkb, V_ROWS, tk), lambda h, i, b: (0, h, 0)),
                pl.BlockSpec((B_V, tq), lambda h, i, b: (h, i)),
            ],
            out_specs=pl.BlockSpec((B_V, tq), lambda h, i, b: (h, i)),
            scratch_shapes=[pltpu.VMEM((SCORE_SLOTS, tk, tq), F32)],
        )
        return pl.pallas_call(
            functools.partial(_mla_redo_kernel, tk=tk, nkb=nkb, nq=nq),
            grid_spec=grid_spec,
            out_shape=jax.ShapeDtypeStruct(obt.shape, obt.dtype),
            compiler_params=pltpu.CompilerParams(
                dimension_semantics=("arbitrary", "arbitrary"), vmem_limit_bytes=VMEM_LIMIT),
            name="mla_redo",
        )(bad, qt, kfull, vt, obt)

    return lax.cond(jnp.any(bad != 0), redo, lambda operands: operands[4], (bad, qt, kfull, vt, obt))


def _win_attn_kernel(sink_ref, q_ref, kvp_ref, kvc_ref, kvn_ref, pq_ref, pkp_ref, pkc_ref, pkn_ref,
                     o_ref, *, seq, slopes):
    i = pl.program_id(0)
    hd = A_HEAD_DIM
    qt = q_ref[...].astype(F32).T.astype(BF16)
    kv = jnp.concatenate([kvp_ref[...], kvc_ref[...], kvn_ref[...]], axis=0)
    kw = A_KV_HEADS * hd
    vt = kv[:, kw:].astype(F32).T.astype(BF16)
    pk = jnp.concatenate([pkp_ref[...], pkc_ref[...], pkn_ref[...]], axis=0)
    dist = jnp.abs(pk - pq_ref[...]).astype(F32)
    c = lax.broadcasted_iota(jnp.int32, dist.shape, 0)
    r = lax.broadcasted_iota(jnp.int32, dist.shape, 1)
    kglob = (i - 1) * BAND + c
    mask = (jnp.abs(BAND + r - c) <= WINDOW) & (kglob >= 0) & (kglob < seq)
    dist = jnp.where(mask, dist, MASKED_DIST)
    for g in range(A_KV_HEADS):
        heads = range(g * A_GROUP, (g + 1) * A_GROUP)
        qg = jnp.concatenate([qt[h * hd:(h + 1) * hd] for h in heads], axis=1)
        st = _dot(kv[:, g * hd:(g + 1) * hd], qg)
        st = st - jnp.concatenate([slopes[h] * dist for h in heads], axis=1)
        sk = jnp.concatenate([jnp.full((1, BAND), sink_ref[h] * LOG2E, F32) for h in heads], axis=1)
        m = jnp.maximum(jnp.max(st, axis=0, keepdims=True), sk)
        p = jnp.exp2(st - m)
        denom = jnp.sum(p, axis=0, keepdims=True) + jnp.exp2(sk - m)
        og = _dot(vt[g * hd:(g + 1) * hd], p.astype(BF16)) / denom
        for j, h in enumerate(heads):
            o_ref[h * hd:(h + 1) * hd, :] = og[:, j * BAND:(j + 1) * BAND].astype(BF16)


def _win_attn(att, pos_col, pos_row, sink):
    s = att.shape[0]
    nb = s // BAND
    slopes = tuple(float(2.0 ** (-8.0 * (h + 1) / A_HEADS)) * LOG2E for h in range(A_HEADS))
    kvcol = (A_HEADS * A_HEAD_DIM) // (2 * A_KV_HEADS * A_HEAD_DIM)
    prev = lambda i, sk: jnp.maximum(i - 1, 0)
    nxt = lambda i, sk: jnp.minimum(i + 1, nb - 1)
    grid_spec = pltpu.PrefetchScalarGridSpec(
        num_scalar_prefetch=1,
        grid=(nb,),
        in_specs=[
            pl.BlockSpec((BAND, A_HEADS * A_HEAD_DIM), lambda i, sk: (i, 0)),
            pl.BlockSpec((BAND, 512), lambda i, sk: (prev(i, sk), kvcol)),
            pl.BlockSpec((BAND, 512), lambda i, sk: (i, kvcol)),
            pl.BlockSpec((BAND, 512), lambda i, sk: (nxt(i, sk), kvcol)),
            pl.BlockSpec((1, BAND), lambda i, sk: (0, i)),
            pl.BlockSpec((BAND, 1), lambda i, sk: (prev(i, sk), 0)),
            pl.BlockSpec((BAND, 1), lambda i, sk: (i, 0)),
            pl.BlockSpec((BAND, 1), lambda i, sk: (nxt(i, sk), 0)),
        ],
        out_specs=pl.BlockSpec((A_HEADS * A_HEAD_DIM, BAND), lambda i, sk: (0, i)),
    )
    return pl.pallas_call(
        functools.partial(_win_attn_kernel, seq=s, slopes=slopes),
        grid_spec=grid_spec,
        out_shape=jax.ShapeDtypeStruct((A_HEADS * A_HEAD_DIM, s), BF16),
        compiler_params=pltpu.CompilerParams(
            dimension_semantics=("arbitrary",), vmem_limit_bytes=VMEM_LIMIT),
        name="win_attn",
    )(sink, att, att, att, att, pos_row, pos_col, pos_col, pos_col)


def _merge_out_kernel(x_ref, oat_ref, obt_ref, ga_ref, gb_ref, wa_ref, wb_ref, wo_ref, g2_ref,
                      x1_ref, h2_ref):
    ya = _dot_tn(oat_ref[...], wa_ref[...])
    yb = _dot_tn(obt_ref[...], wb_ref[...])
    merged = ga_ref[...].astype(F32) * ya + gb_ref[...].astype(F32) * yb
    x1 = x_ref[...] + _dot(merged.astype(BF16), wo_ref[...])
    x1_ref[...] = x1
    h2_ref[...] = _rms(x1, g2_ref[...]).astype(BF16)


def _merge_out(x, oat, obt, gates, wa, wb, wo, g2, tm=512):
    s, d = x.shape
    const = lambda i: (0, 0)
    return pl.pallas_call(
        _merge_out_kernel,
        grid=(s // tm,),
        in_specs=[
            pl.BlockSpec((tm, d), lambda i: (i, 0)),
            pl.BlockSpec((oat.shape[0], tm), lambda i: (0, i)),
            pl.BlockSpec((obt.shape[0], tm), lambda i: (0, i)),
            pl.BlockSpec((tm, d), lambda i: (i, 0)),
            pl.BlockSpec((tm, d), lambda i: (i, 1)),
            pl.BlockSpec(wa.shape, const, pipeline_mode=pl.Buffered(1)),
            pl.BlockSpec(wb.shape, const, pipeline_mode=pl.Buffered(1)),
            pl.BlockSpec(wo.shape, const, pipeline_mode=pl.Buffered(1)),
            pl.BlockSpec((1, d), const),
        ],
        out_specs=[
            pl.BlockSpec((tm, d), lambda i: (i, 0)),
            pl.BlockSpec((tm, d), lambda i: (i, 0)),
        ],
        out_shape=[
            jax.ShapeDtypeStruct((s, d), F32),
            jax.ShapeDtypeStruct((s, d), BF16),
        ],
        compiler_params=pltpu.CompilerParams(
            dimension_semantics=("arbitrary",), vmem_limit_bytes=MERGE_VMEM_LIMIT),
        name="merge_out",
    )(x, oat, obt, gates, gates, wa, wb, wo, g2)


def _mlp_kernel(h2_ref, x1_ref, w1_ref, w2_ref, gf_ref, o_ref, *, nf, final_norm):
    f = pl.program_id(1)

    @pl.when(f == 0)
    def _():
        o_ref[...] = x1_ref[...]

    u = jnp.maximum(_dot(h2_ref[...], w1_ref[...]), 0.0)
    uu = (u * u).astype(BF16)
    for n in range(0, o_ref.shape[1], MLP_OUT_PIECE):
        o_ref[:, n:n + MLP_OUT_PIECE] += _dot(uu, w2_ref[:, n:n + MLP_OUT_PIECE])

    if final_norm:
        @pl.when(f == nf - 1)
        def _():
            o_ref[...] = _rms(o_ref[...], gf_ref[...])


def _mlp(h2, x1, w1, w2, gf, final_norm, tm=512, tf=1024):
    s, d = x1.shape
    nf = w1.shape[1] // tf
    return pl.pallas_call(
        functools.partial(_mlp_kernel, nf=nf, final_norm=final_norm),
        grid=(s // tm, nf),
        in_specs=[
            pl.BlockSpec((tm, d), lambda i, f: (i, 0)),
            pl.BlockSpec((tm, d), lambda i, f: (i, 0)),
            pl.BlockSpec((d, tf), lambda i, f: (0, f)),
            pl.BlockSpec((tf, d), lambda i, f: (f, 0)),
            pl.BlockSpec((1, d), lambda i, f: (0, 0)),
        ],
        out_specs=pl.BlockSpec((tm, d), lambda i, f: (i, 0)),
        out_shape=jax.ShapeDtypeStruct((s, d), F32),
        compiler_params=pltpu.CompilerParams(
            dimension_semantics=("arbitrary", "arbitrary"), vmem_limit_bytes=VMEM_LIMIT),
        name="mlp",
    )(h2, x1, w1, w2, gf)


def _arrange_w_in(w_in):
    d = w_in.shape[0]
    wt = w_in.T
    n_a = N_ATT + 512
    n_kv = 256 + B_ROPE
    wa = wt[:n_a].astype(BF16)
    kr = wt[n_a + 256:n_a + n_kv]
    kr_rot = jnp.concatenate([-kr[B_HALF:], kr[:B_HALF]], axis=0)
    pad = jnp.zeros((IN_TN - n_kv - B_ROPE, d), w_in.dtype)
    wb = jnp.concatenate([wt[n_a:n_a + n_kv], kr_rot, pad], axis=0).astype(BF16)
    wg = wt[n_a + n_kv:].astype(BF16)
    n_q = A_HEADS * A_HEAD_DIM
    cs = jnp.concatenate([jnp.full((1, n_q), A_HEAD_DIM ** -0.5 * LOG2E, F32),
                          jnp.ones((1, N_ATT - n_q), F32)], axis=1)
    return wa, wb, wg, cs


def _arrange_mla_weights(w_uq, w_uk, w_uv):
    c_q = w_uq.shape[0]
    c_kv = w_uk.shape[0]
    qscale = (B_NOPE + B_ROPE) ** -0.5 * math.log2(math.e)
    wq = w_uq * qscale
    wq_pad = jnp.concatenate(
        [wq, jnp.zeros((c_q, B_HEADS, HEAD_PAD - B_NOPE - B_ROPE), wq.dtype)], axis=-1)
    wq_t = wq_pad.reshape(c_q, B_HEADS * HEAD_PAD).T
    rope = wq[:, :, B_NOPE:]
    rot = jnp.concatenate([-rope[..., B_HALF:], rope[..., :B_HALF]], axis=-1)
    wqr_t = rot.reshape(c_q, B_HEADS * B_ROPE).T
    wk_ext = jnp.concatenate(
        [w_uk, jnp.zeros((c_kv, B_HEADS, HEAD_PAD - B_NOPE), w_uk.dtype)], axis=-1
    ).reshape(c_kv, B_HEADS * HEAD_PAD)
    sel = np.zeros((HEAD_PAD, B_HEADS, HEAD_PAD), np.float32)
    for r in range(B_ROPE):
        sel[r, :, B_NOPE + r] = 1.0
    wk_rope = jnp.asarray(sel.reshape(HEAD_PAD, B_HEADS * HEAD_PAD))
    wv_t = w_uv.reshape(c_kv, B_HEADS * B_V).T
    return (wq_t.astype(BF16), wqr_t.astype(BF16), wk_ext.astype(BF16),
            wk_rope.astype(BF16), wv_t.astype(BF16))


def _rope_inv_freq():
    inv = ROPE_THETA ** (-jnp.arange(B_HALF, dtype=F32) / B_HALF)
    return jnp.concatenate([inv, inv])[:, None]


MLA_TQ = 512
MLA_TK = 512


def kernel(x, positions, attn_norm_g, w_in, a_sink, b_q_norm_g, b_kv_norm_g, b_w_uq, b_w_uk, b_w_uv,
           w_branch_a, w_branch_b, w_out, mlp_norm_g, w_mlp_in, w_mlp_out, final_norm_g):
    b, s, d = x.shape
    depth = w_in.shape[0]
    inv_col = _rope_inv_freq()
    outs = []
    for bi in range(b):
        xb = x[bi]
        pos_col = positions[bi][:, None]
        pos_row = positions[bi][None, :]
        for l in range(depth):
            wa, wb, wg, cs = _arrange_w_in(w_in[l])
            wq_t, wqr_t, wk_ext, wk_rope, wv_t = _arrange_mla_weights(b_w_uq[l], b_w_uk[l], b_w_uv[l])
            att, lat, gates = _in_proj(xb, attn_norm_g[l][None, :], wa, wb, wg, cs)
            qt, kfull, vt = _mla_prep(lat, pos_row, b_q_norm_g[l][None, :], b_kv_norm_g[l][None, :],
                                      wq_t, wqr_t, wk_ext, wk_rope, wv_t, inv_col, MLA_TK)
            obt = _mla_attn(qt, kfull, vt, MLA_TQ, MLA_TK)
            oat = _win_attn(att, pos_col, pos_row, a_sink[l])
            x1, h2 = _merge_out(xb, oat, obt, gates, w_branch_a[l].astype(BF16), w_branch_b[l].astype(BF16),
                                w_out[l].astype(BF16), mlp_norm_g[l][None, :])
            xb = _mlp(h2, x1, w_mlp_in[l].astype(BF16), w_mlp_out[l].astype(BF16),
                      final_norm_g[None, :], final_norm=(l == depth - 1))
        outs.append(xb)
    return outs[0][None] if b == 1 else jnp.stack(outs, axis=0)
```

```python
import functools
import math

import jax
import jax.numpy as jnp
import numpy as np
from jax import lax
from jax.experimental import pallas as pl
from jax.experimental.pallas import tpu as pltpu

F32 = jnp.float32
BF16 = jnp.bfloat16

EPS = 1e-6
NEG_INF = -1e30
LOG2E = math.log2(math.e)
MASKED_DIST = 1e33
ROPE_THETA = 10000.0

A_HEADS = 16
A_KV_HEADS = 4
A_GROUP = A_HEADS // A_KV_HEADS
A_HEAD_DIM = 64
WINDOW = 128
BAND = 128

B_HEADS = 16
B_NOPE = 64
B_ROPE = 32
B_V = 64
V_ROWS = B_V + 16
B_HALF = B_ROPE // 2
HEAD_PAD = 128

VMEM_LIMIT = 52 * 1024 * 1024
MERGE_VMEM_LIMIT = 56 * 1024 * 1024

N_ATT = 1536
N_LAT = 1024
N_GATE = 4096
IN_TN = 512
J_ATT = N_ATT // IN_TN
J_LAT = N_LAT // IN_TN
GATE_TN = 1024
J_GATE = N_GATE // GATE_TN
NORM_ROWS = 256
MLP_OUT_PIECE = 512


def _rms(x, g):
    ms = jnp.mean(x * x, axis=-1, keepdims=True)
    return x * lax.rsqrt(ms + EPS) * g


def _dot(a, b):
    return jnp.dot(a, b, preferred_element_type=F32)


def _dot_nt(a, b):
    return lax.dot_general(a, b, (((1,), (1,)), ((), ())), preferred_element_type=F32)


def _dot_tn(a, b):
    return lax.dot_general(a, b, (((0,), (0,)), ((), ())), preferred_element_type=F32)


def _in_proj_kernel(x_ref, g_ref, wa_ref, wb_ref, wg_ref, cs_ref, att_ref, lat_ref, gate_ref, h_ref):
    j = pl.program_id(1)

    @pl.when(j == 0)
    def _():
        def slab(c, carry):
            rows = pl.ds(pl.multiple_of(c * NORM_ROWS, NORM_ROWS), NORM_ROWS)
            h_ref[rows, :] = _rms(x_ref[rows, :], g_ref[...]).astype(BF16)
            return carry
        lax.fori_loop(0, x_ref.shape[0] // NORM_ROWS, slab, 0)

    @pl.when(j < J_ATT)
    def _():
        att_ref[...] = (_dot_nt(h_ref[...], wa_ref[...]) * cs_ref[...]).astype(BF16)

    @pl.when(j == J_ATT)
    def _():
        lat_ref[...] = _dot_nt(h_ref[...], wa_ref[...])

    @pl.when(j == J_ATT + 1)
    def _():
        lat_ref[...] = _dot_nt(h_ref[...], wb_ref[...])

    @pl.when(j >= J_ATT + J_LAT)
    def _():
        gate_ref[...] = jax.nn.sigmoid(_dot_nt(h_ref[...], wg_ref[...])).astype(BF16)


def _in_proj(x, g, wa, wb, wg, cs, tm=1024):
    s, d = x.shape
    nj = J_ATT + J_LAT + J_GATE
    return pl.pallas_call(
        _in_proj_kernel,
        grid=(s // tm, nj),
        in_specs=[
            pl.BlockSpec((tm, d), lambda i, j: (i, 0)),
            pl.BlockSpec((1, d), lambda i, j: (0, 0)),
            pl.BlockSpec((IN_TN, d), lambda i, j: (jnp.minimum(j, J_ATT), 0)),
            pl.BlockSpec((IN_TN, d), lambda i, j: (0, 0)),
            pl.BlockSpec((GATE_TN, d), lambda i, j: (jnp.clip(j - J_ATT - J_LAT, 0, J_GATE - 1), 0)),
            pl.BlockSpec((1, IN_TN), lambda i, j: (0, jnp.minimum(j, J_ATT - 1))),
        ],
        out_specs=[
            pl.BlockSpec((tm, IN_TN), lambda i, j: (i, jnp.minimum(j, J_ATT - 1))),
            pl.BlockSpec((tm, IN_TN), lambda i, j: (i, jnp.clip(j - J_ATT, 0, J_LAT - 1))),
            pl.BlockSpec((tm, GATE_TN), lambda i, j: (i, jnp.clip(j - J_ATT - J_LAT, 0, J_GATE - 1))),
        ],
        out_shape=[
            jax.ShapeDtypeStruct((s, N_ATT), BF16),
            jax.ShapeDtypeStruct((s, N_LAT), F32),
            jax.ShapeDtypeStruct((s, N_GATE), BF16),
        ],
        scratch_shapes=[pltpu.VMEM((tm, d), BF16)],
        compiler_params=pltpu.CompilerParams(
            dimension_semantics=("arbitrary", "arbitrary"), vmem_limit_bytes=VMEM_LIMIT),
        name="in_proj",
    )(x, g, wa, wb, wg, cs)


def _mla_prep_kernel(cq_ref, ckv_ref, kr_ref, posr_ref, gq_ref, gkv_ref,
                     wq_ref, wqr_ref, wk_ref, wkr_ref, wv_ref, invc_ref,
                     qt_ref, k_ref, vt_ref):
    cqn = _rms(cq_ref[...], gq_ref[...]).astype(BF16)
    ckvn = _rms(ckv_ref[...], gkv_ref[...]).astype(BF16)
    tm = cqn.shape[0]

    qt = _dot_nt(wq_ref[...], cqn)
    qrt = _dot_nt(wqr_ref[...], cqn)
    ang_t = invc_ref[...] * posr_ref[...].astype(F32)
    cos_t = jnp.cos(ang_t)
    sin_t = jnp.sin(ang_t)
    zeros = jnp.zeros((HEAD_PAD - B_NOPE - B_ROPE, tm), BF16)
    for h in range(B_HEADS):
        r0 = h * HEAD_PAD
        qt_ref[r0:r0 + B_NOPE, :] = qt[r0:r0 + B_NOPE].astype(BF16)
        rope = (qt[r0 + B_NOPE:r0 + B_NOPE + B_ROPE] * cos_t
                + qrt[h * B_ROPE:(h + 1) * B_ROPE] * sin_t)
        qt_ref[r0 + B_NOPE:r0 + B_NOPE + B_ROPE, :] = rope.astype(BF16)
        qt_ref[r0 + B_NOPE + B_ROPE:r0 + HEAD_PAD, :] = zeros

    cs = jnp.concatenate([cos_t, sin_t, jnp.zeros((HEAD_PAD - 2 * B_ROPE, tm), F32)], axis=0).T
    u = kr_ref[...] * cs
    k_rope = u + pltpu.roll(u, HEAD_PAD - B_ROPE, 1)
    kfull = _dot(ckvn, wk_ref[...]) + _dot(k_rope.astype(BF16), wkr_ref[...])
    col = lax.broadcasted_iota(jnp.int32, kfull.shape, 1)
    kfull = jnp.where((col & (HEAD_PAD - 1)) == SHIFT_ROW, 1.0, kfull).astype(BF16)
    for h in range(B_HEADS):
        k_ref[h] = kfull[:, h * HEAD_PAD:(h + 1) * HEAD_PAD]

    vt = _dot_nt(wv_ref[...], ckvn)
    pad_row = lax.broadcasted_iota(jnp.int32, (V_ROWS - B_V, tm), 0)
    ones_block = jnp.where(pad_row == 0, 1.0, 0.0).astype(BF16)
    for h in range(B_HEADS):
        vt_ref[0, h * V_ROWS:h * V_ROWS + B_V, :] = vt[h * B_V:(h + 1) * B_V].astype(BF16)
        vt_ref[0, h * V_ROWS + B_V:(h + 1) * V_ROWS, :] = ones_block


def _mla_prep(lat, pos_row, gq, gkv, wq_t, wqr_t, wk_ext, wk_rope, wv_t, inv_col, tm):
    s = lat.shape[0]
    nq = B_HEADS * HEAD_PAD
    nv = B_HEADS * V_ROWS
    const = lambda i: (0, 0)
    return pl.pallas_call(
        _mla_prep_kernel,
        grid=(s // tm,),
        in_specs=[
            pl.BlockSpec((tm, 512), lambda i: (i, 0)),
            pl.BlockSpec((tm, 256), lambda i: (i, 2)),
            pl.BlockSpec((tm, 128), lambda i: (i, 6)),
            pl.BlockSpec((1, tm), lambda i: (0, i)),
            pl.BlockSpec((1, 512), const),
            pl.BlockSpec((1, 256), const),
            pl.BlockSpec(wq_t.shape, const),
            pl.BlockSpec(wqr_t.shape, const),
            pl.BlockSpec(wk_ext.shape, const),
            pl.BlockSpec(wk_rope.shape, const),
            pl.BlockSpec(wv_t.shape, const),
            pl.BlockSpec(inv_col.shape, const),
        ],
        out_specs=[
            pl.BlockSpec((nq, tm), lambda i: (0, i)),
            pl.BlockSpec((B_HEADS, tm, HEAD_PAD), lambda i: (0, i, 0)),
            pl.BlockSpec((1, nv, tm), lambda i: (i, 0, 0)),
        ],
        out_shape=[
            jax.ShapeDtypeStruct((nq, s), BF16),
            jax.ShapeDtypeStruct((B_HEADS, s, HEAD_PAD), BF16),
            jax.ShapeDtypeStruct((s // tm, nv, tm), BF16),
        ],
        compiler_params=pltpu.CompilerParams(
            dimension_semantics=("arbitrary",), vmem_limit_bytes=VMEM_LIMIT),
        name="mla_prep",
    )(lat, lat, lat, pos_row, gq, gkv, wq_t, wqr_t, wk_ext, wk_rope, wv_t, inv_col)


SHIFT_KEYS = 64
SHIFT_ROW = B_NOPE + B_ROPE
L_MIN = 0.5
L_MAX = 2.0 ** 100


SCORE_SLOTS = 2


def _sweep(nkb, scores, update, carry, group):
    def blocks(kb0, carry, last):
        for j in range(group):
            if not (last and j == group - 1):
                scores(kb0 + j + 1, (j + 1) % SCORE_SLOTS)
            carry = update(kb0 + j, j % SCORE_SLOTS, carry)
        return carry

    scores(0, 0)
    if nkb > group:
        carry = lax.fori_loop(0, nkb // group - 1, lambda i, c: blocks(group * i, c, False), carry)
    return blocks(nkb - group, carry, True)


def _mla_attn_kernel(q_ref, qn_ref, k_ref, v_ref, o_ref, bad_ref, s_ref, qs_ref, shift_ref, *, tk, nkb,
                     group):
    tq = q_ref.shape[1]

    def key_block(kb):
        if isinstance(kb, int):
            return k_ref[kb * tk:(kb + 1) * tk, :]
        return k_ref[pl.ds(pl.multiple_of(kb * tk, tk), tk), :]

    def tile_shift(qt_ref):
        s0 = _dot(k_ref[0:SHIFT_KEYS, :], qt_ref[...])
        return jnp.max(s0, axis=0, keepdims=True)

    @pl.when(pl.program_id(1) == 0)
    def _():
        shift_ref[...] = tile_shift(q_ref)

    row = lax.broadcasted_iota(jnp.int32, (16, tq), 0)
    qs_ref[...] = q_ref[...]
    qs_ref[SHIFT_ROW:SHIFT_ROW + 16, :] = jnp.where(row == 0, -shift_ref[...], 0.0).astype(BF16)
    shift_ref[...] = tile_shift(qn_ref)

    def scores_fixed(kb, slot):
        s_ref[slot] = _dot(key_block(kb).astype(F32), qs_ref[...].astype(F32))

    def update_fixed(kb, slot, acc):
        p = jnp.exp2(s_ref[slot]).astype(BF16).astype(F32)
        return acc + _dot(v_ref[kb].astype(F32), p)

    acc = _sweep(nkb, scores_fixed, update_fixed, jnp.zeros((V_ROWS, tq), F32), group)
    l = acc[B_V:B_V + 1]
    l_ok = (l >= L_MIN) & (l < L_MAX)
    acc_ok = jnp.abs(acc) < L_MAX
    n_bad = jnp.sum(jnp.where(l_ok, 0.0, 1.0)) + jnp.sum(jnp.where(acc_ok, 0.0, 1.0))
    o_ref[...] = (acc[:B_V] / l).astype(BF16)
    bad_ref[...] = jnp.full(bad_ref.shape, n_bad, F32)


def _mla_redo_kernel(bad_ref, q_ref, k_ref, v_ref, oin_ref, o_ref, s_ref, *, tk, nkb, nq):
    tq = q_ref.shape[1]
    flagged = bad_ref[pl.program_id(0) * nq + pl.program_id(1)] != 0

    @pl.when(jnp.logical_not(flagged))
    def _():
        o_ref[...] = oin_ref[...]

    @pl.when(flagged)
    def _():
        def scores_raw(kb, slot):
            s_ref[slot] = _dot(k_ref[pl.ds(pl.multiple_of(kb * tk, tk), tk), :], q_ref[...])

        def update_online(kb, slot, carry):
            m, acc = carry
            m_new = jnp.maximum(m, jnp.max(s_ref[slot], axis=0, keepdims=True))
            p = jnp.exp2(s_ref[slot] - m_new).astype(BF16)
            acc = acc * jnp.exp2(m - m_new) + _dot(v_ref[kb], p)
            return m_new, acc

        _, acc = _sweep(nkb, scores_raw, update_online,
                        (jnp.full((1, tq), NEG_INF, F32), jnp.zeros((V_ROWS, tq), F32)), SCORE_SLOTS)
        o_ref[...] = (acc[:B_V] / acc[B_V:B_V + 1]).astype(BF16)


def _mla_attn(qt, kfull, vt, tq, tk, group=32):
    s = kfull.shape[1]
    nkb = s // tk
    nq = s // tq
    group = min(group, nkb)
    obt, flags = pl.pallas_call(
        functools.partial(_mla_attn_kernel, tk=tk, nkb=nkb, group=group),
        grid=(B_HEADS, nq),
        in_specs=[
            pl.BlockSpec((HEAD_PAD, tq), lambda h, i: (h, i)),
            pl.BlockSpec((HEAD_PAD, tq), lambda h, i: (h, jnp.minimum(i + 1, nq - 1))),
            pl.BlockSpec((None, s, HEAD_PAD), lambda h, i: (h, 0, 0)),
            pl.BlockSpec((nkb, V_ROWS, tk), lambda h, i: (0, h, 0)),
        ],
        out_specs=[
            pl.BlockSpec((B_V, tq), lambda h, i: (h, i)),
            pl.BlockSpec((1, 8, 128), lambda h, i: (h * nq + i, 0, 0)),
        ],
        out_shape=[
            jax.ShapeDtypeStruct((B_HEADS * B_V, s), BF16),
            jax.ShapeDtypeStruct((B_HEADS * nq, 8, 128), F32),
        ],
        scratch_shapes=[pltpu.VMEM((SCORE_SLOTS, tk, tq), F32), pltpu.VMEM((HEAD_PAD, tq), BF16),
                        pltpu.VMEM((1, tq), F32)],
        compiler_params=pltpu.CompilerParams(
            dimension_semantics=("arbitrary", "arbitrary"), vmem_limit_bytes=VMEM_LIMIT),
        name="mla_attn",
    )(qt, qt, kfull, vt)
    bad = (flags[:, 0, 0] != 0.0).astype(jnp.int32)

    def redo(operands):
        bad, qt, kfull, vt, obt = operands
        grid_spec = pltpu.PrefetchScalarGridSpec(
            num_scalar_prefetch=1,
            grid=(B_HEADS, nq),
            in_specs=[
                pl.BlockSpec((HEAD_PAD, tq), lambda h, i, b: (h, i)),
                pl.BlockSpec((None, s, HEAD_PAD), lambda h, i, b: (h, 0, 0)),
                pl.BlockSpec((nkb, V_ROWS, tk), lambda h, i, b: (0, h, 0)),
                pl.BlockSpec((B_V, tq), lambda h, i, b: (h, i)),
            ],
            out_specs=pl.BlockSpec((B_V, tq), lambda h, i, b: (h, i)),
            scratch_shapes=[pltpu.VMEM((SCORE_SLOTS, tk, tq), F32)],
        )
        return pl.pallas_call(
            functools.partial(_mla_redo_kernel, tk=tk, nkb=nkb, nq=nq),
            grid_spec=grid_spec,
            out_shape=jax.ShapeDtypeStruct(obt.shape, obt.dtype),
            compiler_params=pltpu.CompilerParams(
                dimension_semantics=("arbitrary", "arbitrary"), vmem_limit_bytes=VMEM_LIMIT),
            name="mla_redo",
        )(bad, qt, kfull, vt, obt)

    return lax.cond(jnp.any(bad != 0), redo, lambda operands: operands[4], (bad, qt, kfull, vt, obt))


def _win_attn_kernel(sink_ref, q_ref, kvp_ref, kvc_ref, kvn_ref, pq_ref, pkp_ref, pkc_ref, pkn_ref,
                     o_ref, *, seq, slopes):
    i = pl.program_id(0)
    hd = A_HEAD_DIM
    qt = q_ref[...].astype(F32).T.astype(BF16)
    kv = jnp.concatenate([kvp_ref[...], kvc_ref[...], kvn_ref[...]], axis=0)
    kw = A_KV_HEADS * hd
    vt = kv[:, kw:].astype(F32).T.astype(BF16)
    pk = jnp.concatenate([pkp_ref[...], pkc_ref[...], pkn_ref[...]], axis=0)
    dist = jnp.abs(pk - pq_ref[...]).astype(F32)
    c = lax.broadcasted_iota(jnp.int32, dist.shape, 0)
    r = lax.broadcasted_iota(jnp.int32, dist.shape, 1)
    kglob = (i - 1) * BAND + c
    mask = (jnp.abs(BAND + r - c) <= WINDOW) & (kglob >= 0) & (kglob < seq)
    dist = jnp.where(mask, dist, MASKED_DIST)
    for g in range(A_KV_HEADS):
        heads = range(g * A_GROUP, (g + 1) * A_GROUP)
        qg = jnp.concatenate([qt[h * hd:(h + 1) * hd] for h in heads], axis=1)
        st = _dot(kv[:, g * hd:(g + 1) * hd], qg)
        st = st - jnp.concatenate([slopes[h] * dist for h in heads], axis=1)
        sk = jnp.concatenate([jnp.full((1, BAND), sink_ref[h] * LOG2E, F32) for h in heads], axis=1)
        m = jnp.maximum(jnp.max(st, axis=0, keepdims=True), sk)
        p = jnp.exp2(st - m)
        denom = jnp.sum(p, axis=0, keepdims=True) + jnp.exp2(sk - m)
        og = _dot(vt[g * hd:(g + 1) * hd], p.astype(BF16)) / denom
        for j, h in enumerate(heads):
            o_ref[h * hd:(h + 1) * hd, :] = og[:, j * BAND:(j + 1) * BAND].astype(BF16)


def _win_attn(att, pos_col, pos_row, sink):
    s = att.shape[0]
    nb = s // BAND
    slopes = tuple(float(2.0 ** (-8.0 * (h + 1) / A_HEADS)) * LOG2E for h in range(A_HEADS))
    kvcol = (A_HEADS * A_HEAD_DIM) // (2 * A_KV_HEADS * A_HEAD_DIM)
    prev = lambda i, sk: jnp.maximum(i - 1, 0)
    nxt = lambda i, sk: jnp.minimum(i + 1, nb - 1)
    grid_spec = pltpu.PrefetchScalarGridSpec(
        num_scalar_prefetch=1,
        grid=(nb,),
        in_specs=[
            pl.BlockSpec((BAND, A_HEADS * A_HEAD_DIM), lambda i, sk: (i, 0)),
            pl.BlockSpec((BAND, 512), lambda i, sk: (prev(i, sk), kvcol)),
            pl.BlockSpec((BAND, 512), lambda i, sk: (i, kvcol)),
            pl.BlockSpec((BAND, 512), lambda i, sk: (nxt(i, sk), kvcol)),
            pl.BlockSpec((1, BAND), lambda i, sk: (0, i)),
            pl.BlockSpec((BAND, 1), lambda i, sk: (prev(i, sk), 0)),
            pl.BlockSpec((BAND, 1), lambda i, sk: (i, 0)),
            pl.BlockSpec((BAND, 1), lambda i, sk: (nxt(i, sk), 0)),
        ],
        out_specs=pl.BlockSpec((A_HEADS * A_HEAD_DIM, BAND), lambda i, sk: (0, i)),
    )
    return pl.pallas_call(
        functools.partial(_win_attn_kernel, seq=s, slopes=slopes),
        grid_spec=grid_spec,
        out_shape=jax.ShapeDtypeStruct((A_HEADS * A_HEAD_DIM, s), BF16),
        compiler_params=pltpu.CompilerParams(
            dimension_semantics=("arbitrary",), vmem_limit_bytes=VMEM_LIMIT),
        name="win_attn",
    )(sink, att, att, att, att, pos_row, pos_col, pos_col, pos_col)


def _merge_out_kernel(x_ref, oat_ref, obt_ref, ga_ref, gb_ref, wa_ref, wb_ref, wo_ref, g2_ref,
                      x1_ref, h2_ref):
    ya = _dot_tn(oat_ref[...], wa_ref[...])
    yb = _dot_tn(obt_ref[...], wb_ref[...])
    merged = ga_ref[...].astype(F32) * ya + gb_ref[...].astype(F32) * yb
    x1 = x_ref[...] + _dot(merged.astype(BF16), wo_ref[...])
    x1_ref[...] = x1
    h2_ref[...] = _rms(x1, g2_ref[...]).astype(BF16)


def _merge_out(x, oat, obt, gates, wa, wb, wo, g2, tm=512):
    s, d = x.shape
    const = lambda i: (0, 0)
    return pl.pallas_call(
        _merge_out_kernel,
        grid=(s // tm,),
        in_specs=[
            pl.BlockSpec((tm, d), lambda i: (i, 0)),
            pl.BlockSpec((oat.shape[0], tm), lambda i: (0, i)),
            pl.BlockSpec((obt.shape[0], tm), lambda i: (0, i)),
            pl.BlockSpec((tm, d), lambda i: (i, 0)),
            pl.BlockSpec((tm, d), lambda i: (i, 1)),
            pl.BlockSpec(wa.shape, const, pipeline_mode=pl.Buffered(1)),
            pl.BlockSpec(wb.shape, const, pipeline_mode=pl.Buffered(1)),
            pl.BlockSpec(wo.shape, const, pipeline_mode=pl.Buffered(1)),
            pl.BlockSpec((1, d), const),
        ],
        out_specs=[
            pl.BlockSpec((tm, d), lambda i: (i, 0)),
            pl.BlockSpec((tm, d), lambda i: (i, 0)),
        ],
        out_shape=[
            jax.ShapeDtypeStruct((s, d), F32),
            jax.ShapeDtypeStruct((s, d), BF16),
        ],
        compiler_params=pltpu.CompilerParams(
            dimension_semantics=("arbitrary",), vmem_limit_bytes=MERGE_VMEM_LIMIT),
        name="merge_out",
    )(x, oat, obt, gates, gates, wa, wb, wo, g2)


def _mlp_kernel(h2_ref, x1_ref, w1_ref, w2_ref, gf_ref, o_ref, *, nf, final_norm):
    f = pl.program_id(1)

    @pl.when(f == 0)
    def _():
        o_ref[...] = x1_ref[...]

    u = jnp.maximum(_dot(h2_ref[...], w1_ref[...]), 0.0)
    uu = (u * u).astype(BF16)
    for n in range(0, o_ref.shape[1], MLP_OUT_PIECE):
        o_ref[:, n:n + MLP_OUT_PIECE] += _dot(uu, w2_ref[:, n:n + MLP_OUT_PIECE])

    if final_norm:
        @pl.when(f == nf - 1)
        def _():
            o_ref[...] = _rms(o_ref[...], gf_ref[...])


def _mlp(h2, x1, w1, w2, gf, final_norm, tm=512, tf=1024):
    s, d = x1.shape
    nf = w1.shape[1] // tf
    return pl.pallas_call(
        functools.partial(_mlp_kernel, nf=nf, final_norm=final_norm),
        grid=(s // tm, nf),
        in_specs=[
            pl.BlockSpec((tm, d), lambda i, f: (i, 0)),
            pl.BlockSpec((tm, d), lambda i, f: (i, 0)),
            pl.BlockSpec((d, tf), lambda i, f: (0, f)),
            pl.BlockSpec((tf, d), lambda i, f: (f, 0)),
            pl.BlockSpec((1, d), lambda i, f: (0, 0)),
        ],
        out_specs=pl.BlockSpec((tm, d), lambda i, f: (i, 0)),
        out_shape=jax.ShapeDtypeStruct((s, d), F32),
        compiler_params=pltpu.CompilerParams(
            dimension_semantics=("arbitrary", "arbitrary"), vmem_limit_bytes=VMEM_LIMIT),
        name="mlp",
    )(h2, x1, w1, w2, gf)


def _arrange_w_in(w_in):
    d = w_in.shape[0]
    wt = w_in.T
    n_a = N_ATT + 512
    n_kv = 256 + B_ROPE
    wa = wt[:n_a].astype(BF16)
    kr = wt[n_a + 256:n_a + n_kv]
    kr_rot = jnp.concatenate([-kr[B_HALF:], kr[:B_HALF]], axis=0)
    pad = jnp.zeros((IN_TN - n_kv - B_ROPE, d), w_in.dtype)
    wb = jnp.concatenate([wt[n_a:n_a + n_kv], kr_rot, pad], axis=0).astype(BF16)
    wg = wt[n_a + n_kv:].astype(BF16)
    n_q = A_HEADS * A_HEAD_DIM
    cs = jnp.concatenate([jnp.full((1, n_q), A_HEAD_DIM ** -0.5 * LOG2E, F32),
                          jnp.ones((1, N_ATT - n_q), F32)], axis=1)
    return wa, wb, wg, cs


def _arrange_mla_weights(w_uq, w_uk, w_uv):
    c_q = w_uq.shape[0]
    c_kv = w_uk.shape[0]
    qscale = (B_NOPE + B_ROPE) ** -0.5 * math.log2(math.e)
    wq = w_uq * qscale
    wq_pad = jnp.concatenate(
        [wq, jnp.zeros((c_q, B_HEADS, HEAD_PAD - B_NOPE - B_ROPE), wq.dtype)], axis=-1)
    wq_t = wq_pad.reshape(c_q, B_HEADS * HEAD_PAD).T
    rope = wq[:, :, B_NOPE:]
    rot = jnp.concatenate([-rope[..., B_HALF:], rope[..., :B_HALF]], axis=-1)
    wqr_t = rot.reshape(c_q, B_HEADS * B_ROPE).T
    wk_ext = jnp.concatenate(
        [w_uk, jnp.zeros((c_kv, B_HEADS, HEAD_PAD - B_NOPE), w_uk.dtype)], axis=-1
    ).reshape(c_kv, B_HEADS * HEAD_PAD)
    sel = np.zeros((HEAD_PAD, B_HEADS, HEAD_PAD), np.float32)
    for r in range(B_ROPE):
        sel[r, :, B_NOPE + r] = 1.0
    wk_rope = jnp.asarray(sel.reshape(HEAD_PAD, B_HEADS * HEAD_PAD))
    wv_t = w_uv.reshape(c_kv, B_HEADS * B_V).T
    return (wq_t.astype(BF16), wqr_t.astype(BF16), wk_ext.astype(BF16),
            wk_rope.astype(BF16), wv_t.astype(BF16))


def _rope_inv_freq():
    inv = ROPE_THETA ** (-jnp.arange(B_HALF, dtype=F32) / B_HALF)
    return jnp.concatenate([inv, inv])[:, None]


MLA_TQ = 512
MLA_TK = 512


def kernel(x, positions, attn_norm_g, w_in, a_sink, b_q_norm_g, b_kv_norm_g, b_w_uq, b_w_uk, b_w_uv,
           w_branch_a, w_branch_b, w_out, mlp_norm_g, w_mlp_in, w_mlp_out, final_norm_g):
    b, s, d = x.shape
    depth = w_in.shape[0]
    inv_col = _rope_inv_freq()
    outs = []
    for bi in range(b):
        xb = x[bi]
        pos_col = positions[bi][:, None]
        pos_row = positions[bi][None, :]
        for l in range(depth):
            wa, wb, wg, cs = _arrange_w_in(w_in[l])
            wq_t, wqr_t, wk_ext, wk_rope, wv_t = _arrange_mla_weights(b_w_uq[l], b_w_uk[l], b_w_uv[l])
            att, lat, gates = _in_proj(xb, attn_norm_g[l][None, :], wa, wb, wg, cs)
            qt, kfull, vt = _mla_prep(lat, pos_row, b_q_norm_g[l][None, :], b_kv_norm_g[l][None, :],
                                      wq_t, wqr_t, wk_ext, wk_rope, wv_t, inv_col, MLA_TK)
            obt = _mla_attn(qt, kfull, vt, MLA_TQ, MLA_TK)
            oat = _win_attn(att, pos_col, pos_row, a_sink[l])
            x1, h2 = _merge_out(xb, oat, obt, gates, w_branch_a[l].astype(BF16), w_branch_b[l].astype(BF16),
                                w_out[l].astype(BF16), mlp_norm_g[l][None, :])
            xb = _mlp(h2, x1, w_mlp_in[l].astype(BF16), w_mlp_out[l].astype(BF16),
                      final_norm_g[None, :], final_norm=(l == depth - 1))
        outs.append(xb)
    return outs[0][None] if b == 1 else jnp.stack(outs, axis=0)
```

```python
import functools
import math

import jax
import jax.numpy as jnp
import numpy as np
from jax import lax
from jax.experimental import pallas as pl
from jax.experimental.pallas import tpu as pltpu

F32 = jnp.float32
BF16 = jnp.bfloat16

EPS = 1e-6
NEG_INF = -1e30
LOG2E = math.log2(math.e)
MASKED_DIST = 1e33
ROPE_THETA = 10000.0

A_HEADS = 16
A_KV_HEADS = 4
A_GROUP = A_HEADS // A_KV_HEADS
A_HEAD_DIM = 64
WINDOW = 128
BAND = 128

B_HEADS = 16
B_NOPE = 64
B_ROPE = 32
B_V = 64
V_ROWS = B_V + 16
B_HALF = B_ROPE // 2
HEAD_PAD = 128

VMEM_LIMIT = 52 * 1024 * 1024
MERGE_VMEM_LIMIT = 56 * 1024 * 1024

N_ATT = 1536
N_LAT = 1024
N_GATE = 4096
IN_TN = 512
J_ATT = N_ATT // IN_TN
J_LAT = N_LAT // IN_TN
GATE_TN = 1024
J_GATE = N_GATE // GATE_TN
NORM_ROWS = 256
MLP_OUT_PIECE = 512


def _rms(x, g):
    ms = jnp.mean(x * x, axis=-1, keepdims=True)
    return x * lax.rsqrt(ms + EPS) * g


def _dot(a, b):
    return jnp.dot(a, b, preferred_element_type=F32)


def _dot_nt(a, b):
    return lax.dot_general(a, b, (((1,), (1,)), ((), ())), preferred_element_type=F32)


def _dot_tn(a, b):
    return lax.dot_general(a, b, (((0,), (0,)), ((), ())), preferred_element_type=F32)


def _in_proj_kernel(x_ref, g_ref, wa_ref, wb_ref, wg_ref, cs_ref, att_ref, lat_ref, gate_ref, h_ref):
    j = pl.program_id(1)

    @pl.when(j == 0)
    def _():
        def slab(c, carry):
            rows = pl.ds(pl.multiple_of(c * NORM_ROWS, NORM_ROWS), NORM_ROWS)
            h_ref[rows, :] = _rms(x_ref[rows, :], g_ref[...]).astype(BF16)
            return carry
        lax.fori_loop(0, x_ref.shape[0] // NORM_ROWS, slab, 0)

    @pl.when(j < J_ATT)
    def _():
        att_ref[...] = (_dot_nt(h_ref[...], wa_ref[...]) * cs_ref[...]).astype(BF16)

    @pl.when(j == J_ATT)
    def _():
        lat_ref[...] = _dot_nt(h_ref[...], wa_ref[...])

    @pl.when(j == J_ATT + 1)
    def _():
        lat_ref[...] = _dot_nt(h_ref[...], wb_ref[...])

    @pl.when(j >= J_ATT + J_LAT)
    def _():
        gate_ref[...] = jax.nn.sigmoid(_dot_nt(h_ref[...], wg_ref[...])).astype(BF16)


def _in_proj(x, g, wa, wb, wg, cs, tm=1024):
    s, d = x.shape
    nj = J_ATT + J_LAT + J_GATE
    return pl.pallas_call(
        _in_proj_kernel,
        grid=(s // tm, nj),
        in_specs=[
            pl.BlockSpec((tm, d), lambda i, j: (i, 0)),
            pl.BlockSpec((1, d), lambda i, j: (0, 0)),
            pl.BlockSpec((IN_TN, d), lambda i, j: (jnp.minimum(j, J_ATT), 0)),
            pl.BlockSpec((IN_TN, d), lambda i, j: (0, 0)),
            pl.BlockSpec((GATE_TN, d), lambda i, j: (jnp.clip(j - J_ATT - J_LAT, 0, J_GATE - 1), 0)),
            pl.BlockSpec((1, IN_TN), lambda i, j: (0, jnp.minimum(j, J_ATT - 1))),
        ],
        out_specs=[
            pl.BlockSpec((tm, IN_TN), lambda i, j: (i, jnp.minimum(j, J_ATT - 1))),
            pl.BlockSpec((tm, IN_TN), lambda i, j: (i, jnp.clip(j - J_ATT, 0, J_LAT - 1))),
            pl.BlockSpec((tm, GATE_TN), lambda i, j: (i, jnp.clip(j - J_ATT - J_LAT, 0, J_GATE - 1))),
        ],
        out_shape=[
            jax.ShapeDtypeStruct((s, N_ATT), BF16),
            jax.ShapeDtypeStruct((s, N_LAT), F32),
            jax.ShapeDtypeStruct((s, N_GATE), BF16),
        ],
        scratch_shapes=[pltpu.VMEM((tm, d), BF16)],
        compiler_params=pltpu.CompilerParams(
            dimension_semantics=("arbitrary", "arbitrary"), vmem_limit_bytes=VMEM_LIMIT),
        name="in_proj",
    )(x, g, wa, wb, wg, cs)


def _mla_prep_kernel(cq_ref, ckv_ref, kr_ref, posr_ref, gq_ref, gkv_ref,
                     wq_ref, wqr_ref, wk_ref, wkr_ref, wv_ref, invc_ref,
                     qt_ref, k_ref, vt_ref):
    cqn = _rms(cq_ref[...], gq_ref[...]).astype(BF16)
    ckvn = _rms(ckv_ref[...], gkv_ref[...]).astype(BF16)
    tm = cqn.shape[0]

    qt = _dot_nt(wq_ref[...], cqn)
    qrt = _dot_nt(wqr_ref[...], cqn)
    ang_t = invc_ref[...] * posr_ref[...].astype(F32)
    cos_t = jnp.cos(ang_t)
    sin_t = jnp.sin(ang_t)
    zeros = jnp.zeros((HEAD_PAD - B_NOPE - B_ROPE, tm), BF16)
    for h in range(B_HEADS):
        r0 = h * HEAD_PAD
        qt_ref[r0:r0 + B_NOPE, :] = qt[r0:r0 + B_NOPE].astype(BF16)
        rope = (qt[r0 + B_NOPE:r0 + B_NOPE + B_ROPE] * cos_t
                + qrt[h * B_ROPE:(h + 1) * B_ROPE] * sin_t)
        qt_ref[r0 + B_NOPE:r0 + B_NOPE + B_ROPE, :] = rope.astype(BF16)
        qt_ref[r0 + B_NOPE + B_ROPE:r0 + HEAD_PAD, :] = zeros

    cs = jnp.concatenate([cos_t, sin_t, jnp.zeros((HEAD_PAD - 2 * B_ROPE, tm), F32)], axis=0).T
    u = kr_ref[...] * cs
    k_rope = u + pltpu.roll(u, HEAD_PAD - B_ROPE, 1)
    kfull = _dot(ckvn, wk_ref[...]) + _dot(k_rope.astype(BF16), wkr_ref[...])
    col = lax.broadcasted_iota(jnp.int32, kfull.shape, 1)
    kfull = jnp.where((col & (HEAD_PAD - 1)) == SHIFT_ROW, 1.0, kfull).astype(BF16)
    for h in range(B_HEADS):
        k_ref[h] = kfull[:, h * HEAD_PAD:(h + 1) * HEAD_PAD]

    vt = _dot_nt(wv_ref[...], ckvn)
    pad_row = lax.broadcasted_iota(jnp.int32, (V_ROWS - B_V, tm), 0)
    ones_block = jnp.where(pad_row == 0, 1.0, 0.0).astype(BF16)
    for h in range(B_HEADS):
        vt_ref[0, h * V_ROWS:h * V_ROWS + B_V, :] = vt[h * B_V:(h + 1) * B_V].astype(BF16)
        vt_ref[0, h * V_ROWS + B_V:(h + 1) * V_ROWS, :] = ones_block


def _mla_prep(lat, pos_row, gq, gkv, wq_t, wqr_t, wk_ext, wk_rope, wv_t, inv_col, tm):
    s = lat.shape[0]
    nq = B_HEADS * HEAD_PAD
    nv = B_HEADS * V_ROWS
    const = lambda i: (0, 0)
    return pl.pallas_call(
        _mla_prep_kernel,
        grid=(s // tm,),
        in_specs=[
            pl.BlockSpec((tm, 512), lambda i: (i, 0)),
            pl.BlockSpec((tm, 256), lambda i: (i, 2)),
            pl.BlockSpec((tm, 128), lambda i: (i, 6)),
            pl.BlockSpec((1, tm), lambda i: (0, i)),
            pl.BlockSpec((1, 512), const),
            pl.BlockSpec((1, 256), const),
            pl.BlockSpec(wq_t.shape, const),
            pl.BlockSpec(wqr_t.shape, const),
            pl.BlockSpec(wk_ext.shape, const),
            pl.BlockSpec(wk_rope.shape, const),
            pl.BlockSpec(wv_t.shape, const),
            pl.BlockSpec(inv_col.shape, const),
        ],
        out_specs=[
            pl.BlockSpec((nq, tm), lambda i: (0, i)),
            pl.BlockSpec((B_HEADS, tm, HEAD_PAD), lambda i: (0, i, 0)),
            pl.BlockSpec((1, nv, tm), lambda i: (i, 0, 0)),
        ],
        out_shape=[
            jax.ShapeDtypeStruct((nq, s), BF16),
            jax.ShapeDtypeStruct((B_HEADS, s, HEAD_PAD), BF16),
            jax.ShapeDtypeStruct((s // tm, nv, tm), BF16),
        ],
        compiler_params=pltpu.CompilerParams(
            dimension_semantics=("arbitrary",), vmem_limit_bytes=VMEM_LIMIT),
        name="mla_prep",
    )(lat, lat, lat, pos_row, gq, gkv, wq_t, wqr_t, wk_ext, wk_rope, wv_t, inv_col)


SHIFT_KEYS = 64
SHIFT_ROW = B_NOPE + B_ROPE
L_MIN = 0.5
L_MAX = 2.0 ** 100


SCORE_SLOTS = 2


def _sweep(nkb, scores, update, carry, group):
    def blocks(kb0, carry, last):
        for j in range(group):
            if not (last and j == group - 1):
                scores(kb0 + j + 1, (j + 1) % SCORE_SLOTS)
            carry = update(kb0 + j, j % SCORE_SLOTS, carry)
        return carry

    scores(0, 0)
    if nkb > group:
        carry = lax.fori_loop(0, nkb // group - 1, lambda i, c: blocks(group * i, c, False), carry)
    return blocks(nkb - group, carry, True)


def _mla_attn_kernel(q_ref, qn_ref, k_ref, v_ref, o_ref, bad_ref, s_ref, qs_ref, shift_ref, *, tk, nkb,
                     group):
    tq = q_ref.shape[1]

    def key_block(kb):
        if isinstance(kb, int):
            return k_ref[kb * tk:(kb + 1) * tk, :]
        return k_ref[pl.ds(pl.multiple_of(kb * tk, tk), tk), :]

    def tile_shift(qt_ref):
        s0 = _dot(k_ref[0:SHIFT_KEYS, :], qt_ref[...])
        return jnp.max(s0, axis=0, keepdims=True)

    @pl.when(pl.program_id(1) == 0)
    def _():
        shift_ref[...] = tile_shift(q_ref)

    row = lax.broadcasted_iota(jnp.int32, (16, tq), 0)
    qs_ref[...] = q_ref[...]
    qs_ref[SHIFT_ROW:SHIFT_ROW + 16, :] = jnp.where(row == 0, -shift_ref[...], 0.0).astype(BF16)
    shift_ref[...] = tile_shift(qn_ref)

    def scores_fixed(kb, slot):
        s_ref[slot] = _dot(key_block(kb).astype(F32), qs_ref[...].astype(F32))

    def update_fixed(kb, slot, acc):
        p = jnp.exp2(s_ref[slot]).astype(BF16).astype(F32)
        return acc + _dot(v_ref[kb].astype(F32), p)

    acc = _sweep(nkb, scores_fixed, update_fixed, jnp.zeros((V_ROWS, tq), F32), group)
    l = acc[B_V:B_V + 1]
    l_ok = (l >= L_MIN) & (l < L_MAX)
    acc_ok = jnp.abs(acc) < L_MAX
    n_bad = jnp.sum(jnp.where(l_ok, 0.0, 1.0)) + jnp.sum(jnp.where(acc_ok, 0.0, 1.0))
    o_ref[...] = (acc[:B_V] / l).astype(BF16)
    bad_ref[...] = jnp.full(bad_ref.shape, n_bad, F32)


def _mla_redo_kernel(bad_ref, q_ref, k_ref, v_ref, oin_ref, o_ref, s_ref, *, tk, nkb, nq):
    tq = q_ref.shape[1]
    flagged = bad_ref[pl.program_id(0) * nq + pl.program_id(1)] != 0

    @pl.when(jnp.logical_not(flagged))
    def _():
        o_ref[...] = oin_ref[...]

    @pl.when(flagged)
    def _():
        def scores_raw(kb, slot):
            s_ref[slot] = _dot(k_ref[pl.ds(pl.multiple_of(kb * tk, tk), tk), :], q_ref[...])

        def update_online(kb, slot, carry):
            m, acc = carry
            m_new = jnp.maximum(m, jnp.max(s_ref[slot], axis=0, keepdims=True))
            p = jnp.exp2(s_ref[slot] - m_new).astype(BF16)
            acc = acc * jnp.exp2(m - m_new) + _dot(v_ref[kb], p)
            return m_new, acc

        _, acc = _sweep(nkb, scores_raw, update_online,
                        (jnp.full((1, tq), NEG_INF, F32), jnp.zeros((V_ROWS, tq), F32)), SCORE_SLOTS)
        o_ref[...] = (acc[:B_V] / acc[B_V:B_V + 1]).astype(BF16)


def _mla_attn(qt, kfull, vt, tq, tk, group=32):
    s = kfull.shape[1]
    nkb = s // tk
    nq = s // tq
    group = min(group, nkb)
    obt, flags = pl.pallas_call(
        functools.partial(_mla_attn_kernel, tk=tk, nkb=nkb, group=group),
        grid=(B_HEADS, nq),
        in_specs=[
            pl.BlockSpec((HEAD_PAD, tq), lambda h, i: (h, i)),
            pl.BlockSpec((HEAD_PAD, tq), lambda h, i: (h, jnp.minimum(i + 1, nq - 1))),
            pl.BlockSpec((None, s, HEAD_PAD), lambda h, i: (h, 0, 0)),
            pl.BlockSpec((nkb, V_ROWS, tk), lambda h, i: (0, h, 0)),
        ],
        out_specs=[
            pl.BlockSpec((B_V, tq), lambda h, i: (h, i)),
            pl.BlockSpec((1, 8, 128), lambda h, i: (h * nq + i, 0, 0)),
        ],
        out_shape=[
            jax.ShapeDtypeStruct((B_HEADS * B_V, s), BF16),
            jax.ShapeDtypeStruct((B_HEADS * nq, 8, 128), F32),
        ],
        scratch_shapes=[pltpu.VMEM((SCORE_SLOTS, tk, tq), F32), pltpu.VMEM((HEAD_PAD, tq), BF16),
                        pltpu.VMEM((1, tq), F32)],
        compiler_params=pltpu.CompilerParams(
            dimension_semantics=("arbitrary", "arbitrary"), vmem_limit_bytes=VMEM_LIMIT),
        name="mla_attn",
    )(qt, qt, kfull, vt)
    bad = (flags[:, 0, 0] != 0.0).astype(jnp.int32)

    def redo(operands):
        bad, qt, kfull, vt, obt = operands
        grid_spec = pltpu.PrefetchScalarGridSpec(
            num_scalar_prefetch=1,
            grid=(B_HEADS, nq),
            in_specs=[
                pl.BlockSpec((HEAD_PAD, tq), lambda h, i, b: (h, i)),
                pl.BlockSpec((None, s, HEAD_PAD), lambda h, i, b: (h, 0, 0)),
                pl.BlockSpec((nkb, V_ROWS, tk), lambda h, i, b: (0, h, 0)),
                pl.BlockSpec((B_V, tq), lambda h, i, b: (h, i)),
            ],
            out_specs=pl.BlockSpec((B_V, tq), lambda h, i, b: (h, i)),
            scratch_shapes=[pltpu.VMEM((SCORE_SLOTS, tk, tq), F32)],
        )
        return pl.pallas_call(
            functools.partial(_mla_redo_kernel, tk=tk, nkb=nkb, nq=nq),
            grid_spec=grid_spec,
            out_shape=jax.ShapeDtypeStruct(obt.shape, obt.dtype),
            compiler_params=pltpu.CompilerParams(
                dimension_semantics=("arbitrary", "arbitrary"), vmem_limit_bytes=VMEM_LIMIT),
            name="mla_redo",
        )(bad, qt, kfull, vt, obt)

    return lax.cond(jnp.any(bad != 0), redo, lambda operands: operands[4], (bad, qt, kfull, vt, obt))


def _win_attn_kernel(sink_ref, q_ref, kvp_ref, kvc_ref, kvn_ref, pq_ref, pkp_ref, pkc_ref, pkn_ref,
                     o_ref, *, seq, slopes):
    i = pl.program_id(0)
    hd = A_HEAD_DIM
    qt = q_ref[...].astype(F32).T.astype(BF16)
    kv = jnp.concatenate([kvp_ref[...], kvc_ref[...], kvn_ref[...]], axis=0)
    kw = A_KV_HEADS * hd
    vt = kv[:, kw:].astype(F32).T.astype(BF16)
    pk = jnp.concatenate([pkp_ref[...], pkc_ref[...], pkn_ref[...]], axis=0)
    dist = jnp.abs(pk - pq_ref[...]).astype(F32)
    c = lax.broadcasted_iota(jnp.int32, dist.shape, 0)
    r = lax.broadcasted_iota(jnp.int32, dist.shape, 1)
    kglob = (i - 1) * BAND + c
    mask = (jnp.abs(BAND + r - c) <= WINDOW) & (kglob >= 0) & (kglob < seq)
    dist = jnp.where(mask, dist, MASKED_DIST)
    for g, half in [(g, half) for g in range(A_KV_HEADS) for half in range(2)]:
        h0 = g * A_GROUP + half * (A_GROUP // 2)
        heads = range(h0, h0 + A_GROUP // 2)
        qg = jnp.concatenate([qt[h * hd:(h + 1) * hd] for h in heads], axis=1)
        st = _dot(kv[:, g * hd:(g + 1) * hd], qg)
        st = st - jnp.concatenate([slopes[h] * dist for h in heads], axis=1)
        sk = jnp.concatenate([jnp.full((1, BAND), sink_ref[h] * LOG2E, F32) for h in heads], axis=1)
        m = jnp.maximum(jnp.max(st, axis=0, keepdims=True), sk)
        p = jnp.exp2(st - m)
        denom = jnp.sum(p, axis=0, keepdims=True) + jnp.exp2(sk - m)
        og = _dot(vt[g * hd:(g + 1) * hd], p.astype(BF16)) / denom
        for j, h in enumerate(heads):
            o_ref[h * hd:(h + 1) * hd, :] = og[:, j * BAND:(j + 1) * BAND].astype(BF16)


def _win_attn(att, pos_col, pos_row, sink):
    s = att.shape[0]
    nb = s // BAND
    slopes = tuple(float(2.0 ** (-8.0 * (h + 1) / A_HEADS)) * LOG2E for h in range(A_HEADS))
    kvcol = (A_HEADS * A_HEAD_DIM) // (2 * A_KV_HEADS * A_HEAD_DIM)
    prev = lambda i, sk: jnp.maximum(i - 1, 0)
    nxt = lambda i, sk: jnp.minimum(i + 1, nb - 1)
    grid_spec = pltpu.PrefetchScalarGridSpec(
        num_scalar_prefetch=1,
        grid=(nb,),
        in_specs=[
            pl.BlockSpec((BAND, A_HEADS * A_HEAD_DIM), lambda i, sk: (i, 0)),
            pl.BlockSpec((BAND, 512), lambda i, sk: (prev(i, sk), kvcol)),
            pl.BlockSpec((BAND, 512), lambda i, sk: (i, kvcol)),
            pl.BlockSpec((BAND, 512), lambda i, sk: (nxt(i, sk), kvcol)),
            pl.BlockSpec((1, BAND), lambda i, sk: (0, i)),
            pl.BlockSpec((BAND, 1), lambda i, sk: (prev(i, sk), 0)),
            pl.BlockSpec((BAND, 1), lambda i, sk: (i, 0)),
            pl.BlockSpec((BAND, 1), lambda i, sk: (nxt(i, sk), 0)),
        ],
        out_specs=pl.BlockSpec((A_HEADS * A_HEAD_DIM, BAND), lambda i, sk: (0, i)),
    )
    return pl.pallas_call(
        functools.partial(_win_attn_kernel, seq=s, slopes=slopes),
        grid_spec=grid_spec,
        out_shape=jax.ShapeDtypeStruct((A_HEADS * A_HEAD_DIM, s), BF16),
        compiler_params=pltpu.CompilerParams(
            dimension_semantics=("arbitrary",), vmem_limit_bytes=VMEM_LIMIT),
        name="win_attn",
    )(sink, att, att, att, att, pos_row, pos_col, pos_col, pos_col)


def _merge_out_kernel(x_ref, oat_ref, obt_ref, ga_ref, gb_ref, wa_ref, wb_ref, wo_ref, g2_ref,
                      x1_ref, h2_ref):
    ya = _dot_tn(oat_ref[...], wa_ref[...])
    yb = _dot_tn(obt_ref[...], wb_ref[...])
    merged = ga_ref[...].astype(F32) * ya + gb_ref[...].astype(F32) * yb
    x1 = x_ref[...] + _dot(merged.astype(BF16), wo_ref[...])
    x1_ref[...] = x1
    h2_ref[...] = _rms(x1, g2_ref[...]).astype(BF16)


def _merge_out(x, oat, obt, gates, wa, wb, wo, g2, tm=512):
    s, d = x.shape
    const = lambda i: (0, 0)
    return pl.pallas_call(
        _merge_out_kernel,
        grid=(s // tm,),
        in_specs=[
            pl.BlockSpec((tm, d), lambda i: (i, 0)),
            pl.BlockSpec((oat.shape[0], tm), lambda i: (0, i)),
            pl.BlockSpec((obt.shape[0], tm), lambda i: (0, i)),
            pl.BlockSpec((tm, d), lambda i: (i, 0)),
            pl.BlockSpec((tm, d), lambda i: (i, 1)),
            pl.BlockSpec(wa.shape, const, pipeline_mode=pl.Buffered(1)),
            pl.BlockSpec(wb.shape, const, pipeline_mode=pl.Buffered(1)),
            pl.BlockSpec(wo.shape, const, pipeline_mode=pl.Buffered(1)),
            pl.BlockSpec((1, d), const),
        ],
        out_specs=[
            pl.BlockSpec((tm, d), lambda i: (i, 0)),
            pl.BlockSpec((tm, d), lambda i: (i, 0)),
        ],
        out_shape=[
            jax.ShapeDtypeStruct((s, d), F32),
            jax.ShapeDtypeStruct((s, d), BF16),
        ],
        compiler_params=pltpu.CompilerParams(
            dimension_semantics=("arbitrary",), vmem_limit_bytes=MERGE_VMEM_LIMIT),
        name="merge_out",
    )(x, oat, obt, gates, gates, wa, wb, wo, g2)


def _mlp_kernel(h2_ref, x1_ref, w1_ref, w2_ref, gf_ref, o_ref, *, nf, final_norm):
    f = pl.program_id(1)

    @pl.when(f == 0)
    def _():
        o_ref[...] = x1_ref[...]

    u = jnp.maximum(_dot(h2_ref[...], w1_ref[...]), 0.0)
    uu = (u * u).astype(BF16)
    for n in range(0, o_ref.shape[1], MLP_OUT_PIECE):
        o_ref[:, n:n + MLP_OUT_PIECE] += _dot(uu, w2_ref[:, n:n + MLP_OUT_PIECE])

    if final_norm:
        @pl.when(f == nf - 1)
        def _():
            o_ref[...] = _rms(o_ref[...], gf_ref[...])


def _mlp(h2, x1, w1, w2, gf, final_norm, tm=512, tf=1024):
    s, d = x1.shape
    nf = w1.shape[1] // tf
    return pl.pallas_call(
        functools.partial(_mlp_kernel, nf=nf, final_norm=final_norm),
        grid=(s // tm, nf),
        in_specs=[
            pl.BlockSpec((tm, d), lambda i, f: (i, 0)),
            pl.BlockSpec((tm, d), lambda i, f: (i, 0)),
            pl.BlockSpec((d, tf), lambda i, f: (0, f)),
            pl.BlockSpec((tf, d), lambda i, f: (f, 0)),
            pl.BlockSpec((1, d), lambda i, f: (0, 0)),
        ],
        out_specs=pl.BlockSpec((tm, d), lambda i, f: (i, 0)),
        out_shape=jax.ShapeDtypeStruct((s, d), F32),
        compiler_params=pltpu.CompilerParams(
            dimension_semantics=("arbitrary", "arbitrary"), vmem_limit_bytes=VMEM_LIMIT),
        name="mlp",
    )(h2, x1, w1, w2, gf)


def _arrange_w_in(w_in):
    d = w_in.shape[0]
    wt = w_in.T
    n_a = N_ATT + 512
    n_kv = 256 + B_ROPE
    wa = wt[:n_a].astype(BF16)
    kr = wt[n_a + 256:n_a + n_kv]
    kr_rot = jnp.concatenate([-kr[B_HALF:], kr[:B_HALF]], axis=0)
    pad = jnp.zeros((IN_TN - n_kv - B_ROPE, d), w_in.dtype)
    wb = jnp.concatenate([wt[n_a:n_a + n_kv], kr_rot, pad], axis=0).astype(BF16)
    wg = wt[n_a + n_kv:].astype(BF16)
    n_q = A_HEADS * A_HEAD_DIM
    cs = jnp.concatenate([jnp.full((1, n_q), A_HEAD_DIM ** -0.5 * LOG2E, F32),
                          jnp.ones((1, N_ATT - n_q), F32)], axis=1)
    return wa, wb, wg, cs


def _arrange_mla_weights(w_uq, w_uk, w_uv):
    c_q = w_uq.shape[0]
    c_kv = w_uk.shape[0]
    qscale = (B_NOPE + B_ROPE) ** -0.5 * math.log2(math.e)
    wq = w_uq * qscale
    wq_pad = jnp.concatenate(
        [wq, jnp.zeros((c_q, B_HEADS, HEAD_PAD - B_NOPE - B_ROPE), wq.dtype)], axis=-1)
    wq_t = wq_pad.reshape(c_q, B_HEADS * HEAD_PAD).T
    rope = wq[:, :, B_NOPE:]
    rot = jnp.concatenate([-rope[..., B_HALF:], rope[..., :B_HALF]], axis=-1)
    wqr_t = rot.reshape(c_q, B_HEADS * B_ROPE).T
    wk_ext = jnp.concatenate(
        [w_uk, jnp.zeros((c_kv, B_HEADS, HEAD_PAD - B_NOPE), w_uk.dtype)], axis=-1
    ).reshape(c_kv, B_HEADS * HEAD_PAD)
    sel = np.zeros((HEAD_PAD, B_HEADS, HEAD_PAD), np.float32)
    for r in range(B_ROPE):
        sel[r, :, B_NOPE + r] = 1.0
    wk_rope = jnp.asarray(sel.reshape(HEAD_PAD, B_HEADS * HEAD_PAD))
    wv_t = w_uv.reshape(c_kv, B_HEADS * B_V).T
    return (wq_t.astype(BF16), wqr_t.astype(BF16), wk_ext.astype(BF16),
            wk_rope.astype(BF16), wv_t.astype(BF16))


def _rope_inv_freq():
    inv = ROPE_THETA ** (-jnp.arange(B_HALF, dtype=F32) / B_HALF)
    return jnp.concatenate([inv, inv])[:, None]


MLA_TQ = 512
MLA_TK = 512


def kernel(x, positions, attn_norm_g, w_in, a_sink, b_q_norm_g, b_kv_norm_g, b_w_uq, b_w_uk, b_w_uv,
           w_branch_a, w_branch_b, w_out, mlp_norm_g, w_mlp_in, w_mlp_out, final_norm_g):
    b, s, d = x.shape
    depth = w_in.shape[0]
    inv_col = _rope_inv_freq()
    outs = []
    for bi in range(b):
        xb = x[bi]
        pos_col = positions[bi][:, None]
        pos_row = positions[bi][None, :]
        for l in range(depth):
            wa, wb, wg, cs = _arrange_w_in(w_in[l])
            wq_t, wqr_t, wk_ext, wk_rope, wv_t = _arrange_mla_weights(b_w_uq[l], b_w_uk[l], b_w_uv[l])
            att, lat, gates = _in_proj(xb, attn_norm_g[l][None, :], wa, wb, wg, cs)
            qt, kfull, vt = _mla_prep(lat, pos_row, b_q_norm_g[l][None, :], b_kv_norm_g[l][None, :],
                                      wq_t, wqr_t, wk_ext, wk_rope, wv_t, inv_col, MLA_TK)
            obt = _mla_attn(qt, kfull, vt, MLA_TQ, MLA_TK)
            oat = _win_attn(att, pos_col, pos_row, a_sink[l])
            x1, h2 = _merge_out(xb, oat, obt, gates, w_branch_a[l].astype(BF16), w_branch_b[l].astype(BF16),
                                w_out[l].astype(BF16), mlp_norm_g[l][None, :])
            xb = _mlp(h2, x1, w_mlp_in[l].astype(BF16), w_mlp_out[l].astype(BF16),
                      final_norm_g[None, :], final_norm=(l == depth - 1))
        outs.append(xb)
    return outs[0][None] if b == 1 else jnp.stack(outs, axis=0)
```
